```python
import jax, jax.numpy as jnp
from jax import lax
import numpy as np

D_MODEL = 1024
BATCH = 2
SEQ = 8192
DEPTH = 2

GLA_HEADS = 4
GLA_DK = D_MODEL // 2 // GLA_HEADS
GLA_DV = D_MODEL // GLA_HEADS
GLA_GATE_RANK = 16
GLA_TAU = 16.0
GLA_CHUNK = 64

NSA_HEADS = 16
NSA_GROUPS = 4
NSA_HPG = NSA_HEADS // NSA_GROUPS
NSA_HEAD_DIM = D_MODEL // NSA_HEADS
CMP_BLOCK = 32
CMP_STRIDE = 16
CMP_HIDDEN = 256
SLC_BLOCK = 64
SLC_TOP_N = 16
WINDOW = 512
Q_BLOCK = 128

ROPE_THETA = 500000.0
ROT_DIM = NSA_HEAD_DIM // 4

N_EXPERTS = 16
N_EXPERT_GROUPS = 4
EXPERTS_PER_GROUP = N_EXPERTS // N_EXPERT_GROUPS
TOP_K = 2
D_FF_EXPERT = D_MODEL // 4

DN_ALPHA = (2 * DEPTH) ** 0.25
DN_BETA = (8 * DEPTH) ** -0.25
LN_EPS = 1e-5
NEG = -1e30
BIG = 1e30

N_GLA_LAYERS = (DEPTH + 1) // 2
N_NSA_LAYERS = DEPTH // 2
GLA_IN = 2 * GLA_HEADS * GLA_DK + 2 * GLA_HEADS * GLA_DV + GLA_GATE_RANK
NSA_IN = NSA_HEADS * NSA_HEAD_DIM + 6 * NSA_GROUPS * NSA_HEAD_DIM + 3 * NSA_HEADS

kernel_name = 'hybrid_gla_nsa_moe_deepnorm'


def layer_norm(x, g, b):
    xf = x.astype(jnp.float32)
    mu = jnp.mean(xf, axis=-1, keepdims=True)
    var = jnp.mean(jnp.square(xf - mu), axis=-1, keepdims=True)
    y = (xf - mu) * lax.rsqrt(var + LN_EPS) * g.astype(jnp.float32) + b.astype(jnp.float32)
    return y.astype(x.dtype)


def partial_rope(t, pos):
    half = ROT_DIM // 2
    inv_freq = jnp.power(ROPE_THETA, -jnp.arange(half, dtype=jnp.float32) * (2.0 / ROT_DIM))
    ang = pos.astype(jnp.float32)[:, None] * inv_freq[None, :]
    cos = jnp.cos(ang).astype(t.dtype)
    sin = jnp.sin(ang).astype(t.dtype)
    x1 = t[..., :half]
    x2 = t[..., half:ROT_DIM]
    return jnp.concatenate([x1 * cos - x2 * sin, x2 * cos + x1 * sin, t[..., ROT_DIM:]], axis=-1)


def masked_softmax(scores, mask):
    return jax.nn.softmax(jnp.where(mask, scores.astype(jnp.float32), NEG), axis=-1)


def gla_mixer(x, w_in, w_gate_up, b_gate, norm_g, w_out):
    B, S, _ = x.shape
    H, DK, DV, C = GLA_HEADS, GLA_DK, GLA_DV, GLA_CHUNK
    N = S // C
    f32 = jnp.float32
    proj = x @ w_in
    cuts = np.cumsum([H * DK, H * DK, H * DV, GLA_GATE_RANK]).tolist()
    q, k, v, g_low, r = jnp.split(proj, cuts, axis=-1)
    log_a = jax.nn.log_sigmoid((g_low @ w_gate_up + b_gate).astype(f32)) / GLA_TAU

    def chunks(t, d):
        return t.reshape(B, N, C, H, d).transpose(0, 3, 1, 2, 4).astype(f32)

    q = chunks(q, DK) * (DK ** -0.5)
    k = chunks(k, DK)
    v = chunks(v, DV)
    b = jnp.cumsum(chunks(log_a, DK), axis=3)
    b_last = b[:, :, :, -1:, :]
    q_dec = q * jnp.exp(b)
    causal = jnp.tril(jnp.ones((C, C), dtype=bool))
    scores = jnp.einsum('bhnid,bhnjd->bhnij', q_dec, k * jnp.exp(-b))
    o_intra = jnp.einsum('bhnij,bhnjv->bhniv', jnp.where(causal, scores, 0.0), v)
    kv = jnp.einsum('bhncd,bhncv->bhndv', k * jnp.exp(b_last - b), v)
    decay = jnp.exp(b_last[:, :, :, 0, :])

    def step(state, inp):
        dec, kv_n = inp
        return dec[..., None] * state + kv_n, state

    _, states = lax.scan(step, jnp.zeros((B, H, DK, DV), f32),
                         (jnp.moveaxis(decay, 2, 0), jnp.moveaxis(kv, 2, 0)))
    o_inter = jnp.einsum('bhncd,nbhdv->bhncv', q_dec, states)
    o = (o_intra + o_inter).transpose(0, 2, 3, 1, 4).reshape(B, S, H, DV)
    o = o * lax.rsqrt(jnp.mean(jnp.square(o), axis=-1, keepdims=True) + LN_EPS) * norm_g.astype(f32)
    o = o.reshape(B, S, H * DV) * jax.nn.silu(r.astype(f32))
    return o.astype(x.dtype) @ w_out


def compress(t, w1, w2, pe):
    B, G, S, Dh = t.shape
    n_chunk = S // CMP_STRIDE
    span = CMP_BLOCK // CMP_STRIDE
    nc = n_chunk - span + 1
    ch = t.reshape(B, G, n_chunk, CMP_STRIDE, Dh)
    blocks = jnp.concatenate([ch[:, :, i:i + nc] for i in range(span)], axis=3)
    blocks = (blocks + pe).reshape(B, G, nc, CMP_BLOCK * Dh)
    return jax.nn.silu(blocks @ w1) @ w2


def nsa_mixer(x, w_in, w_ck1, w_ck2, w_cv1, w_cv2, cmp_pe, w_out):
    B, S, _ = x.shape
    H, G, HPG, Dh = NSA_HEADS, NSA_GROUPS, NSA_HPG, NSA_HEAD_DIM
    f32 = jnp.float32
    proj = x @ w_in
    cuts = np.cumsum([H * Dh] + [G * Dh] * 6).tolist()
    q, kc, vc, ks, vs, kw, vw, gates = jnp.split(proj, cuts, axis=-1)
    pos = jnp.arange(S, dtype=jnp.int32)

    def kv_heads(t):
        return t.reshape(B, S, G, Dh).transpose(0, 2, 1, 3)

    q = partial_rope(q.reshape(B, S, G, HPG, Dh).transpose(0, 2, 3, 1, 4), pos) * (Dh ** -0.5)
    gates = jax.nn.sigmoid(gates.astype(f32)).reshape(B, S, G, HPG, 3).transpose(0, 2, 3, 1, 4)

    span = CMP_BLOCK // CMP_STRIDE
    nc = S // CMP_STRIDE - span + 1
    cmp_end = jnp.arange(nc, dtype=jnp.int32) * CMP_STRIDE + (CMP_BLOCK - 1)
    k_cmp = partial_rope(compress(kv_heads(kc), w_ck1, w_ck2, cmp_pe), cmp_end)
    v_cmp = compress(kv_heads(vc), w_cv1, w_cv2, cmp_pe).astype(f32)

    ns = S // SLC_BLOCK
    n_sel = min(SLC_TOP_N, ns)
    k_slc = partial_rope(kv_heads(ks), pos).reshape(B, G, ns, SLC_BLOCK, Dh)
    v_slc = kv_heads(vs).reshape(B, G, ns, SLC_BLOCK, Dh)
    ratio = SLC_BLOCK // CMP_STRIDE
    pad_amt = ratio * ns + span - 1 - nc

    pad = ((0, 0), (0, 0), (WINDOW, 0), (0, 0))
    k_win = jnp.pad(partial_rope(kv_heads(kw), pos), pad)
    v_win = jnp.pad(kv_heads(vw), pad)

    bi = jnp.arange(B)[:, None, None, None]
    gi = jnp.arange(G)[None, :, None, None]
    blk_ids = jnp.arange(ns)
    in_blk = jnp.arange(SLC_BLOCK)
    win_off = jnp.arange(WINDOW + Q_BLOCK) - WINDOW

    def query_block(qi):
        s0 = qi * Q_BLOCK
        qpos = s0 + jnp.arange(Q_BLOCK)
        qb = lax.dynamic_slice_in_dim(q, s0, Q_BLOCK, axis=3)
        gb = lax.dynamic_slice_in_dim(gates, s0, Q_BLOCK, axis=3)

        cmask = cmp_end[None, :] <= qpos[:, None]
        p_cmp = masked_softmax(jnp.einsum('bghqd,bgcd->bghqc', qb, k_cmp), cmask)
        p_cmp = p_cmp * jnp.any(cmask, axis=-1)[:, None]
        o_cmp = jnp.einsum('bghqc,bgcd->bghqd', p_cmp, v_cmp)

        p_grp = jnp.pad(p_cmp.sum(axis=2), ((0, 0), (0, 0), (0, 0), (0, pad_amt)))
        imp = p_grp[..., 0:ratio * ns:ratio]
        for m in range(ratio):
            for n in range(span):
                if m + n > 0:
                    imp = imp + p_grp[..., m + n:m + n + ratio * ns:ratio]
        cur = (qpos // SLC_BLOCK)[:, None]
        forced = (blk_ids == 0) | (blk_ids == cur) | (blk_ids == cur - 1)
        imp = jnp.where(blk_ids <= cur, jnp.where(forced, BIG, imp), NEG)
        _, sel = lax.top_k(imp, n_sel)
        k_sel = k_slc[bi, gi, sel].reshape(B, G, Q_BLOCK, n_sel * SLC_BLOCK, Dh)
        v_sel = v_slc[bi, gi, sel].reshape(B, G, Q_BLOCK, n_sel * SLC_BLOCK, Dh)
        kpos = (sel[..., None] * SLC_BLOCK + in_blk).reshape(B, G, Q_BLOCK, n_sel * SLC_BLOCK)
        smask = (kpos <= qpos[:, None])[:, :, None]
        p_slc = masked_softmax(jnp.einsum('bghqd,bgqkd->bghqk', qb, k_sel), smask)
        o_slc = jnp.einsum('bghqk,bgqkd->bghqd', p_slc, v_sel.astype(f32))

        kw_b = lax.dynamic_slice_in_dim(k_win, s0, WINDOW + Q_BLOCK, axis=2)
        vw_b = lax.dynamic_slice_in_dim(v_win, s0, WINDOW + Q_BLOCK, axis=2)
        kwpos = s0 + win_off
        rel = qpos[:, None] - kwpos[None, :]
        wmask = (rel >= 0) & (rel < WINDOW) & (kwpos[None, :] >= 0)
        p_win = masked_softmax(jnp.einsum('bghqd,bgkd->bghqk', qb, kw_b), wmask)
        o_win = jnp.einsum('bghqk,bgkd->bghqd', p_win, vw_b.astype(f32))

        o = gb[..., 0:1] * o_cmp + gb[..., 1:2] * o_slc + gb[..., 2:3] * o_win
        return o.astype(x.dtype)

    outs = lax.map(query_block, jnp.arange(S // Q_BLOCK))
    o = outs.transpose(1, 0, 4, 2, 3, 5).reshape(B, S, H * Dh)
    return o @ w_out


def moe_ffn(x, w_router, b_router, w_gate, w_up, w_down):
    B, S, D = x.shape
    xt = x.reshape(B * S, D)
    n = xt.shape[0]
    s = jax.nn.sigmoid((xt @ w_router).astype(jnp.float32))
    biased = s + b_router.astype(jnp.float32)
    grp_score = lax.top_k(biased.reshape(n, N_EXPERT_GROUPS, EXPERTS_PER_GROUP), TOP_K)[0].sum(-1)
    grp = jnp.argmax(grp_score, axis=-1)
    in_grp = (jnp.arange(N_EXPERTS) // EXPERTS_PER_GROUP)[None, :] == grp[:, None]
    _, top_e = lax.top_k(jnp.where(in_grp, biased, NEG), TOP_K)
    w = jnp.take_along_axis(s, top_e, axis=-1)
    w = w / jnp.sum(w, axis=-1, keepdims=True)
    gate = jnp.zeros((n, N_EXPERTS), jnp.float32).at[jnp.arange(n)[:, None], top_e].set(w)
    h = jax.nn.silu(jnp.einsum('nd,edf->nef', xt, w_gate)) * jnp.einsum('nd,edf->nef', xt, w_up)
    h = h * gate[:, :, None].astype(h.dtype)
    return jnp.einsum('nef,efd->nd', h, w_down).reshape(B, S, D)


def setup_inputs(seed: int = 0) -> dict:
    key = jax.random.key(seed)
    ks = jax.random.split(key, 20)
    f32 = jnp.float32

    def nrm(k, shape, scale):
        return jax.random.normal(k, shape, f32) * scale

    D = D_MODEL
    nG, nN = N_GLA_LAYERS, N_NSA_LAYERS
    cmp_in = CMP_BLOCK * NSA_HEAD_DIM
    return {
        'x': nrm(ks[0], (BATCH, SEQ, D), 1.0),
        'gla_w_in': nrm(ks[1], (nG, D, GLA_IN), D ** -0.5),
        'gla_w_gate_up': nrm(ks[2], (nG, GLA_GATE_RANK, GLA_HEADS * GLA_DK), GLA_GATE_RANK ** -0.5),
        'gla_b_gate': nrm(ks[3], (nG, GLA_HEADS * GLA_DK), 0.1),
        'gla_norm_g': 1.0 + nrm(ks[4], (nG, GLA_DV), 0.02),
        'gla_w_out': nrm(ks[5], (nG, GLA_HEADS * GLA_DV, D), DN_BETA * (GLA_HEADS * GLA_DV) ** -0.5),
        'nsa_w_in': nrm(ks[6], (nN, D, NSA_IN), D ** -0.5),
        'nsa_w_cmp_k1': nrm(ks[7], (nN, cmp_in, CMP_HIDDEN), cmp_in ** -0.5),
        'nsa_w_cmp_k2': nrm(ks[8], (nN, CMP_HIDDEN, NSA_HEAD_DIM), CMP_HIDDEN ** -0.5),
        'nsa_w_cmp_v1': nrm(ks[9], (nN, cmp_in, CMP_HIDDEN), cmp_in ** -0.5),
        'nsa_w_cmp_v2': nrm(ks[10], (nN, CMP_HIDDEN, NSA_HEAD_DIM), CMP_HIDDEN ** -0.5),
        'nsa_cmp_pe': nrm(ks[11], (nN, CMP_BLOCK, NSA_HEAD_DIM), 0.1),
        'nsa_w_out': nrm(ks[12], (nN, NSA_HEADS * NSA_HEAD_DIM, D), DN_BETA * (NSA_HEADS * NSA_HEAD_DIM) ** -0.5),
        'moe_w_router': nrm(ks[13], (D, N_EXPERTS), D ** -0.5),
        'moe_b_router': nrm(ks[14], (N_EXPERTS,), 0.01),
        'moe_w_gate': nrm(ks[15], (DEPTH, N_EXPERTS, D, D_FF_EXPERT), D ** -0.5),
        'moe_w_up': nrm(ks[16], (DEPTH, N_EXPERTS, D, D_FF_EXPERT), D ** -0.5),
        'moe_w_down': nrm(ks[17], (DEPTH, N_EXPERTS, D_FF_EXPERT, D), DN_BETA * D_FF_EXPERT ** -0.5),
        'ln_g': 1.0 + nrm(ks[18], (DEPTH, 2, D), 0.02),
        'ln_b': nrm(ks[19], (DEPTH, 2, D), 0.02),
    }


def reference(x, gla_w_in, gla_w_gate_up, gla_b_gate, gla_norm_g, gla_w_out,
              nsa_w_in, nsa_w_cmp_k1, nsa_w_cmp_k2, nsa_w_cmp_v1, nsa_w_cmp_v2, nsa_cmp_pe, nsa_w_out,
              moe_w_router, moe_b_router, moe_w_gate, moe_w_up, moe_w_down, ln_g, ln_b):
    for i in range(DEPTH):
        j = i // 2
        if i % 2 == 0:
            h = gla_mixer(x, gla_w_in[j], gla_w_gate_up[j], gla_b_gate[j], gla_norm_g[j], gla_w_out[j])
        else:
            h = nsa_mixer(x, nsa_w_in[j], nsa_w_cmp_k1[j], nsa_w_cmp_k2[j], nsa_w_cmp_v1[j],
                          nsa_w_cmp_v2[j], nsa_cmp_pe[j], nsa_w_out[j])
        x = layer_norm(DN_ALPHA * x + h, ln_g[i, 0], ln_b[i, 0])
        f = moe_ffn(x, moe_w_router, moe_b_router, moe_w_gate[i], moe_w_up[i], moe_w_down[i])
        x = layer_norm(DN_ALPHA * x + f, ln_g[i, 1], ln_b[i, 1])
    return x
```

```python
import functools

import numpy as np
import jax
import jax.numpy as jnp
from jax import lax
from jax.experimental import pallas as pl
from jax.experimental.pallas import tpu as pltpu

F32 = jnp.float32
BF16 = jnp.bfloat16
HIGHEST = lax.Precision.HIGHEST

D_MODEL = 1024
DEPTH = 2

GLA_HEADS = 4
GLA_DK = D_MODEL // 2 // GLA_HEADS
GLA_DV = D_MODEL // GLA_HEADS
GLA_GATE_RANK = 16
GLA_TAU = 16.0
GLA_CHUNK = 64

NSA_HEADS = 16
NSA_GROUPS = 4
NSA_HPG = NSA_HEADS // NSA_GROUPS
NSA_HEAD_DIM = D_MODEL // NSA_HEADS
CMP_BLOCK = 32
CMP_STRIDE = 16
CMP_HIDDEN = 256
SLC_BLOCK = 64
SLC_SHIFT = SLC_BLOCK.bit_length() - 1
SLC_TOP_N = 16
WINDOW = 512

ROPE_THETA = 500000.0
ROT_DIM = NSA_HEAD_DIM // 4
ROT_HALF = ROT_DIM // 2

N_EXPERTS = 16
N_EXPERT_GROUPS = 4
EXPERTS_PER_GROUP = N_EXPERTS // N_EXPERT_GROUPS
D_FF_EXPERT = D_MODEL // 4

DN_ALPHA = (2 * DEPTH) ** 0.25
LN_EPS = 1e-5
NEG = -1e30
BIG = 1e30

LANES = 128
VMEM_LIMIT = 48 * 1024 * 1024

PROJ_TM = 256
GLA_T = 512
MOE_TM = 1024
CMP_TQ = 256
ATT_TQ = 128
ATT_TK = 512
WIN_TK = 128


def _cparams(sem):
    return pltpu.CompilerParams(dimension_semantics=sem, vmem_limit_bytes=VMEM_LIMIT)


def _bdot(a, b):
    return jnp.dot(a, b, preferred_element_type=F32)


def _dot_nt(a, b, precision=None):
    return lax.dot_general(a, b, (((1,), (1,)), ((), ())), preferred_element_type=F32, precision=precision)


def _dot_tn(a, b):
    return lax.dot_general(a, b, (((0,), (0,)), ((), ())), preferred_element_type=F32)


def _layer_norm(z, g, b):
    mu = jnp.mean(z, axis=-1, keepdims=True)
    zc = z - mu
    var = jnp.mean(zc * zc, axis=-1, keepdims=True)
    return zc * lax.rsqrt(var + LN_EPS) * g + b


def _silu(t):
    return t * jax.nn.sigmoid(t)


def _rope128(t, c, s1, s2):
    up = pltpu.roll(t, LANES - ROT_HALF, axis=1)
    dn = pltpu.roll(t, ROT_HALF, axis=1)
    return t * c + up * s1 + dn * s2


def _gla_proj_kernel(x_ref, wm_ref, wg_ref, wgu_ref, bg_ref, q_ref, k_ref, v_ref, r_ref, la_ref):
    xb = x_ref[...].astype(BF16)
    hk = GLA_HEADS * GLA_DK
    hv = GLA_HEADS * GLA_DV
    q_ref[...] = _bdot(xb, wm_ref[:, 0:hk])
    k_ref[...] = _bdot(xb, wm_ref[:, hk:2 * hk])
    v_ref[...] = _bdot(xb, wm_ref[:, 2 * hk:2 * hk + hv])
    r_ref[...] = _bdot(xb, wm_ref[:, 2 * hk + hv:2 * hk + 2 * hv])
    g_low = _bdot(xb, wg_ref[...])
    z = _bdot(g_low.astype(BF16), wgu_ref[...]) + bg_ref[...]
    log_sig = jnp.minimum(z, 0.0) - jnp.log1p(jnp.exp(-jnp.abs(z)))
    la_ref[...] = log_sig * (1.0 / GLA_TAU)


def _gla_proj(x2, w_main, w_glow, w_gu, b_gate):
    n = x2.shape[0]
    hk = GLA_HEADS * GLA_DK
    hv = GLA_HEADS * GLA_DV
    tm = PROJ_TM
    row = lambda i: (i, 0)
    full = lambda i: (0, 0)
    return pl.pallas_call(
        _gla_proj_kernel,
        grid=(n // tm,),
        in_specs=[pl.BlockSpec((tm, D_MODEL), row),
                  pl.BlockSpec(w_main.shape, full),
                  pl.BlockSpec(w_glow.shape, full),
                  pl.BlockSpec(w_gu.shape, full),
                  pl.BlockSpec(b_gate.shape, full)],
        out_specs=[pl.BlockSpec((tm, hk), row), pl.BlockSpec((tm, hk), row),
                   pl.BlockSpec((tm, hv), row), pl.BlockSpec((tm, hv), row),
                   pl.BlockSpec((tm, hk), row)],
        out_shape=[jax.ShapeDtypeStruct((n, hk), F32), jax.ShapeDtypeStruct((n, hk), F32),
                   jax.ShapeDtypeStruct((n, hv), F32), jax.ShapeDtypeStruct((n, hv), F32),
                   jax.ShapeDtypeStruct((n, hk), F32)],
        compiler_params=_cparams(("arbitrary",)),
        name="gla_proj",
    )(x2, w_main, w_glow, w_gu, b_gate)


def _gla_core_kernel(q_ref, k_ref, v_ref, la_ref, r_ref, g_ref, o_ref, st_ref):
    @pl.when(pl.program_id(2) == 0)
    def _():
        st_ref[...] = jnp.zeros_like(st_ref)

    c = GLA_CHUNK
    row = lax.broadcasted_iota(jnp.int32, (c, c), 0)
    col = lax.broadcasted_iota(jnp.int32, (c, c), 1)
    tril = row >= col
    tril_f = tril.astype(F32)
    g = g_ref[...]
    for ci in range(GLA_T // c):
        sl = pl.ds(ci * c, c)
        la = la_ref[0, sl, :]
        b = jnp.dot(tril_f, la, preferred_element_type=F32, precision=HIGHEST)
        b_last = b[c - 1:c, :]
        q = q_ref[0, sl, :] * (GLA_DK ** -0.5)
        k = k_ref[0, sl, :]
        vb = v_ref[0, sl, :].astype(BF16)
        q_dec = (q * jnp.exp(b)).astype(BF16)
        k_neg = (k * jnp.exp(-b)).astype(BF16)
        k_dec = (k * jnp.exp(b_last - b)).astype(BF16)
        s = jnp.where(tril, _dot_nt(q_dec, k_neg), 0.0)
        st = st_ref[...]
        o = _bdot(s.astype(BF16), vb) + _dot_nt(q_dec, st.astype(BF16))
        st_ref[...] = jnp.exp(b_last) * st + _dot_tn(vb, k_dec)
        ms = jnp.mean(o * o, axis=-1, keepdims=True)
        o = o * lax.rsqrt(ms + LN_EPS) * g
        o_ref[0, sl, :] = o * _silu(r_ref[0, sl, :])


def _gla_core(q, k, v, la, r, norm_g):
    bsz, s, _ = q.shape
    t = GLA_T
    kmap = lambda b, h, n: (b, n, h)
    return pl.pallas_call(
        _gla_core_kernel,
        grid=(bsz, GLA_HEADS, s // t),
        in_specs=[pl.BlockSpec((1, t, GLA_DK), kmap), pl.BlockSpec((1, t, GLA_DK), kmap),
                  pl.BlockSpec((1, t, GLA_DV), kmap), pl.BlockSpec((1, t, GLA_DK), kmap),
                  pl.BlockSpec((1, t, GLA_DV), kmap),
                  pl.BlockSpec((1, GLA_DV), lambda b, h, n: (0, 0))],
        out_specs=pl.BlockSpec((1, t, GLA_DV), kmap),
        out_shape=jax.ShapeDtypeStruct((bsz, s, GLA_HEADS * GLA_DV), F32),
        scratch_shapes=[pltpu.VMEM((GLA_DV, GLA_DK), F32)],
        compiler_params=_cparams(("arbitrary", "arbitrary", "arbitrary")),
        name="gla_core",
    )(q, k, v, la, r, norm_g)


def _outproj_ln_kernel(h_ref, x_ref, w_ref, g_ref, b_ref, o_ref):
    y = _bdot(h_ref[...].astype(BF16), w_ref[...])
    o_ref[...] = _layer_norm(DN_ALPHA * x_ref[...] + y, g_ref[...], b_ref[...])


def _outproj_ln(h2, x2, w, g, b):
    n = x2.shape[0]
    tm = PROJ_TM
    row = lambda i: (i, 0)
    full = lambda i: (0, 0)
    return pl.pallas_call(
        _outproj_ln_kernel,
        grid=(n // tm,),
        in_specs=[pl.BlockSpec((tm, h2.shape[1]), row), pl.BlockSpec((tm, D_MODEL), row),
                  pl.BlockSpec(w.shape, full), pl.BlockSpec((1, D_MODEL), full), pl.BlockSpec((1, D_MODEL), full)],
        out_specs=pl.BlockSpec((tm, D_MODEL), row),
        out_shape=jax.ShapeDtypeStruct((n, D_MODEL), F32),
        compiler_params=_cparams(("arbitrary",)),
        name="outproj_ln",
    )(h2, x2, w, g, b)


def _lane_xor(v, k, lane):
    up = pltpu.roll(v, LANES - k, axis=1)
    dn = pltpu.roll(v, k, axis=1)
    return jnp.where((lane & k) == 0, up, dn)


def _router_kernel(x_ref, w_ref, b_ref, gate_ref):
    logits = jnp.dot(x_ref[...], w_ref[...], preferred_element_type=F32, precision=HIGHEST)
    s = jax.nn.sigmoid(logits)
    a = s + b_ref[...]
    lane = lax.broadcasted_iota(jnp.int32, a.shape, 1)
    a1 = _lane_xor(a, 1, lane)
    a2 = _lane_xor(a, 2, lane)
    a3 = _lane_xor(a1, 2, lane)
    p, q = jnp.maximum(a, a1), jnp.minimum(a, a1)
    r, t = jnp.maximum(a2, a3), jnp.minimum(a2, a3)
    gs = jnp.maximum(p, r) + jnp.maximum(jnp.minimum(p, r), jnp.maximum(q, t))
    g1 = _lane_xor(gs, 4, lane)
    g2 = _lane_xor(gs, 8, lane)
    g3 = _lane_xor(g1, 8, lane)

    def beats(other, other_first):
        return (other > gs) | ((other == gs) & other_first)

    lose = (beats(g1, (lane & 4) != 0) | beats(g2, (lane & 8) != 0) | beats(g3, (lane & 8) != 0))

    def ahead(other, other_first):
        return jnp.where((other > a) | ((other == a) & other_first), 1.0, 0.0)

    rank = ahead(a1, (lane & 1) != 0) + ahead(a2, (lane & 2) != 0) + ahead(a3, (lane & 2) != 0)
    sel = jnp.logical_not(lose) & (rank < 2.0) & (lane < N_EXPERTS)
    ssel = jnp.where(sel, s, 0.0)
    s1 = _lane_xor(ssel, 1, lane)
    tot = (ssel + s1) + (_lane_xor(ssel, 2, lane) + _lane_xor(s1, 2, lane))
    gate_ref[...] = jnp.where(sel, s / tot, 0.0)


def _router(x2, w_pad, b_pad):
    n = x2.shape[0]
    tm = PROJ_TM
    return pl.pallas_call(
        _router_kernel,
        grid=(n // tm,),
        in_specs=[pl.BlockSpec((tm, D_MODEL), lambda i: (i, 0)),
                  pl.BlockSpec(w_pad.shape, lambda i: (0, 0)),
                  pl.BlockSpec(b_pad.shape, lambda i: (0, 0))],
        out_specs=pl.BlockSpec((tm, LANES), lambda i: (i, 0)),
        out_shape=jax.ShapeDtypeStruct((n, LANES), F32),
        compiler_params=_cparams(("arbitrary",)),
        name="moe_router",
    )(x2, w_pad, b_pad)


def _moe_kernel(x_ref, gate_ref, wgu_ref, wd_ref, lg_ref, lb_ref, o_ref, xb_ref, acc_ref):
    e = pl.program_id(1)

    @pl.when(e == 0)
    def _():
        xb_ref[...] = x_ref[...].astype(BF16)
        acc_ref[...] = jnp.zeros_like(acc_ref)

    hgu = _bdot(xb_ref[...], wgu_ref[0])
    lane = lax.broadcasted_iota(jnp.int32, gate_ref.shape, 1)
    gcol = jnp.sum(jnp.where(lane == e, gate_ref[...], 0.0), axis=1, keepdims=True)
    h = _silu(hgu[:, :D_FF_EXPERT]) * hgu[:, D_FF_EXPERT:] * gcol
    acc_ref[...] += _bdot(h.astype(BF16), wd_ref[0])

    @pl.when(e == N_EXPERTS - 1)
    def _():
        o_ref[...] = _layer_norm(DN_ALPHA * x_ref[...] + acc_ref[...], lg_ref[...], lb_ref[...])


def _moe_ffn_ln(x2, gate, wgu, wd, lg, lb):
    n = x2.shape[0]
    tm = min(MOE_TM, n)
    return pl.pallas_call(
        _moe_kernel,
        grid=(n // tm, N_EXPERTS),
        in_specs=[pl.BlockSpec((tm, D_MODEL), lambda i, e: (i, 0)),
                  pl.BlockSpec((tm, LANES), lambda i, e: (i, 0)),
                  pl.BlockSpec((1, D_MODEL, 2 * D_FF_EXPERT), lambda i, e: (e, 0, 0)),
                  pl.BlockSpec((1, D_FF_EXPERT, D_MODEL), lambda i, e: (e, 0, 0)),
                  pl.BlockSpec((1, D_MODEL), lambda i, e: (0, 0)),
                  pl.BlockSpec((1, D_MODEL), lambda i, e: (0, 0))],
        out_specs=pl.BlockSpec((tm, D_MODEL), lambda i, e: (i, 0)),
        out_shape=jax.ShapeDtypeStruct((n, D_MODEL), F32),
        scratch_shapes=[pltpu.VMEM((tm, D_MODEL), BF16), pltpu.VMEM((tm, D_MODEL), F32)],
        compiler_params=_cparams(("arbitrary", "arbitrary")),
        name="moe_ffn",
    )(x2, gate, wgu, wd, lg, lb)


def _nsa_proj_kernel(x_ref, wr_ref, wp_ref, wgt_ref, c_ref, s1_ref, s2_ref,
                     q_ref, ks_ref, kw_ref, kc_ref, vc_ref, vs_ref, vw_ref, gt_ref):
    xb = x_ref[...].astype(BF16)
    c, s1, s2 = c_ref[...], s1_ref[...], s2_ref[...]
    nq = NSA_HEADS * NSA_HEAD_DIM // LANES
    ng = NSA_GROUPS * NSA_HEAD_DIM // LANES
    for j in range(nq + 2 * ng):
        t = _rope128(_bdot(xb, wr_ref[:, j * LANES:(j + 1) * LANES]), c, s1, s2)
        if j < nq:
            q_ref[:, j * LANES:(j + 1) * LANES] = t * (NSA_HEAD_DIM ** -0.5)
        elif j < nq + ng:
            ks_ref[:, (j - nq) * LANES:(j - nq + 1) * LANES] = t
        else:
            kw_ref[:, (j - nq - ng) * LANES:(j - nq - ng + 1) * LANES] = t
    gd = NSA_GROUPS * NSA_HEAD_DIM
    for j, ref in enumerate((kc_ref, vc_ref, vs_ref, vw_ref)):
        ref[...] = _bdot(xb, wp_ref[:, j * gd:(j + 1) * gd])
    gt_ref[...] = jax.nn.sigmoid(_bdot(xb, wgt_ref[...]))


def _nsa_proj(x2, w_rope, w_plain, w_gates, rc, rs1, rs2, seq):
    n = x2.shape[0]
    tm = PROJ_TM
    gd = NSA_GROUPS * NSA_HEAD_DIM
    row = lambda i: (i, 0)
    full = lambda i: (0, 0)
    per = seq // tm
    pos = lambda i: (i % per, 0)
    out_w = [D_MODEL, gd, gd, gd, gd, gd, gd, NSA_GROUPS * LANES]
    return pl.pallas_call(
        _nsa_proj_kernel,
        grid=(n // tm,),
        in_specs=[pl.BlockSpec((tm, D_MODEL), row),
                  pl.BlockSpec(w_rope.shape, full), pl.BlockSpec(w_plain.shape, full),
                  pl.BlockSpec(w_gates.shape, full),
                  pl.BlockSpec((tm, LANES), pos), pl.BlockSpec((tm, LANES), pos), pl.BlockSpec((tm, LANES), pos)],
        out_specs=[pl.BlockSpec((tm, w), row) for w in out_w],
        out_shape=[jax.ShapeDtypeStruct((n, w), F32) for w in out_w],
        compiler_params=_cparams(("arbitrary",)),
        name="nsa_proj",
    )(x2, w_rope, w_plain, w_gates, rc, rs1, rs2)


def _compress_kernel(ck_ref, cv_ref, pet_ref, peb_ref, wk1_ref, wk2_ref, wv1_ref, wv2_ref,
                     c_ref, s1_ref, s2_ref, ko_ref, vo_ref):
    nch = ck_ref.shape[2]
    half = CMP_STRIDE * NSA_HEAD_DIM
    pet, peb = pet_ref[...], peb_ref[...]

    def mlp(ch, w1_ref, w2_ref):
        a = _bdot((ch + pet).astype(BF16), w1_ref[0:half, :])
        bm = _bdot((ch + peb).astype(BF16), w1_ref[half:2 * half, :])
        h = _silu(a + pltpu.roll(bm, nch - 1, axis=0))
        return _bdot(h.astype(BF16), w2_ref[...])

    kc = _rope128(mlp(ck_ref[0, 0], wk1_ref, wk2_ref), c_ref[...], s1_ref[...], s2_ref[...])
    ko_ref[0, 0] = kc[:, :NSA_HEAD_DIM]
    vo_ref[0, 0] = mlp(cv_ref[0, 0], wv1_ref, wv2_ref)[:, :NSA_HEAD_DIM]


def _compress(ck, cv, pet, peb, wk1, wk2, wv1, wv2, cc, cs1, cs2):
    bsz, g, nch, width = ck.shape
    blk = lambda b, gi: (b, gi, 0, 0)
    full = lambda b, gi: (0, 0)
    return pl.pallas_call(
        _compress_kernel,
        grid=(bsz, g),
        in_specs=[pl.BlockSpec((1, 1, nch, width), blk), pl.BlockSpec((1, 1, nch, width), blk),
                  pl.BlockSpec(pet.shape, full), pl.BlockSpec(peb.shape, full),
                  pl.BlockSpec(wk1.shape, full), pl.BlockSpec(wk2.shape, full),
                  pl.BlockSpec(wv1.shape, full), pl.BlockSpec(wv2.shape, full),
                  pl.BlockSpec(cc.shape, full), pl.BlockSpec(cs1.shape, full), pl.BlockSpec(cs2.shape, full)],
        out_specs=[pl.BlockSpec((1, 1, nch, NSA_HEAD_DIM), blk), pl.BlockSpec((1, 1, nch, NSA_HEAD_DIM), blk)],
        out_shape=[jax.ShapeDtypeStruct((bsz, g, nch, NSA_HEAD_DIM), F32)] * 2,
        compiler_params=_cparams(("arbitrary", "arbitrary")),
        name="nsa_compress",
    )(ck, cv, pet, peb, wk1, wk2, wv1, wv2, cc, cs1, cs2)


def _cmp_select_kernel(q_ref, kc_ref, vc_ref, wimp_ref, o_ref, sel_ref):
    tq = q_ref.shape[1]
    nch = kc_ref.shape[2]
    ns = wimp_ref.shape[0]
    s0 = pl.program_id(2) * tq
    qpos_c = s0 + lax.broadcasted_iota(jnp.int32, (tq, 1), 0)
    cend = lax.broadcasted_iota(jnp.int32, (1, nch), 1) * CMP_STRIDE + (CMP_BLOCK - 1)
    cmask = cend <= qpos_c
    anyvis = (qpos_c >= CMP_BLOCK - 1).astype(F32)
    kc = kc_ref[0, 0].astype(BF16)
    vc = vc_ref[0, 0].astype(BF16)
    pg = jnp.zeros((tq, nch), F32)
    d = NSA_HEAD_DIM
    for h in range(NSA_HPG):
        qh = q_ref[0, :, h * d:(h + 1) * d].astype(BF16)
        s = jnp.where(cmask, _dot_nt(qh, kc), NEG)
        e = jnp.exp(s - jnp.max(s, axis=-1, keepdims=True))
        p = (e / jnp.sum(e, axis=-1, keepdims=True)) * anyvis
        o_ref[0, :, h * d:(h + 1) * d] = _bdot(p.astype(BF16), vc)
        pg = pg + p

    imp = _dot_nt(wimp_ref[...], pg, precision=HIGHEST)
    blk = lax.broadcasted_iota(jnp.int32, (ns, tq), 0)
    cur = (s0 + lax.broadcasted_iota(jnp.int32, (ns, tq), 1)) >> SLC_SHIFT
    forced = (blk == 0) | (blk == cur) | (blk == cur - 1)
    imp = jnp.where(blk <= cur, jnp.where(forced, BIG, imp), NEG)
    blk_f = blk.astype(F32)
    sel = jnp.zeros((ns, tq), F32)
    for _ in range(min(SLC_TOP_N, ns)):
        m = jnp.max(imp, axis=0, keepdims=True)
        first = jnp.min(jnp.where(imp == m, blk_f, float(ns)), axis=0, keepdims=True)
        hit = blk_f == first
        sel = jnp.where(hit, 1.0, sel)
        imp = jnp.where(hit, -jnp.inf, imp)
    sel_ref[0, 0] = sel.T.astype(BF16)


def _cmp_select(q, k_cmp, v_cmp, wimp_t):
    bsz, s, _ = q.shape
    g = NSA_GROUPS
    nch = k_cmp.shape[2]
    ns = wimp_t.shape[0]
    tq = min(CMP_TQ, s)
    gd = NSA_HPG * NSA_HEAD_DIM
    return pl.pallas_call(
        _cmp_select_kernel,
        grid=(bsz, g, s // tq),
        in_specs=[pl.BlockSpec((1, tq, gd), lambda b, gi, qi: (b, qi, gi)),
                  pl.BlockSpec((1, 1, nch, NSA_HEAD_DIM), lambda b, gi, qi: (b, gi, 0, 0)),
                  pl.BlockSpec((1, 1, nch, NSA_HEAD_DIM), lambda b, gi, qi: (b, gi, 0, 0)),
                  pl.BlockSpec(wimp_t.shape, lambda b, gi, qi: (0, 0))],
        out_specs=[pl.BlockSpec((1, tq, gd), lambda b, gi, qi: (b, qi, gi)),
                   pl.BlockSpec((1, 1, tq, ns), lambda b, gi, qi: (b, gi, qi, 0))],
        out_shape=[jax.ShapeDtypeStruct((bsz, s, g * gd), F32),
                   jax.ShapeDtypeStruct((bsz, g, s, ns), BF16)],
        compiler_params=_cparams(("arbitrary", "arbitrary", "arbitrary")),
        name="nsa_cmp_select",
    )(q, k_cmp, v_cmp, wimp_t)


def _nsa_attn_kernel(q_ref, kst_ref, vs_ref, kwt_ref, vw_ref, sel_ref, eexp_ref, ocmp_ref, gt_ref,
                     o_ref, m_ref, l_ref, acc_ref):
    tq = q_ref.shape[1]
    d = NSA_HEAD_DIM
    hpg = NSA_HPG
    qi = pl.program_id(2)
    s0 = qi * tq
    q4 = jnp.concatenate([q_ref[0, :, h * d:(h + 1) * d] for h in range(hpg)], axis=0).astype(BF16)
    sel = sel_ref[0, 0]
    qpos = s0 + lax.broadcasted_iota(jnp.int32, (tq, 1), 0)

    def reset():
        m_ref[...] = jnp.full_like(m_ref, NEG)
        l_ref[...] = jnp.zeros_like(l_ref)
        acc_ref[...] = jnp.zeros_like(acc_ref)

    def update(s, bias, v):
        tk = s.shape[1]
        s = (s.reshape(hpg, tq, tk) + bias[None]).reshape(hpg * tq, tk)
        m_prev = m_ref[...]
        m_new = jnp.maximum(m_prev, jnp.max(s, axis=-1, keepdims=True))
        alpha = jnp.exp(m_prev - m_new)
        p = jnp.exp(s - m_new)
        l_ref[...] = alpha * l_ref[...] + jnp.sum(p, axis=-1, keepdims=True)
        acc_ref[...] = alpha * acc_ref[...] + _bdot(p.astype(BF16), v)
        m_ref[...] = m_new

    def sel_tile(t, causal):
        s = _bdot(q4, kst_ref[0, 0, t])
        bias = (_bdot(sel, eexp_ref[t]) - 1.0) * BIG
        if causal:
            kpos = t * ATT_TK + lax.broadcasted_iota(jnp.int32, (1, ATT_TK), 1)
            bias = jnp.where(kpos <= qpos, bias, NEG)
        update(s, bias, vs_ref[0, 0, t])

    reset()
    t_diag = s0 // ATT_TK

    def sel_body(t, carry):
        sel_tile(t, False)
        return carry

    lax.fori_loop(0, t_diag, sel_body, 0)
    sel_tile(t_diag, True)
    o_slc = acc_ref[...] / l_ref[...]

    reset()

    def win_body(t, carry):
        s = _bdot(q4, kwt_ref[0, 0, t])
        rel = qpos - (t * WIN_TK + lax.broadcasted_iota(jnp.int32, (1, WIN_TK), 1))
        bias = jnp.where((rel >= 0) & (rel < WINDOW), 0.0, NEG)
        update(s, bias, vw_ref[0, 0, t])
        return carry

    w_hi = (s0 + tq) // WIN_TK
    w_lo = jnp.maximum(s0 // WIN_TK - WINDOW // WIN_TK, 0)
    lax.fori_loop(w_lo, w_hi, win_body, 0)
    o_win = acc_ref[...] / l_ref[...]

    gts = gt_ref[0]
    for h in range(hpg):
        g0, g1, g2 = (gts[:, 3 * h + j:3 * h + j + 1] for j in range(3))
        rows = slice(h * tq, (h + 1) * tq)
        o_ref[0, :, h * d:(h + 1) * d] = (g0 * ocmp_ref[0, :, h * d:(h + 1) * d]
                                          + g1 * o_slc[rows] + g2 * o_win[rows])


def _nsa_attn(q, kst, vst, kwt, vwt, selmask, eexp, o_cmp, gates):
    bsz, s, _ = q.shape
    g = NSA_GROUPS
    tq = ATT_TQ
    gd = NSA_HPG * NSA_HEAD_DIM
    ns = selmask.shape[-1]
    qmap = lambda b, gi, qi: (b, qi, gi)
    kvmap = lambda b, gi, qi: (b, gi, 0, 0, 0)
    return pl.pallas_call(
        _nsa_attn_kernel,
        grid=(bsz, g, s // tq),
        in_specs=[pl.BlockSpec((1, tq, gd), qmap),
                  pl.BlockSpec((1, 1) + kst.shape[2:], kvmap), pl.BlockSpec((1, 1) + vst.shape[2:], kvmap),
                  pl.BlockSpec((1, 1) + kwt.shape[2:], kvmap), pl.BlockSpec((1, 1) + vwt.shape[2:], kvmap),
                  pl.BlockSpec((1, 1, tq, ns), lambda b, gi, qi: (b, gi, qi, 0)),
                  pl.BlockSpec(eexp.shape, lambda b, gi, qi: (0, 0, 0)),
                  pl.BlockSpec((1, tq, gd), qmap),
                  pl.BlockSpec((1, tq, LANES), qmap)],
        out_specs=pl.BlockSpec((1, tq, gd), qmap),
        out_shape=jax.ShapeDtypeStruct((bsz, s, g * gd), F32),
        scratch_shapes=[pltpu.VMEM((NSA_HPG * tq, 1), F32), pltpu.VMEM((NSA_HPG * tq, 1), F32),
                        pltpu.VMEM((NSA_HPG * tq, NSA_HEAD_DIM), F32)],
        compiler_params=_cparams(("arbitrary", "arbitrary", "arbitrary")),
        name="nsa_attn",
    )(q, kst, vst, kwt, vwt, selmask, eexp, o_cmp, gates)


def _rope_tables(pos):
    inv_freq = jnp.power(ROPE_THETA, -jnp.arange(ROT_HALF, dtype=F32) * (2.0 / ROT_DIM))
    ang = pos.astype(F32)[:, None] * inv_freq[None, :]
    cos, sin = jnp.cos(ang), jnp.sin(ang)
    n = pos.shape[0]
    one = jnp.ones((n, NSA_HEAD_DIM - ROT_DIM), F32)
    zero8 = jnp.zeros((n, ROT_HALF), F32)
    zero = jnp.zeros((n, NSA_HEAD_DIM - ROT_DIM), F32)
    c = jnp.concatenate([cos, cos, one], axis=1)
    s1 = jnp.concatenate([-sin, zero8, zero], axis=1)
    s2 = jnp.concatenate([zero8, sin, zero], axis=1)
    rep = LANES // NSA_HEAD_DIM
    return tuple(jnp.tile(t, (1, rep)) for t in (c, s1, s2))


def _importance_weights(ns, nch):
    ratio = SLC_BLOCK // CMP_STRIDE
    span = CMP_BLOCK // CMP_STRIDE
    w = np.zeros((ns, nch), np.float32)
    for j in range(ns):
        for m in range(ratio):
            for n in range(span):
                c = ratio * j + m + n
                if c < nch - 1:
                    w[j, c] += 1.0
    return jnp.asarray(w)


def _block_expansion(nt, ns):
    key_blk = (np.arange(nt)[:, None] * ATT_TK + np.arange(ATT_TK)[None, :]) // SLC_BLOCK
    e = (np.arange(ns)[None, :, None] == key_blk[:, None, :]).astype(np.float32)
    return jnp.asarray(e, dtype=BF16)


def _gla_layer(x2, bsz, seq, w_in, w_gate_up, b_gate, norm_g, w_out, ln_g, ln_b):
    hk = GLA_HEADS * GLA_DK
    hv = GLA_HEADS * GLA_DV
    cuts = np.cumsum([hk, hk, hv, GLA_GATE_RANK]).tolist()
    wq, wk, wv, wg, wr = jnp.split(w_in, cuts, axis=1)
    w_main = jnp.concatenate([wq, wk, wv, wr], axis=1).astype(BF16)
    w_glow = jnp.pad(wg, ((0, 0), (0, LANES - GLA_GATE_RANK))).astype(BF16)
    w_gu = jnp.pad(w_gate_up, ((0, LANES - GLA_GATE_RANK), (0, 0))).astype(BF16)
    q, k, v, r, la = _gla_proj(x2, w_main, w_glow, w_gu, b_gate.reshape(1, hk))
    sh = lambda t: t.reshape(bsz, seq, t.shape[-1])
    o = _gla_core(sh(q), sh(k), sh(v), sh(la), sh(r), norm_g.reshape(1, GLA_DV))
    return _outproj_ln(o.reshape(bsz * seq, hv), x2, w_out.astype(BF16),
                       ln_g.reshape(1, D_MODEL), ln_b.reshape(1, D_MODEL))


def _nsa_layer(x2, bsz, seq, w_in, w_ck1, w_ck2, w_cv1, w_cv2, cmp_pe, w_out, ln_g, ln_b):
    h, g, hpg, d = NSA_HEADS, NSA_GROUPS, NSA_HPG, NSA_HEAD_DIM
    gd = g * d
    cuts = np.cumsum([h * d] + [gd] * 6).tolist()
    wq, wkc, wvc, wks, wvs, wkw, wvw, wgt = jnp.split(w_in, cuts, axis=1)
    w_rope = jnp.concatenate([wq, wks, wkw], axis=1).astype(BF16)
    w_plain = jnp.concatenate([wkc, wvc, wvs, wvw], axis=1).astype(BF16)
    w_gates = jnp.pad(wgt.reshape(D_MODEL, g, hpg * 3), ((0, 0), (0, 0), (0, LANES - hpg * 3)))
    w_gates = w_gates.reshape(D_MODEL, g * LANES).astype(BF16)
    rc, rs1, rs2 = _rope_tables(jnp.arange(seq, dtype=jnp.int32))
    q, ks, kw, kc, vc, vs, vw, gates = _nsa_proj(x2, w_rope, w_plain, w_gates, rc, rs1, rs2, seq)

    nch = seq // CMP_STRIDE
    chunks = lambda t: (t.reshape(bsz, nch, CMP_STRIDE, g, d).transpose(0, 3, 1, 2, 4)
                        .reshape(bsz, g, nch, CMP_STRIDE * d))
    pe = cmp_pe.reshape(1, CMP_BLOCK * d)
    half = CMP_STRIDE * d
    pad2 = lambda w: jnp.pad(w, ((0, 0), (0, LANES - d))).astype(BF16)
    cc, cs1, cs2 = _rope_tables(jnp.arange(nch, dtype=jnp.int32) * CMP_STRIDE + (CMP_BLOCK - 1))
    k_cmp, v_cmp = _compress(chunks(kc), chunks(vc), pe[:, :half], pe[:, half:],
                             w_ck1.astype(BF16), pad2(w_ck2), w_cv1.astype(BF16), pad2(w_cv2), cc, cs1, cs2)

    ns = seq // SLC_BLOCK
    sh = lambda t: t.reshape(bsz, seq, t.shape[-1])
    o_cmp, selmask = _cmp_select(sh(q), k_cmp, v_cmp, _importance_weights(ns, nch))

    def k_tiles(t, tk):
        return t.astype(BF16).reshape(bsz, seq // tk, tk, g, d).transpose(0, 3, 1, 4, 2)

    def v_tiles(t, tk):
        return t.astype(BF16).reshape(bsz, seq // tk, tk, g, d).transpose(0, 3, 1, 2, 4)

    o = _nsa_attn(sh(q), k_tiles(ks, ATT_TK), v_tiles(vs, ATT_TK), k_tiles(kw, WIN_TK), v_tiles(vw, WIN_TK),
                  selmask, _block_expansion(seq // ATT_TK, ns), o_cmp, sh(gates))
    return _outproj_ln(o.reshape(bsz * seq, h * d), x2, w_out.astype(BF16),
                       ln_g.reshape(1, D_MODEL), ln_b.reshape(1, D_MODEL))


def _moe_layer(x2, w_router_pad, b_router_pad, w_gate, w_up, w_down, ln_g, ln_b):
    gate = _router(x2, w_router_pad, b_router_pad)
    wgu = jnp.concatenate([w_gate, w_up], axis=-1).astype(BF16)
    return _moe_ffn_ln(x2, gate, wgu, w_down.astype(BF16), ln_g.reshape(1, D_MODEL), ln_b.reshape(1, D_MODEL))


def kernel(x, gla_w_in, gla_w_gate_up, gla_b_gate, gla_norm_g, gla_w_out, nsa_w_in, nsa_w_cmp_k1, nsa_w_cmp_k2,
           nsa_w_cmp_v1, nsa_w_cmp_v2, nsa_cmp_pe, nsa_w_out, moe_w_router, moe_b_router, moe_w_gate, moe_w_up,
           moe_w_down, ln_g, ln_b):
    bsz, seq, _ = x.shape
    x2 = x.reshape(bsz * seq, D_MODEL)
    w_router_pad = jnp.pad(moe_w_router, ((0, 0), (0, LANES - N_EXPERTS)))
    b_router_pad = jnp.pad(moe_b_router, (0, LANES - N_EXPERTS)).reshape(1, LANES)
    for i in range(DEPTH):
        j = i // 2
        if i % 2 == 0:
            x2 = _gla_layer(x2, bsz, seq, gla_w_in[j], gla_w_gate_up[j], gla_b_gate[j], gla_norm_g[j],
                            gla_w_out[j], ln_g[i, 0], ln_b[i, 0])
        else:
            x2 = _nsa_layer(x2, bsz, seq, nsa_w_in[j], nsa_w_cmp_k1[j], nsa_w_cmp_k2[j], nsa_w_cmp_v1[j],
                            nsa_w_cmp_v2[j], nsa_cmp_pe[j], nsa_w_out[j], ln_g[i, 0], ln_b[i, 0])
        x2 = _moe_layer(x2, w_router_pad, b_router_pad, moe_w_gate[i], moe_w_up[i], moe_w_down[i],
                        ln_g[i, 1], ln_b[i, 1])
    return x2.reshape(bsz, seq, D_MODEL)
```

```python
import functools

import numpy as np
import jax
import jax.numpy as jnp
from jax import lax
from jax.experimental import pallas as pl
from jax.experimental.pallas import tpu as pltpu

F32 = jnp.float32
BF16 = jnp.bfloat16
HIGHEST = lax.Precision.HIGHEST

D_MODEL = 1024
DEPTH = 2

GLA_HEADS = 4
GLA_DK = D_MODEL // 2 // GLA_HEADS
GLA_DV = D_MODEL // GLA_HEADS
GLA_GATE_RANK = 16
GLA_TAU = 16.0
GLA_CHUNK = 64

NSA_HEADS = 16
NSA_GROUPS = 4
NSA_HPG = NSA_HEADS // NSA_GROUPS
NSA_HEAD_DIM = D_MODEL // NSA_HEADS
CMP_BLOCK = 32
CMP_STRIDE = 16
CMP_HIDDEN = 256
SLC_BLOCK = 64
SLC_SHIFT = SLC_BLOCK.bit_length() - 1
SLC_TOP_N = 16
WINDOW = 512

ROPE_THETA = 500000.0
ROT_DIM = NSA_HEAD_DIM // 4
ROT_HALF = ROT_DIM // 2

N_EXPERTS = 16
N_EXPERT_GROUPS = 4
EXPERTS_PER_GROUP = N_EXPERTS // N_EXPERT_GROUPS
D_FF_EXPERT = D_MODEL // 4

DN_ALPHA = (2 * DEPTH) ** 0.25
LN_EPS = 1e-5
NEG = -1e30
BIG = 1e30
LOG2E = 1.4426950408889634

LANES = 128
VMEM_LIMIT = 48 * 1024 * 1024

PROJ_TM = 256
GLA_T = 512
MOE_TM = 1024
CMP_TQ = 256
ATT_TQ = 128
ATT_TK = 512
WIN_TK = 128


def _cparams(sem):
    return pltpu.CompilerParams(dimension_semantics=sem, vmem_limit_bytes=VMEM_LIMIT)


def _bdot(a, b):
    return jnp.dot(a, b, preferred_element_type=F32)


def _dot_nt(a, b, precision=None):
    return lax.dot_general(a, b, (((1,), (1,)), ((), ())), preferred_element_type=F32, precision=precision)


def _dot_tn(a, b):
    return lax.dot_general(a, b, (((0,), (0,)), ((), ())), preferred_element_type=F32)


def _layer_norm(z, g, b):
    mu = jnp.mean(z, axis=-1, keepdims=True)
    zc = z - mu
    var = jnp.mean(zc * zc, axis=-1, keepdims=True)
    return zc * lax.rsqrt(var + LN_EPS) * g + b


def _silu(t):
    return t * jax.nn.sigmoid(t)


def _rope128(t, c, s1, s2):
    up = pltpu.roll(t, LANES - ROT_HALF, axis=1)
    dn = pltpu.roll(t, ROT_HALF, axis=1)
    return t * c + up * s1 + dn * s2


def _gla_proj_kernel(x_ref, wm_ref, wg_ref, wgu_ref, bg_ref, q_ref, k_ref, v_ref, r_ref, la_ref):
    xb = x_ref[...].astype(BF16)
    hk = GLA_HEADS * GLA_DK
    hv = GLA_HEADS * GLA_DV
    q_ref[...] = _bdot(xb, wm_ref[:, 0:hk])
    k_ref[...] = _bdot(xb, wm_ref[:, hk:2 * hk])
    v_ref[...] = _bdot(xb, wm_ref[:, 2 * hk:2 * hk + hv])
    r_ref[...] = _bdot(xb, wm_ref[:, 2 * hk + hv:2 * hk + 2 * hv])
    g_low = _bdot(xb, wg_ref[...])
    z = _bdot(g_low.astype(BF16), wgu_ref[...]) + bg_ref[...]
    log_sig = jnp.minimum(z, 0.0) - jnp.log1p(jnp.exp(-jnp.abs(z)))
    la_ref[...] = log_sig * (1.0 / GLA_TAU)


def _gla_proj(x2, w_main, w_glow, w_gu, b_gate):
    n = x2.shape[0]
    hk = GLA_HEADS * GLA_DK
    hv = GLA_HEADS * GLA_DV
    tm = PROJ_TM
    row = lambda i: (i, 0)
    full = lambda i: (0, 0)
    return pl.pallas_call(
        _gla_proj_kernel,
        grid=(n // tm,),
        in_specs=[pl.BlockSpec((tm, D_MODEL), row),
                  pl.BlockSpec(w_main.shape, full),
                  pl.BlockSpec(w_glow.shape, full),
                  pl.BlockSpec(w_gu.shape, full),
                  pl.BlockSpec(b_gate.shape, full)],
        out_specs=[pl.BlockSpec((tm, hk), row), pl.BlockSpec((tm, hk), row),
                   pl.BlockSpec((tm, hv), row), pl.BlockSpec((tm, hv), row),
                   pl.BlockSpec((tm, hk), row)],
        out_shape=[jax.ShapeDtypeStruct((n, hk), F32), jax.ShapeDtypeStruct((n, hk), F32),
                   jax.ShapeDtypeStruct((n, hv), F32), jax.ShapeDtypeStruct((n, hv), F32),
                   jax.ShapeDtypeStruct((n, hk), F32)],
        compiler_params=_cparams(("arbitrary",)),
        name="gla_proj",
    )(x2, w_main, w_glow, w_gu, b_gate)


def _gla_core_kernel(q_ref, k_ref, v_ref, la_ref, r_ref, g_ref, o_ref, st_ref):
    @pl.when(pl.program_id(2) == 0)
    def _():
        st_ref[...] = jnp.zeros_like(st_ref)

    c = GLA_CHUNK
    row = lax.broadcasted_iota(jnp.int32, (c, c), 0)
    col = lax.broadcasted_iota(jnp.int32, (c, c), 1)
    tril = row >= col
    tril_f = tril.astype(F32)
    g = g_ref[...]
    for ci in range(GLA_T // c):
        sl = pl.ds(ci * c, c)
        la = la_ref[0, sl, :]
        b = jnp.dot(tril_f, la, preferred_element_type=F32, precision=HIGHEST)
        b_last = b[c - 1:c, :]
        q = q_ref[0, sl, :] * (GLA_DK ** -0.5)
        k = k_ref[0, sl, :]
        vb = v_ref[0, sl, :].astype(BF16)
        q_dec = (q * jnp.exp(b)).astype(BF16)
        k_neg = (k * jnp.exp(-b)).astype(BF16)
        k_dec = (k * jnp.exp(b_last - b)).astype(BF16)
        s = jnp.where(tril, _dot_nt(q_dec, k_neg), 0.0)
        st = st_ref[...]
        o = _bdot(s.astype(BF16), vb) + _dot_nt(q_dec, st.astype(BF16))
        st_ref[...] = jnp.exp(b_last) * st + _dot_tn(vb, k_dec)
        ms = jnp.mean(o * o, axis=-1, keepdims=True)
        o = o * lax.rsqrt(ms + LN_EPS) * g
        o_ref[0, sl, :] = o * _silu(r_ref[0, sl, :])


def _gla_core(q, k, v, la, r, norm_g):
    bsz, s, _ = q.shape
    t = GLA_T
    kmap = lambda b, h, n: (b, n, h)
    return pl.pallas_call(
        _gla_core_kernel,
        grid=(bsz, GLA_HEADS, s // t),
        in_specs=[pl.BlockSpec((1, t, GLA_DK), kmap), pl.BlockSpec((1, t, GLA_DK), kmap),
                  pl.BlockSpec((1, t, GLA_DV), kmap), pl.BlockSpec((1, t, GLA_DK), kmap),
                  pl.BlockSpec((1, t, GLA_DV), kmap),
                  pl.BlockSpec((1, GLA_DV), lambda b, h, n: (0, 0))],
        out_specs=pl.BlockSpec((1, t, GLA_DV), kmap),
        out_shape=jax.ShapeDtypeStruct((bsz, s, GLA_HEADS * GLA_DV), F32),
        scratch_shapes=[pltpu.VMEM((GLA_DV, GLA_DK), F32)],
        compiler_params=_cparams(("arbitrary", "arbitrary", "arbitrary")),
        name="gla_core",
    )(q, k, v, la, r, norm_g)


def _outproj_ln_kernel(h_ref, x_ref, w_ref, g_ref, b_ref, o_ref):
    y = _bdot(h_ref[...].astype(BF16), w_ref[...])
    o_ref[...] = _layer_norm(DN_ALPHA * x_ref[...] + y, g_ref[...], b_ref[...])


def _outproj_ln(h2, x2, w, g, b):
    n = x2.shape[0]
    tm = PROJ_TM
    row = lambda i: (i, 0)
    full = lambda i: (0, 0)
    return pl.pallas_call(
        _outproj_ln_kernel,
        grid=(n // tm,),
        in_specs=[pl.BlockSpec((tm, h2.shape[1]), row), pl.BlockSpec((tm, D_MODEL), row),
                  pl.BlockSpec(w.shape, full), pl.BlockSpec((1, D_MODEL), full), pl.BlockSpec((1, D_MODEL), full)],
        out_specs=pl.BlockSpec((tm, D_MODEL), row),
        out_shape=jax.ShapeDtypeStruct((n, D_MODEL), F32),
        compiler_params=_cparams(("arbitrary",)),
        name="outproj_ln",
    )(h2, x2, w, g, b)


def _lane_xor(v, k, lane):
    up = pltpu.roll(v, LANES - k, axis=1)
    dn = pltpu.roll(v, k, axis=1)
    return jnp.where((lane & k) == 0, up, dn)


def _router_kernel(x_ref, w_ref, b_ref, gate_ref):
    logits = jnp.dot(x_ref[...], w_ref[...], preferred_element_type=F32, precision=HIGHEST)
    s = jax.nn.sigmoid(logits)
    a = s + b_ref[...]
    lane = lax.broadcasted_iota(jnp.int32, a.shape, 1)
    a1 = _lane_xor(a, 1, lane)
    a2 = _lane_xor(a, 2, lane)
    a3 = _lane_xor(a1, 2, lane)
    p, q = jnp.maximum(a, a1), jnp.minimum(a, a1)
    r, t = jnp.maximum(a2, a3), jnp.minimum(a2, a3)
    gs = jnp.maximum(p, r) + jnp.maximum(jnp.minimum(p, r), jnp.maximum(q, t))
    g1 = _lane_xor(gs, 4, lane)
    g2 = _lane_xor(gs, 8, lane)
    g3 = _lane_xor(g1, 8, lane)

    def beats(other, other_first):
        return (other > gs) | ((other == gs) & other_first)

    lose = (beats(g1, (lane & 4) != 0) | beats(g2, (lane & 8) != 0) | beats(g3, (lane & 8) != 0))

    def ahead(other, other_first):
        return jnp.where((other > a) | ((other == a) & other_first), 1.0, 0.0)

    rank = ahead(a1, (lane & 1) != 0) + ahead(a2, (lane & 2) != 0) + ahead(a3, (lane & 2) != 0)
    sel = jnp.logical_not(lose) & (rank < 2.0) & (lane < N_EXPERTS)
    ssel = jnp.where(sel, s, 0.0)
    s1 = _lane_xor(ssel, 1, lane)
    tot = (ssel + s1) + (_lane_xor(ssel, 2, lane) + _lane_xor(s1, 2, lane))
    gate_ref[...] = jnp.where(sel, s / tot, 0.0)


def _router(x2, w_pad, b_pad):
    n = x2.shape[0]
    tm = PROJ_TM
    return pl.pallas_call(
        _router_kernel,
        grid=(n // tm,),
        in_specs=[pl.BlockSpec((tm, D_MODEL), lambda i: (i, 0)),
                  pl.BlockSpec(w_pad.shape, lambda i: (0, 0)),
                  pl.BlockSpec(b_pad.shape, lambda i: (0, 0))],
        out_specs=pl.BlockSpec((tm, LANES), lambda i: (i, 0)),
        out_shape=jax.ShapeDtypeStruct((n, LANES), F32),
        compiler_params=_cparams(("arbitrary",)),
        name="moe_router",
    )(x2, w_pad, b_pad)


def _moe_kernel(x_ref, gate_ref, wgu_ref, wd_ref, lg_ref, lb_ref, o_ref, xb_ref, acc_ref):
    e = pl.program_id(1)

    @pl.when(e == 0)
    def _():
        xb_ref[...] = x_ref[...].astype(BF16)
        acc_ref[...] = jnp.zeros_like(acc_ref)

    hgu = _bdot(xb_ref[...], wgu_ref[0])
    lane = lax.broadcasted_iota(jnp.int32, gate_ref.shape, 1)
    gcol = jnp.sum(jnp.where(lane == e, gate_ref[...], 0.0), axis=1, keepdims=True)
    h = _silu(hgu[:, :D_FF_EXPERT]) * hgu[:, D_FF_EXPERT:] * gcol
    acc_ref[...] += _bdot(h.astype(BF16), wd_ref[0])

    @pl.when(e == N_EXPERTS - 1)
    def _():
        o_ref[...] = _layer_norm(DN_ALPHA * x_ref[...] + acc_ref[...], lg_ref[...], lb_ref[...])


def _moe_ffn_ln(x2, gate, wgu, wd, lg, lb):
    n = x2.shape[0]
    tm = min(MOE_TM, n)
    return pl.pallas_call(
        _moe_kernel,
        grid=(n // tm, N_EXPERTS),
        in_specs=[pl.BlockSpec((tm, D_MODEL), lambda i, e: (i, 0)),
                  pl.BlockSpec((tm, LANES), lambda i, e: (i, 0)),
                  pl.BlockSpec((1, D_MODEL, 2 * D_FF_EXPERT), lambda i, e: (e, 0, 0)),
                  pl.BlockSpec((1, D_FF_EXPERT, D_MODEL), lambda i, e: (e, 0, 0)),
                  pl.BlockSpec((1, D_MODEL), lambda i, e: (0, 0)),
                  pl.BlockSpec((1, D_MODEL), lambda i, e: (0, 0))],
        out_specs=pl.BlockSpec((tm, D_MODEL), lambda i, e: (i, 0)),
        out_shape=jax.ShapeDtypeStruct((n, D_MODEL), F32),
        scratch_shapes=[pltpu.VMEM((tm, D_MODEL), BF16), pltpu.VMEM((tm, D_MODEL), F32)],
        compiler_params=_cparams(("arbitrary", "arbitrary")),
        name="moe_ffn",
    )(x2, gate, wgu, wd, lg, lb)


def _nsa_proj_kernel(x_ref, wr_ref, wp_ref, wgt_ref, c_ref, s1_ref, s2_ref,
                     q_ref, ks_ref, kw_ref, kc_ref, vc_ref, vs_ref, vw_ref, gt_ref):
    xb = x_ref[...].astype(BF16)
    c, s1, s2 = c_ref[...], s1_ref[...], s2_ref[...]
    nq = NSA_HEADS * NSA_HEAD_DIM // LANES
    ng = NSA_GROUPS * NSA_HEAD_DIM // LANES
    for j in range(nq + 2 * ng):
        t = _rope128(_bdot(xb, wr_ref[:, j * LANES:(j + 1) * LANES]), c, s1, s2)
        if j < nq:
            q_ref[:, j * LANES:(j + 1) * LANES] = t * (NSA_HEAD_DIM ** -0.5)
        elif j < nq + ng:
            ks_ref[:, (j - nq) * LANES:(j - nq + 1) * LANES] = t
        else:
            kw_ref[:, (j - nq - ng) * LANES:(j - nq - ng + 1) * LANES] = t
    gd = NSA_GROUPS * NSA_HEAD_DIM
    for j, ref in enumerate((kc_ref, vc_ref, vs_ref, vw_ref)):
        ref[...] = _bdot(xb, wp_ref[:, j * gd:(j + 1) * gd])
    gt_ref[...] = jax.nn.sigmoid(_bdot(xb, wgt_ref[...]))


def _nsa_proj(x2, w_rope, w_plain, w_gates, rc, rs1, rs2, seq):
    n = x2.shape[0]
    tm = PROJ_TM
    gd = NSA_GROUPS * NSA_HEAD_DIM
    row = lambda i: (i, 0)
    full = lambda i: (0, 0)
    per = seq // tm
    pos = lambda i: (i % per, 0)
    out_w = [D_MODEL, gd, gd, gd, gd, gd, gd, NSA_GROUPS * LANES]
    return pl.pallas_call(
        _nsa_proj_kernel,
        grid=(n // tm,),
        in_specs=[pl.BlockSpec((tm, D_MODEL), row),
                  pl.BlockSpec(w_rope.shape, full), pl.BlockSpec(w_plain.shape, full),
                  pl.BlockSpec(w_gates.shape, full),
                  pl.BlockSpec((tm, LANES), pos), pl.BlockSpec((tm, LANES), pos), pl.BlockSpec((tm, LANES), pos)],
        out_specs=[pl.BlockSpec((tm, w), row) for w in out_w],
        out_shape=[jax.ShapeDtypeStruct((n, w), F32) for w in out_w],
        compiler_params=_cparams(("arbitrary",)),
        name="nsa_proj",
    )(x2, w_rope, w_plain, w_gates, rc, rs1, rs2)


def _compress_kernel(ck_ref, cv_ref, pet_ref, peb_ref, wk1_ref, wk2_ref, wv1_ref, wv2_ref,
                     c_ref, s1_ref, s2_ref, ko_ref, vo_ref):
    nch = ck_ref.shape[2]
    half = CMP_STRIDE * NSA_HEAD_DIM
    pet, peb = pet_ref[...], peb_ref[...]

    def mlp(ch, w1_ref, w2_ref):
        a = _bdot((ch + pet).astype(BF16), w1_ref[0:half, :])
        bm = _bdot((ch + peb).astype(BF16), w1_ref[half:2 * half, :])
        h = _silu(a + pltpu.roll(bm, nch - 1, axis=0))
        return _bdot(h.astype(BF16), w2_ref[...])

    kc = _rope128(mlp(ck_ref[0, 0], wk1_ref, wk2_ref), c_ref[...], s1_ref[...], s2_ref[...])
    ko_ref[0, 0] = kc[:, :NSA_HEAD_DIM]
    vo_ref[0, 0] = mlp(cv_ref[0, 0], wv1_ref, wv2_ref)[:, :NSA_HEAD_DIM]


def _compress(ck, cv, pet, peb, wk1, wk2, wv1, wv2, cc, cs1, cs2):
    bsz, g, nch, width = ck.shape
    blk = lambda b, gi: (b, gi, 0, 0)
    full = lambda b, gi: (0, 0)
    return pl.pallas_call(
        _compress_kernel,
        grid=(bsz, g),
        in_specs=[pl.BlockSpec((1, 1, nch, width), blk), pl.BlockSpec((1, 1, nch, width), blk),
                  pl.BlockSpec(pet.shape, full), pl.BlockSpec(peb.shape, full),
                  pl.BlockSpec(wk1.shape, full), pl.BlockSpec(wk2.shape, full),
                  pl.BlockSpec(wv1.shape, full), pl.BlockSpec(wv2.shape, full),
                  pl.BlockSpec(cc.shape, full), pl.BlockSpec(cs1.shape, full), pl.BlockSpec(cs2.shape, full)],
        out_specs=[pl.BlockSpec((1, 1, nch, NSA_HEAD_DIM), blk), pl.BlockSpec((1, 1, nch, NSA_HEAD_DIM), blk)],
        out_shape=[jax.ShapeDtypeStruct((bsz, g, nch, NSA_HEAD_DIM), F32)] * 2,
        compiler_params=_cparams(("arbitrary", "arbitrary")),
        name="nsa_compress",
    )(ck, cv, pet, peb, wk1, wk2, wv1, wv2, cc, cs1, cs2)


def _cmp_select_kernel(q_ref, kc_ref, vc_ref, wimp_ref, o_ref, sel_ref):
    tq = q_ref.shape[1]
    nch = kc_ref.shape[2]
    ns = wimp_ref.shape[0]
    s0 = pl.program_id(2) * tq
    qpos_c = s0 + lax.broadcasted_iota(jnp.int32, (tq, 1), 0)
    cend = lax.broadcasted_iota(jnp.int32, (1, nch), 1) * CMP_STRIDE + (CMP_BLOCK - 1)
    cmask = cend <= qpos_c
    anyvis = (qpos_c >= CMP_BLOCK - 1).astype(F32)
    kc = kc_ref[0, 0].astype(BF16)
    vc = vc_ref[0, 0].astype(BF16)
    pg = jnp.zeros((tq, nch), F32)
    d = NSA_HEAD_DIM
    for h in range(NSA_HPG):
        qh = q_ref[0, :, h * d:(h + 1) * d].astype(BF16)
        s = jnp.where(cmask, _dot_nt(qh, kc), NEG)
        e = jnp.exp(s - jnp.max(s, axis=-1, keepdims=True))
        p = (e / jnp.sum(e, axis=-1, keepdims=True)) * anyvis
        o_ref[0, :, h * d:(h + 1) * d] = _bdot(p.astype(BF16), vc)
        pg = pg + p

    imp = _dot_nt(wimp_ref[...], pg, precision=HIGHEST)
    blk = lax.broadcasted_iota(jnp.int32, (ns, tq), 0)
    cur = (s0 + lax.broadcasted_iota(jnp.int32, (ns, tq), 1)) >> SLC_SHIFT
    forced = (blk == 0) | (blk == cur) | (blk == cur - 1)
    imp = jnp.where(blk <= cur, jnp.where(forced, BIG, imp), NEG)
    blk_f = blk.astype(F32)
    sel = jnp.zeros((ns, tq), F32)
    for _ in range(min(SLC_TOP_N, ns)):
        m = jnp.max(imp, axis=0, keepdims=True)
        first = jnp.min(jnp.where(imp == m, blk_f, float(ns)), axis=0, keepdims=True)
        hit = blk_f == first
        sel = jnp.where(hit, 1.0, sel)
        imp = jnp.where(hit, -jnp.inf, imp)
    sel_ref[0, 0] = jnp.where(sel.T > 0.0, 0.0, NEG).astype(BF16)


def _cmp_select(q, k_cmp, v_cmp, wimp_t):
    bsz, s, _ = q.shape
    g = NSA_GROUPS
    nch = k_cmp.shape[2]
    ns = wimp_t.shape[0]
    tq = min(CMP_TQ, s)
    gd = NSA_HPG * NSA_HEAD_DIM
    return pl.pallas_call(
        _cmp_select_kernel,
        grid=(bsz, g, s // tq),
        in_specs=[pl.BlockSpec((1, tq, gd), lambda b, gi, qi: (b, qi, gi)),
                  pl.BlockSpec((1, 1, nch, NSA_HEAD_DIM), lambda b, gi, qi: (b, gi, 0, 0)),
                  pl.BlockSpec((1, 1, nch, NSA_HEAD_DIM), lambda b, gi, qi: (b, gi, 0, 0)),
                  pl.BlockSpec(wimp_t.shape, lambda b, gi, qi: (0, 0))],
        out_specs=[pl.BlockSpec((1, tq, gd), lambda b, gi, qi: (b, qi, gi)),
                   pl.BlockSpec((1, 1, tq, ns), lambda b, gi, qi: (b, gi, qi, 0))],
        out_shape=[jax.ShapeDtypeStruct((bsz, s, g * gd), F32),
                   jax.ShapeDtypeStruct((bsz, g, s, ns), BF16)],
        compiler_params=_cparams(("arbitrary", "arbitrary", "arbitrary")),
        name="nsa_cmp_select",
    )(q, k_cmp, v_cmp, wimp_t)


def _nsa_attn_kernel(q_ref, kst_ref, vs_ref, kw_ref, vw_ref, selb_ref, eexp_ref, ocmp_ref, gt_ref,
                     o_ref, m_ref, acc_ref, s_ref):
    tq = q_ref.shape[1]
    d = NSA_HEAD_DIM
    hpg = NSA_HPG
    qi = pl.program_id(2)
    s0 = qi * tq
    selb = selb_ref[0, 0]
    qh = [(q_ref[0, :, h * d:(h + 1) * d] * LOG2E).astype(BF16) for h in range(hpg)]
    q4 = jnp.concatenate(qh, axis=0)
    q4a = jnp.concatenate([jnp.concatenate([selb, t], axis=1) for t in qh], axis=0)
    qpos = s0 + lax.broadcasted_iota(jnp.int32, (tq, 1), 0)
    qpos4 = jnp.concatenate([qpos] * hpg, axis=0)

    m_ref[...] = jnp.full_like(m_ref, NEG)
    acc_ref[...] = jnp.zeros_like(acc_ref)

    def scores(t):
        return _bdot(q4a, jnp.concatenate([eexp_ref[t], kst_ref[0, 0, t]], axis=0))

    def accumulate(s, t):
        m_prev = m_ref[...]
        m_new = jnp.maximum(m_prev, jnp.max(s, axis=-1, keepdims=True))
        p = jnp.exp2(s - m_new).astype(BF16)
        acc_ref[...] = jnp.exp2(m_prev - m_new) * acc_ref[...] + _bdot(p, vs_ref[0, 0, t])
        m_ref[...] = m_new

    t_diag = s0 // ATT_TK
    s_ref[...] = scores(0)

    def sel_body(t, carry):
        s = s_ref[...]
        s_next = scores(t + 1)
        accumulate(s, t)
        s_ref[...] = s_next
        return carry

    lax.fori_loop(0, t_diag, sel_body, 0)
    kpos = t_diag * ATT_TK + lax.broadcasted_iota(jnp.int32, (1, ATT_TK), 1)
    accumulate(jnp.where(kpos <= qpos4, s_ref[...], NEG), t_diag)
    acc = acc_ref[...]
    o_slc = acc[:, :d] / acc[:, d:d + 1]

    span = WINDOW + tq
    w0 = pl.multiple_of(jnp.maximum(s0 - WINDOW, 0), tq)
    rel = qpos - (w0 + lax.broadcasted_iota(jnp.int32, (1, span), 1))
    wbias = jnp.where((rel >= 0) & (rel < WINDOW), 0.0, NEG)
    s = _dot_nt(q4, kw_ref[0, 0, pl.ds(w0, span), :])
    s = (s.reshape(hpg, tq, span) + wbias[None]).reshape(hpg * tq, span)
    p = jnp.exp2(s - jnp.max(s, axis=-1, keepdims=True)).astype(BF16)
    accw = _bdot(p, vw_ref[0, 0, pl.ds(w0, span), :])
    o_win = accw[:, :d] / accw[:, d:d + 1]

    gts = gt_ref[0]
    for h in range(hpg):
        g0, g1, g2 = (gts[:, 3 * h + j:3 * h + j + 1] for j in range(3))
        rows = slice(h * tq, (h + 1) * tq)
        o_ref[0, :, h * d:(h + 1) * d] = (g0 * ocmp_ref[0, :, h * d:(h + 1) * d]
                                          + g1 * o_slc[rows] + g2 * o_win[rows])


def _nsa_attn(q, kst, vst, kw, vw, selb, eexp, o_cmp, gates):
    bsz, s, _ = q.shape
    g = NSA_GROUPS
    tq = ATT_TQ
    gd = NSA_HPG * NSA_HEAD_DIM
    ns = selb.shape[-1]
    qmap = lambda b, gi, qi: (b, qi, gi)
    tmap = lambda b, gi, qi: (b, gi, 0, 0, 0)
    rmap = lambda b, gi, qi: (b, gi, 0, 0)
    return pl.pallas_call(
        _nsa_attn_kernel,
        grid=(bsz, g, s // tq),
        in_specs=[pl.BlockSpec((1, tq, gd), qmap),
                  pl.BlockSpec((1, 1) + kst.shape[2:], tmap), pl.BlockSpec((1, 1) + vst.shape[2:], tmap),
                  pl.BlockSpec((1, 1) + kw.shape[2:], rmap), pl.BlockSpec((1, 1) + vw.shape[2:], rmap),
                  pl.BlockSpec((1, 1, tq, ns), lambda b, gi, qi: (b, gi, qi, 0)),
                  pl.BlockSpec(eexp.shape, lambda b, gi, qi: (0, 0, 0)),
                  pl.BlockSpec((1, tq, gd), qmap),
                  pl.BlockSpec((1, tq, LANES), qmap)],
        out_specs=pl.BlockSpec((1, tq, gd), qmap),
        out_shape=jax.ShapeDtypeStruct((bsz, s, g * gd), F32),
        scratch_shapes=[pltpu.VMEM((NSA_HPG * tq, 1), F32),
                        pltpu.VMEM((NSA_HPG * tq, 2 * NSA_HEAD_DIM), F32),
                        pltpu.VMEM((NSA_HPG * tq, ATT_TK), F32)],
        compiler_params=_cparams(("arbitrary", "arbitrary", "arbitrary")),
        name="nsa_attn",
    )(q, kst, vst, kw, vw, selb, eexp, o_cmp, gates)


def _rope_tables(pos):
    inv_freq = jnp.power(ROPE_THETA, -jnp.arange(ROT_HALF, dtype=F32) * (2.0 / ROT_DIM))
    ang = pos.astype(F32)[:, None] * inv_freq[None, :]
    cos, sin = jnp.cos(ang), jnp.sin(ang)
    n = pos.shape[0]
    one = jnp.ones((n, NSA_HEAD_DIM - ROT_DIM), F32)
    zero8 = jnp.zeros((n, ROT_HALF), F32)
    zero = jnp.zeros((n, NSA_HEAD_DIM - ROT_DIM), F32)
    c = jnp.concatenate([cos, cos, one], axis=1)
    s1 = jnp.concatenate([-sin, zero8, zero], axis=1)
    s2 = jnp.concatenate([zero8, sin, zero], axis=1)
    rep = LANES // NSA_HEAD_DIM
    return tuple(jnp.tile(t, (1, rep)) for t in (c, s1, s2))


def _importance_weights(ns, nch):
    ratio = SLC_BLOCK // CMP_STRIDE
    span = CMP_BLOCK // CMP_STRIDE
    w = np.zeros((ns, nch), np.float32)
    for j in range(ns):
        for m in range(ratio):
            for n in range(span):
                c = ratio * j + m + n
                if c < nch - 1:
                    w[j, c] += 1.0
    return jnp.asarray(w)


def _block_expansion(nt, ns):
    key_blk = (np.arange(nt)[:, None] * ATT_TK + np.arange(ATT_TK)[None, :]) // SLC_BLOCK
    e = (np.arange(ns)[None, :, None] == key_blk[:, None, :]).astype(np.float32)
    return jnp.asarray(e, dtype=BF16)


def _gla_layer(x2, bsz, seq, w_in, w_gate_up, b_gate, norm_g, w_out, ln_g, ln_b):
    hk = GLA_HEADS * GLA_DK
    hv = GLA_HEADS * GLA_DV
    cuts = np.cumsum([hk, hk, hv, GLA_GATE_RANK]).tolist()
    wq, wk, wv, wg, wr = jnp.split(w_in, cuts, axis=1)
    w_main = jnp.concatenate([wq, wk, wv, wr], axis=1).astype(BF16)
    w_glow = jnp.pad(wg, ((0, 0), (0, LANES - GLA_GATE_RANK))).astype(BF16)
    w_gu = jnp.pad(w_gate_up, ((0, LANES - GLA_GATE_RANK), (0, 0))).astype(BF16)
    q, k, v, r, la = _gla_proj(x2, w_main, w_glow, w_gu, b_gate.reshape(1, hk))
    sh = lambda t: t.reshape(bsz, seq, t.shape[-1])
    o = _gla_core(sh(q), sh(k), sh(v), sh(la), sh(r), norm_g.reshape(1, GLA_DV))
    return _outproj_ln(o.reshape(bsz * seq, hv), x2, w_out.astype(BF16),
                       ln_g.reshape(1, D_MODEL), ln_b.reshape(1, D_MODEL))


def _nsa_layer(x2, bsz, seq, w_in, w_ck1, w_ck2, w_cv1, w_cv2, cmp_pe, w_out, ln_g, ln_b):
    h, g, hpg, d = NSA_HEADS, NSA_GROUPS, NSA_HPG, NSA_HEAD_DIM
    gd = g * d
    cuts = np.cumsum([h * d] + [gd] * 6).tolist()
    wq, wkc, wvc, wks, wvs, wkw, wvw, wgt = jnp.split(w_in, cuts, axis=1)
    w_rope = jnp.concatenate([wq, wks, wkw], axis=1).astype(BF16)
    w_plain = jnp.concatenate([wkc, wvc, wvs, wvw], axis=1).astype(BF16)
    w_gates = jnp.pad(wgt.reshape(D_MODEL, g, hpg * 3), ((0, 0), (0, 0), (0, LANES - hpg * 3)))
    w_gates = w_gates.reshape(D_MODEL, g * LANES).astype(BF16)
    rc, rs1, rs2 = _rope_tables(jnp.arange(seq, dtype=jnp.int32))
    q, ks, kw, kc, vc, vs, vw, gates = _nsa_proj(x2, w_rope, w_plain, w_gates, rc, rs1, rs2, seq)

    nch = seq // CMP_STRIDE
    chunks = lambda t: (t.reshape(bsz, nch, CMP_STRIDE, g, d).transpose(0, 3, 1, 2, 4)
                        .reshape(bsz, g, nch, CMP_STRIDE * d))
    pe = cmp_pe.reshape(1, CMP_BLOCK * d)
    half = CMP_STRIDE * d
    pad2 = lambda w: jnp.pad(w, ((0, 0), (0, LANES - d))).astype(BF16)
    cc, cs1, cs2 = _rope_tables(jnp.arange(nch, dtype=jnp.int32) * CMP_STRIDE + (CMP_BLOCK - 1))
    k_cmp, v_cmp = _compress(chunks(kc), chunks(vc), pe[:, :half], pe[:, half:],
                             w_ck1.astype(BF16), pad2(w_ck2), w_cv1.astype(BF16), pad2(w_cv2), cc, cs1, cs2)

    ns = seq // SLC_BLOCK
    sh = lambda t: t.reshape(bsz, seq, t.shape[-1])
    o_cmp, selb = _cmp_select(sh(q), k_cmp, v_cmp, _importance_weights(ns, nch))

    def heads(t):
        return t.astype(BF16).reshape(bsz, seq, g, d).transpose(0, 2, 1, 3)

    def with_ones(t):
        return jnp.concatenate([t, jnp.ones_like(t)], axis=-1)

    nt = seq // ATT_TK
    kst = ks.astype(BF16).reshape(bsz, nt, ATT_TK, g, d).transpose(0, 3, 1, 4, 2)
    vst = with_ones(heads(vs)).reshape(bsz, g, nt, ATT_TK, 2 * d)
    o = _nsa_attn(sh(q), kst, vst, heads(kw), with_ones(heads(vw)),
                  selb, _block_expansion(nt, ns), o_cmp, sh(gates))
    return _outproj_ln(o.reshape(bsz * seq, h * d), x2, w_out.astype(BF16),
                       ln_g.reshape(1, D_MODEL), ln_b.reshape(1, D_MODEL))


def _moe_layer(x2, w_router_pad, b_router_pad, w_gate, w_up, w_down, ln_g, ln_b):
    gate = _router(x2, w_router_pad, b_router_pad)
    wgu = jnp.concatenate([w_gate, w_up], axis=-1).astype(BF16)
    return _moe_ffn_ln(x2, gate, wgu, w_down.astype(BF16), ln_g.reshape(1, D_MODEL), ln_b.reshape(1, D_MODEL))


def kernel(x, gla_w_in, gla_w_gate_up, gla_b_gate, gla_norm_g, gla_w_out, nsa_w_in, nsa_w_cmp_k1, nsa_w_cmp_k2,
           nsa_w_cmp_v1, nsa_w_cmp_v2, nsa_cmp_pe, nsa_w_out, moe_w_router, moe_b_router, moe_w_gate, moe_w_up,
           moe_w_down, ln_g, ln_b):
    bsz, seq, _ = x.shape
    x2 = x.reshape(bsz * seq, D_MODEL)
    w_router_pad = jnp.pad(moe_w_router, ((0, 0), (0, LANES - N_EXPERTS)))
    b_router_pad = jnp.pad(moe_b_router, (0, LANES - N_EXPERTS)).reshape(1, LANES)
    for i in range(DEPTH):
        j = i // 2
        if i % 2 == 0:
            x2 = _gla_layer(x2, bsz, seq, gla_w_in[j], gla_w_gate_up[j], gla_b_gate[j], gla_norm_g[j],
                            gla_w_out[j], ln_g[i, 0], ln_b[i, 0])
        else:
            x2 = _nsa_layer(x2, bsz, seq, nsa_w_in[j], nsa_w_cmp_k1[j], nsa_w_cmp_k2[j], nsa_w_cmp_v1[j],
                            nsa_w_cmp_v2[j], nsa_cmp_pe[j], nsa_w_out[j], ln_g[i, 0], ln_b[i, 0])
        x2 = _moe_layer(x2, w_router_pad, b_router_pad, moe_w_gate[i], moe_w_up[i], moe_w_down[i],
                        ln_g[i, 1], ln_b[i, 1])
    return x2.reshape(bsz, seq, D_MODEL)
```

```python
import functools

import numpy as np
import jax
import jax.numpy as jnp
from jax import lax
from jax.experimental import pallas as pl
from jax.experimental.pallas import tpu as pltpu

F32 = jnp.float32
BF16 = jnp.bfloat16
HIGHEST = lax.Precision.HIGHEST

D_MODEL = 1024
DEPTH = 2

GLA_HEADS = 4
GLA_DK = D_MODEL // 2 // GLA_HEADS
GLA_DV = D_MODEL // GLA_HEADS
GLA_GATE_RANK = 16
GLA_TAU = 16.0
GLA_CHUNK = 64

NSA_HEADS = 16
NSA_GROUPS = 4
NSA_HPG = NSA_HEADS // NSA_GROUPS
NSA_HEAD_DIM = D_MODEL // NSA_HEADS
CMP_BLOCK = 32
CMP_STRIDE = 16
CMP_HIDDEN = 256
SLC_BLOCK = 64
SLC_SHIFT = SLC_BLOCK.bit_length() - 1
SLC_TOP_N = 16
WINDOW = 512

ROPE_THETA = 500000.0
ROT_DIM = NSA_HEAD_DIM // 4
ROT_HALF = ROT_DIM // 2

N_EXPERTS = 16
N_EXPERT_GROUPS = 4
EXPERTS_PER_GROUP = N_EXPERTS // N_EXPERT_GROUPS
D_FF_EXPERT = D_MODEL // 4

DN_ALPHA = (2 * DEPTH) ** 0.25
LN_EPS = 1e-5
NEG = -1e30
BIG = 1e30
LOG2E = 1.4426950408889634

LANES = 128
VMEM_LIMIT = 48 * 1024 * 1024

PROJ_TM = 256
GLA_T = 512
MOE_TM = 1024
CMP_TQ = 256
CMP_CW = 128
ATT_TQ = 128
ATT_TK = 512
WIN_TK = 128


def _cparams(sem):
    return pltpu.CompilerParams(dimension_semantics=sem, vmem_limit_bytes=VMEM_LIMIT)


def _bdot(a, b):
    return jnp.dot(a, b, preferred_element_type=F32)


def _dot_nt(a, b, precision=None):
    return lax.dot_general(a, b, (((1,), (1,)), ((), ())), preferred_element_type=F32, precision=precision)


def _dot_tn(a, b):
    return lax.dot_general(a, b, (((0,), (0,)), ((), ())), preferred_element_type=F32)


def _layer_norm(z, g, b):
    mu = jnp.mean(z, axis=-1, keepdims=True)
    zc = z - mu
    var = jnp.mean(zc * zc, axis=-1, keepdims=True)
    return zc * lax.rsqrt(var + LN_EPS) * g + b


def _silu(t):
    return t * jax.nn.sigmoid(t)


def _rope128(t, c, s1, s2):
    up = pltpu.roll(t, LANES - ROT_HALF, axis=1)
    dn = pltpu.roll(t, ROT_HALF, axis=1)
    return t * c + up * s1 + dn * s2


def _gla_proj_kernel(x_ref, wm_ref, wg_ref, wgu_ref, bg_ref, q_ref, k_ref, v_ref, r_ref, la_ref):
    xb = x_ref[...].astype(BF16)
    hk = GLA_HEADS * GLA_DK
    hv = GLA_HEADS * GLA_DV
    q_ref[...] = _bdot(xb, wm_ref[:, 0:hk])
    k_ref[...] = _bdot(xb, wm_ref[:, hk:2 * hk])
    v_ref[...] = _bdot(xb, wm_ref[:, 2 * hk:2 * hk + hv])
    r_ref[...] = _bdot(xb, wm_ref[:, 2 * hk + hv:2 * hk + 2 * hv])
    g_low = _bdot(xb, wg_ref[...])
    z = _bdot(g_low.astype(BF16), wgu_ref[...]) + bg_ref[...]
    log_sig = jnp.minimum(z, 0.0) - jnp.log1p(jnp.exp(-jnp.abs(z)))
    la_ref[...] = log_sig * (1.0 / GLA_TAU)


def _gla_proj(x2, w_main, w_glow, w_gu, b_gate):
    n = x2.shape[0]
    hk = GLA_HEADS * GLA_DK
    hv = GLA_HEADS * GLA_DV
    tm = PROJ_TM
    row = lambda i: (i, 0)
    full = lambda i: (0, 0)
    return pl.pallas_call(
        _gla_proj_kernel,
        grid=(n // tm,),
        in_specs=[pl.BlockSpec((tm, D_MODEL), row),
                  pl.BlockSpec(w_main.shape, full),
                  pl.BlockSpec(w_glow.shape, full),
                  pl.BlockSpec(w_gu.shape, full),
                  pl.BlockSpec(b_gate.shape, full)],
        out_specs=[pl.BlockSpec((tm, hk), row), pl.BlockSpec((tm, hk), row),
                   pl.BlockSpec((tm, hv), row), pl.BlockSpec((tm, hv), row),
                   pl.BlockSpec((tm, hk), row)],
        out_shape=[jax.ShapeDtypeStruct((n, hk), F32), jax.ShapeDtypeStruct((n, hk), F32),
                   jax.ShapeDtypeStruct((n, hv), F32), jax.ShapeDtypeStruct((n, hv), F32),
                   jax.ShapeDtypeStruct((n, hk), F32)],
        compiler_params=_cparams(("arbitrary",)),
        name="gla_proj",
    )(x2, w_main, w_glow, w_gu, b_gate)


def _gla_core_kernel(q_ref, k_ref, v_ref, la_ref, r_ref, g_ref, o_ref, st_ref):
    @pl.when(pl.program_id(2) == 0)
    def _():
        st_ref[...] = jnp.zeros_like(st_ref)

    c = GLA_CHUNK
    row = lax.broadcasted_iota(jnp.int32, (c, c), 0)
    col = lax.broadcasted_iota(jnp.int32, (c, c), 1)
    tril = row >= col
    tril_f = tril.astype(F32)
    g = g_ref[...]
    for ci in range(GLA_T // c):
        sl = pl.ds(ci * c, c)
        la = la_ref[0, sl, :]
        b = jnp.dot(tril_f, la, preferred_element_type=F32, precision=HIGHEST)
        b_last = b[c - 1:c, :]
        q = q_ref[0, sl, :] * (GLA_DK ** -0.5)
        k = k_ref[0, sl, :]
        vb = v_ref[0, sl, :].astype(BF16)
        q_dec = (q * jnp.exp(b)).astype(BF16)
        k_neg = (k * jnp.exp(-b)).astype(BF16)
        k_dec = (k * jnp.exp(b_last - b)).astype(BF16)
        s = jnp.where(tril, _dot_nt(q_dec, k_neg), 0.0)
        st = st_ref[...]
        o = _bdot(s.astype(BF16), vb) + _dot_nt(q_dec, st.astype(BF16))
        st_ref[...] = jnp.exp(b_last) * st + _dot_tn(vb, k_dec)
        ms = jnp.mean(o * o, axis=-1, keepdims=True)
        o = o * lax.rsqrt(ms + LN_EPS) * g
        o_ref[0, sl, :] = o * _silu(r_ref[0, sl, :])


def _gla_core(q, k, v, la, r, norm_g):
    bsz, s, _ = q.shape
    t = GLA_T
    kmap = lambda b, h, n: (b, n, h)
    return pl.pallas_call(
        _gla_core_kernel,
        grid=(bsz, GLA_HEADS, s // t),
        in_specs=[pl.BlockSpec((1, t, GLA_DK), kmap), pl.BlockSpec((1, t, GLA_DK), kmap),
                  pl.BlockSpec((1, t, GLA_DV), kmap), pl.BlockSpec((1, t, GLA_DK), kmap),
                  pl.BlockSpec((1, t, GLA_DV), kmap),
                  pl.BlockSpec((1, GLA_DV), lambda b, h, n: (0, 0))],
        out_specs=pl.BlockSpec((1, t, GLA_DV), kmap),
        out_shape=jax.ShapeDtypeStruct((bsz, s, GLA_HEADS * GLA_DV), F32),
        scratch_shapes=[pltpu.VMEM((GLA_DV, GLA_DK), F32)],
        compiler_params=_cparams(("arbitrary", "arbitrary", "arbitrary")),
        name="gla_core",
    )(q, k, v, la, r, norm_g)


def _outproj_ln_kernel(h_ref, x_ref, w_ref, g_ref, b_ref, o_ref):
    y = _bdot(h_ref[...].astype(BF16), w_ref[...])
    o_ref[...] = _layer_norm(DN_ALPHA * x_ref[...] + y, g_ref[...], b_ref[...])


def _outproj_ln(h2, x2, w, g, b):
    n = x2.shape[0]
    tm = PROJ_TM
    row = lambda i: (i, 0)
    full = lambda i: (0, 0)
    return pl.pallas_call(
        _outproj_ln_kernel,
        grid=(n // tm,),
        in_specs=[pl.BlockSpec((tm, h2.shape[1]), row), pl.BlockSpec((tm, D_MODEL), row),
                  pl.BlockSpec(w.shape, full), pl.BlockSpec((1, D_MODEL), full), pl.BlockSpec((1, D_MODEL), full)],
        out_specs=pl.BlockSpec((tm, D_MODEL), row),
        out_shape=jax.ShapeDtypeStruct((n, D_MODEL), F32),
        compiler_params=_cparams(("arbitrary",)),
        name="outproj_ln",
    )(h2, x2, w, g, b)


def _lane_xor(v, k, lane):
    up = pltpu.roll(v, LANES - k, axis=1)
    dn = pltpu.roll(v, k, axis=1)
    return jnp.where((lane & k) == 0, up, dn)


def _router_kernel(x_ref, w_ref, b_ref, gate_ref):
    logits = jnp.dot(x_ref[...], w_ref[...], preferred_element_type=F32, precision=HIGHEST)
    s = jax.nn.sigmoid(logits)
    a = s + b_ref[...]
    lane = lax.broadcasted_iota(jnp.int32, a.shape, 1)
    a1 = _lane_xor(a, 1, lane)
    a2 = _lane_xor(a, 2, lane)
    a3 = _lane_xor(a1, 2, lane)
    p, q = jnp.maximum(a, a1), jnp.minimum(a, a1)
    r, t = jnp.maximum(a2, a3), jnp.minimum(a2, a3)
    gs = jnp.maximum(p, r) + jnp.maximum(jnp.minimum(p, r), jnp.maximum(q, t))
    g1 = _lane_xor(gs, 4, lane)
    g2 = _lane_xor(gs, 8, lane)
    g3 = _lane_xor(g1, 8, lane)

    def beats(other, other_first):
        return (other > gs) | ((other == gs) & other_first)

    lose = (beats(g1, (lane & 4) != 0) | beats(g2, (lane & 8) != 0) | beats(g3, (lane & 8) != 0))

    def ahead(other, other_first):
        return jnp.where((other > a) | ((other == a) & other_first), 1.0, 0.0)

    rank = ahead(a1, (lane & 1) != 0) + ahead(a2, (lane & 2) != 0) + ahead(a3, (lane & 2) != 0)
    sel = jnp.logical_not(lose) & (rank < 2.0) & (lane < N_EXPERTS)
    ssel = jnp.where(sel, s, 0.0)
    s1 = _lane_xor(ssel, 1, lane)
    tot = (ssel + s1) + (_lane_xor(ssel, 2, lane) + _lane_xor(s1, 2, lane))
    gate_ref[...] = jnp.where(sel, s / tot, 0.0)


def _router(x2, w_pad, b_pad):
    n = x2.shape[0]
    tm = PROJ_TM
    return pl.pallas_call(
        _router_kernel,
        grid=(n // tm,),
        in_specs=[pl.BlockSpec((tm, D_MODEL), lambda i: (i, 0)),
                  pl.BlockSpec(w_pad.shape, lambda i: (0, 0)),
                  pl.BlockSpec(b_pad.shape, lambda i: (0, 0))],
        out_specs=pl.BlockSpec((tm, LANES), lambda i: (i, 0)),
        out_shape=jax.ShapeDtypeStruct((n, LANES), F32),
        compiler_params=_cparams(("arbitrary",)),
        name="moe_router",
    )(x2, w_pad, b_pad)


def _moe_kernel(x_ref, gate_ref, wgu_ref, wd_ref, lg_ref, lb_ref, o_ref, xb_ref, acc_ref):
    e = pl.program_id(1)

    @pl.when(e == 0)
    def _():
        xb_ref[...] = x_ref[...].astype(BF16)
        acc_ref[...] = jnp.zeros_like(acc_ref)

    hgu = _bdot(xb_ref[...], wgu_ref[0])
    lane = lax.broadcasted_iota(jnp.int32, gate_ref.shape, 1)
    gcol = jnp.sum(jnp.where(lane == e, gate_ref[...], 0.0), axis=1, keepdims=True)
    h = _silu(hgu[:, :D_FF_EXPERT]) * hgu[:, D_FF_EXPERT:] * gcol
    acc_ref[...] += _bdot(h.astype(BF16), wd_ref[0])

    @pl.when(e == N_EXPERTS - 1)
    def _():
        o_ref[...] = _layer_norm(DN_ALPHA * x_ref[...] + acc_ref[...], lg_ref[...], lb_ref[...])


def _moe_ffn_ln(x2, gate, wgu, wd, lg, lb):
    n = x2.shape[0]
    tm = min(MOE_TM, n)
    return pl.pallas_call(
        _moe_kernel,
        grid=(n // tm, N_EXPERTS),
        in_specs=[pl.BlockSpec((tm, D_MODEL), lambda i, e: (i, 0)),
                  pl.BlockSpec((tm, LANES), lambda i, e: (i, 0)),
                  pl.BlockSpec((1, D_MODEL, 2 * D_FF_EXPERT), lambda i, e: (e, 0, 0)),
                  pl.BlockSpec((1, D_FF_EXPERT, D_MODEL), lambda i, e: (e, 0, 0)),
                  pl.BlockSpec((1, D_MODEL), lambda i, e: (0, 0)),
                  pl.BlockSpec((1, D_MODEL), lambda i, e: (0, 0))],
        out_specs=pl.BlockSpec((tm, D_MODEL), lambda i, e: (i, 0)),
        out_shape=jax.ShapeDtypeStruct((n, D_MODEL), F32),
        scratch_shapes=[pltpu.VMEM((tm, D_MODEL), BF16), pltpu.VMEM((tm, D_MODEL), F32)],
        compiler_params=_cparams(("arbitrary", "arbitrary")),
        name="moe_ffn",
    )(x2, gate, wgu, wd, lg, lb)


def _nsa_proj_kernel(x_ref, wr_ref, wp_ref, wgt_ref, c_ref, s1_ref, s2_ref,
                     q_ref, ks_ref, kw_ref, kc_ref, vc_ref, vs_ref, vw_ref, gt_ref):
    xb = x_ref[...].astype(BF16)
    c, s1, s2 = c_ref[...], s1_ref[...], s2_ref[...]
    nq = NSA_HEADS * NSA_HEAD_DIM // LANES
    ng = NSA_GROUPS * NSA_HEAD_DIM // LANES
    for j in range(nq + 2 * ng):
        t = _rope128(_bdot(xb, wr_ref[:, j * LANES:(j + 1) * LANES]), c, s1, s2)
        if j < nq:
            t = (t * (NSA_HEAD_DIM ** -0.5 * LOG2E)).astype(BF16)
            q_ref[0, 2 * j] = t[:, :NSA_HEAD_DIM]
            q_ref[0, 2 * j + 1] = t[:, NSA_HEAD_DIM:]
        elif j < nq + ng:
            ks_ref[:, (j - nq) * LANES:(j - nq + 1) * LANES] = t
        else:
            kw_ref[:, (j - nq - ng) * LANES:(j - nq - ng + 1) * LANES] = t
    gd = NSA_GROUPS * NSA_HEAD_DIM
    for j, ref in enumerate((kc_ref, vc_ref, vs_ref, vw_ref)):
        ref[...] = _bdot(xb, wp_ref[:, j * gd:(j + 1) * gd])
    gt_ref[...] = jax.nn.sigmoid(_bdot(xb, wgt_ref[...]))


def _nsa_proj(x2, w_rope, w_plain, w_gates, rc, rs1, rs2, seq):
    n = x2.shape[0]
    tm = PROJ_TM
    gd = NSA_GROUPS * NSA_HEAD_DIM
    row = lambda i: (i, 0)
    full = lambda i: (0, 0)
    per = seq // tm
    pos = lambda i: (i % per, 0)
    out_w = [gd, gd, gd, gd, gd, gd, NSA_GROUPS * LANES]
    q_spec = pl.BlockSpec((1, NSA_HEADS, tm, NSA_HEAD_DIM), lambda i: (i // per, 0, i % per, 0))
    q_shape = jax.ShapeDtypeStruct((n // seq, NSA_HEADS, seq, NSA_HEAD_DIM), BF16)
    return pl.pallas_call(
        _nsa_proj_kernel,
        grid=(n // tm,),
        in_specs=[pl.BlockSpec((tm, D_MODEL), row),
                  pl.BlockSpec(w_rope.shape, full), pl.BlockSpec(w_plain.shape, full),
                  pl.BlockSpec(w_gates.shape, full),
                  pl.BlockSpec((tm, LANES), pos), pl.BlockSpec((tm, LANES), pos), pl.BlockSpec((tm, LANES), pos)],
        out_specs=[q_spec] + [pl.BlockSpec((tm, w), row) for w in out_w],
        out_shape=[q_shape] + [jax.ShapeDtypeStruct((n, w), F32) for w in out_w],
        compiler_params=_cparams(("arbitrary",)),
        name="nsa_proj",
    )(x2, w_rope, w_plain, w_gates, rc, rs1, rs2)


def _compress_kernel(ck_ref, cv_ref, pet_ref, peb_ref, wk1_ref, wk2_ref, wv1_ref, wv2_ref,
                     c_ref, s1_ref, s2_ref, ko_ref, vo_ref):
    nch = ck_ref.shape[2]
    half = CMP_STRIDE * NSA_HEAD_DIM
    pet, peb = pet_ref[...], peb_ref[...]

    def mlp(ch, w1_ref, w2_ref):
        a = _bdot((ch + pet).astype(BF16), w1_ref[0:half, :])
        bm = _bdot((ch + peb).astype(BF16), w1_ref[half:2 * half, :])
        h = _silu(a + pltpu.roll(bm, nch - 1, axis=0))
        return _bdot(h.astype(BF16), w2_ref[...])

    kc = _rope128(mlp(ck_ref[0, 0], wk1_ref, wk2_ref), c_ref[...], s1_ref[...], s2_ref[...])
    ko_ref[0, 0] = kc[:, :NSA_HEAD_DIM].astype(BF16)
    vo_ref[0, 0] = mlp(cv_ref[0, 0], wv1_ref, wv2_ref)[:, :NSA_HEAD_DIM].astype(BF16)


def _compress(ck, cv, pet, peb, wk1, wk2, wv1, wv2, cc, cs1, cs2):
    bsz, g, nch, width = ck.shape
    blk = lambda b, gi: (b, gi, 0, 0)
    full = lambda b, gi: (0, 0)
    return pl.pallas_call(
        _compress_kernel,
        grid=(bsz, g),
        in_specs=[pl.BlockSpec((1, 1, nch, width), blk), pl.BlockSpec((1, 1, nch, width), blk),
                  pl.BlockSpec(pet.shape, full), pl.BlockSpec(peb.shape, full),
                  pl.BlockSpec(wk1.shape, full), pl.BlockSpec(wk2.shape, full),
                  pl.BlockSpec(wv1.shape, full), pl.BlockSpec(wv2.shape, full),
                  pl.BlockSpec(cc.shape, full), pl.BlockSpec(cs1.shape, full), pl.BlockSpec(cs2.shape, full)],
        out_specs=[pl.BlockSpec((1, 1, nch, NSA_HEAD_DIM), blk), pl.BlockSpec((1, 1, nch, NSA_HEAD_DIM), blk)],
        out_shape=[jax.ShapeDtypeStruct((bsz, g, nch, NSA_HEAD_DIM), BF16)] * 2,
        compiler_params=_cparams(("arbitrary", "arbitrary")),
        name="nsa_compress",
    )(ck, cv, pet, peb, wk1, wk2, wv1, wv2, cc, cs1, cs2)


def _cmp_select_kernel(q_ref, kc_ref, vc_ref, wimp_ref, o_ref, sel_ref):
    tq = q_ref.shape[2]
    nch = kc_ref.shape[2]
    ns = wimp_ref.shape[0]
    s0 = pl.program_id(2) * tq
    ratio = SLC_BLOCK // CMP_STRIDE
    n_vis = (s0 + tq - CMP_BLOCK) // CMP_STRIDE + 1
    n_chunks = (n_vis + CMP_CW - 1) // CMP_CW

    def body(w):
        nb = w // ratio
        qpos_c = s0 + lax.broadcasted_iota(jnp.int32, (tq, 1), 0)
        cend = lax.broadcasted_iota(jnp.int32, (1, w), 1) * CMP_STRIDE + (CMP_BLOCK - 1)
        cmask = cend <= qpos_c
        anyvis = (qpos_c >= CMP_BLOCK - 1).astype(F32)
        kc = kc_ref[0, 0, :w, :]
        vc = vc_ref[0, 0, :w, :]
        pg = jnp.zeros((tq, w), F32)
        for h in range(NSA_HPG):
            s = jnp.where(cmask, _dot_nt(q_ref[0, h], kc), NEG)
            e = jnp.exp2(s - jnp.max(s, axis=-1, keepdims=True))
            p = e * (anyvis / jnp.sum(e, axis=-1, keepdims=True))
            o_ref[0, h] = _bdot(p.astype(BF16), vc)
            pg = pg + p

        pg_hi = pg.astype(BF16)
        pg_lo = (pg - pg_hi.astype(F32)).astype(BF16)
        wimp = wimp_ref[:nb, :w]
        imp = _dot_nt(wimp, pg_hi) + _dot_nt(wimp, pg_lo)
        blk = lax.broadcasted_iota(jnp.int32, (nb, tq), 0)
        cur = (s0 + lax.broadcasted_iota(jnp.int32, (nb, tq), 1)) >> SLC_SHIFT
        forced = (blk == 0) | (blk == cur) | (blk == cur - 1)
        imp = jnp.where(blk <= cur, jnp.where(forced, BIG, imp), NEG)
        blk_f = blk.astype(F32)
        sel = jnp.zeros((nb, tq), F32)
        for _ in range(min(SLC_TOP_N, nb)):
            m = jnp.max(imp, axis=0, keepdims=True)
            first = jnp.min(jnp.where(imp == m, blk_f, float(nb)), axis=0, keepdims=True)
            hit = blk_f == first
            sel = jnp.where(hit, 1.0, sel)
            imp = jnp.where(hit, -jnp.inf, imp)
        if nb < ns:
            sel = jnp.concatenate([sel, jnp.zeros((ns - nb, tq), F32)], axis=0)
        sel_ref[0, 0] = jnp.where(sel.T > 0.0, 0.0, NEG).astype(BF16)

    for nv in range(1, nch // CMP_CW + 1):
        pl.when(n_chunks == nv)(functools.partial(body, nv * CMP_CW))


def _cmp_select(q, k_cmp, v_cmp, wimp_t):
    bsz, _, s, d = q.shape
    g = NSA_GROUPS
    nch = k_cmp.shape[2]
    ns = wimp_t.shape[0]
    tq = min(CMP_TQ, s)
    hmap = lambda b, gi, qi: (b, gi, qi, 0)
    return pl.pallas_call(
        _cmp_select_kernel,
        grid=(bsz, g, s // tq),
        in_specs=[pl.BlockSpec((1, NSA_HPG, tq, d), hmap),
                  pl.BlockSpec((1, 1, nch, d), lambda b, gi, qi: (b, gi, 0, 0)),
                  pl.BlockSpec((1, 1, nch, d), lambda b, gi, qi: (b, gi, 0, 0)),
                  pl.BlockSpec(wimp_t.shape, lambda b, gi, qi: (0, 0))],
        out_specs=[pl.BlockSpec((1, NSA_HPG, tq, d), hmap),
                   pl.BlockSpec((1, 1, tq, ns), hmap)],
        out_shape=[jax.ShapeDtypeStruct((bsz, NSA_HEADS, s, d), F32),
                   jax.ShapeDtypeStruct((bsz, g, s, ns), BF16)],
        compiler_params=_cparams(("arbitrary", "arbitrary", "arbitrary")),
        name="nsa_cmp_select",
    )(q, k_cmp, v_cmp, wimp_t)


def _nsa_attn_kernel(q_ref, kst_ref, vs_ref, kw_ref, vw_ref, selb_ref, eexp_ref, ocmp_ref, gt_ref,
                     o_ref, m_ref, acc_ref, s_ref, p_ref, a_ref):
    hpg, tq, d = q_ref.shape[1:]
    rows = hpg * tq
    s0 = pl.program_id(2) * tq
    q4 = q_ref[0].reshape(rows, d)
    q4a = jnp.concatenate([jnp.concatenate([selb_ref[0, 0]] * hpg, axis=0), q4], axis=1)
    qpos = s0 + lax.broadcasted_iota(jnp.int32, (tq, 1), 0)

    def scores(t):
        return _bdot(q4a, jnp.concatenate([eexp_ref[t], kst_ref[0, 0, t]], axis=0))

    def flush(t):
        acc_ref[...] = a_ref[...] * acc_ref[...] + _bdot(p_ref[...], vs_ref[0, 0, t])

    def softmax_step(s):
        m_prev = m_ref[...]
        m_new = jnp.maximum(m_prev, jnp.max(s, axis=-1, keepdims=True))
        a_ref[...] = jnp.exp2(m_prev - m_new)
        p_ref[...] = jnp.exp2(s - m_new).astype(BF16)
        m_ref[...] = m_new

    m_ref[...] = jnp.full_like(m_ref, NEG)
    acc_ref[...] = jnp.zeros_like(acc_ref)
    p_ref[...] = jnp.zeros_like(p_ref)
    a_ref[...] = jnp.ones_like(a_ref)
    t_diag = s0 // ATT_TK
    s_ref[...] = scores(0)

    def sel_body(t, carry):
        s = s_ref[...]
        s_next = scores(t + 1)
        flush(jnp.maximum(t - 1, 0))
        softmax_step(s)
        s_ref[...] = s_next
        return carry

    lax.fori_loop(0, t_diag, sel_body, 0)

    span = WINDOW + tq
    w0 = pl.multiple_of(jnp.maximum(s0 - WINDOW, 0), tq)
    rel = qpos - (w0 + lax.broadcasted_iota(jnp.int32, (1, span), 1))
    wbias = jnp.where((rel >= 0) & (rel < WINDOW), 0.0, NEG)
    sw = _dot_nt(q4, kw_ref[0, 0, pl.ds(w0, span), :])
    sw = (sw.reshape(hpg, tq, span) + wbias[None]).reshape(rows, span)
    pw = jnp.exp2(sw - jnp.max(sw, axis=-1, keepdims=True)).astype(BF16)
    accw = _bdot(pw, vw_ref[0, 0, pl.ds(w0, span), :])
    o_win = (accw[:, :d] / accw[:, d:d + 1]).reshape(hpg, tq, d)

    flush(jnp.maximum(t_diag - 1, 0))
    kpos = t_diag * ATT_TK + lax.broadcasted_iota(jnp.int32, (1, ATT_TK), 1)
    s = s_ref[...].reshape(hpg, tq, ATT_TK)
    softmax_step(jnp.where((kpos <= qpos)[None], s, NEG).reshape(rows, ATT_TK))
    flush(t_diag)
    acc = acc_ref[...]
    o_slc = (acc[:, :d] / acc[:, d:d + 1]).reshape(hpg, tq, d)

    gts = gt_ref[0]
    for h in range(hpg):
        g0, g1, g2 = (gts[:, 3 * h + j:3 * h + j + 1] for j in range(3))
        o_ref[0, h] = g0 * ocmp_ref[0, h] + g1 * o_slc[h] + g2 * o_win[h]


def _nsa_attn(q, kst, vst, kw, vw, selb, eexp, o_cmp, gates):
    bsz, _, s, d = q.shape
    g = NSA_GROUPS
    tq = ATT_TQ
    rows = NSA_HPG * tq
    ns = selb.shape[-1]
    hmap = lambda b, gi, qi: (b, gi, qi, 0)
    tmap = lambda b, gi, qi: (b, gi, 0, 0, 0)
    rmap = lambda b, gi, qi: (b, gi, 0, 0)
    return pl.pallas_call(
        _nsa_attn_kernel,
        grid=(bsz, g, s // tq),
        in_specs=[pl.BlockSpec((1, NSA_HPG, tq, d), hmap),
                  pl.BlockSpec((1, 1) + kst.shape[2:], tmap), pl.BlockSpec((1, 1) + vst.shape[2:], tmap),
                  pl.BlockSpec((1, 1) + kw.shape[2:], rmap), pl.BlockSpec((1, 1) + vw.shape[2:], rmap),
                  pl.BlockSpec((1, 1, tq, ns), hmap),
                  pl.BlockSpec(eexp.shape, lambda b, gi, qi: (0, 0, 0)),
                  pl.BlockSpec((1, NSA_HPG, tq, d), hmap),
                  pl.BlockSpec((1, tq, LANES), lambda b, gi, qi: (b, qi, gi))],
        out_specs=pl.BlockSpec((1, NSA_HPG, tq, d), hmap),
        out_shape=jax.ShapeDtypeStruct((bsz, NSA_HEADS, s, d), F32),
        scratch_shapes=[pltpu.VMEM((rows, 1), F32),
                        pltpu.VMEM((rows, 2 * d), F32),
                        pltpu.VMEM((rows, ATT_TK), F32),
                        pltpu.VMEM((rows, ATT_TK), BF16),
                        pltpu.VMEM((rows, 1), F32)],
        compiler_params=_cparams(("arbitrary", "arbitrary", "arbitrary")),
        name="nsa_attn",
    )(q, kst, vst, kw, vw, selb, eexp, o_cmp, gates)


def _rope_tables(pos):
    inv_freq = jnp.power(ROPE_THETA, -jnp.arange(ROT_HALF, dtype=F32) * (2.0 / ROT_DIM))
    ang = pos.astype(F32)[:, None] * inv_freq[None, :]
    cos, sin = jnp.cos(ang), jnp.sin(ang)
    n = pos.shape[0]
    one = jnp.ones((n, NSA_HEAD_DIM - ROT_DIM), F32)
    zero8 = jnp.zeros((n, ROT_HALF), F32)
    zero = jnp.zeros((n, NSA_HEAD_DIM - ROT_DIM), F32)
    c = jnp.concatenate([cos, cos, one], axis=1)
    s1 = jnp.concatenate([-sin, zero8, zero], axis=1)
    s2 = jnp.concatenate([zero8, sin, zero], axis=1)
    rep = LANES // NSA_HEAD_DIM
    return tuple(jnp.tile(t, (1, rep)) for t in (c, s1, s2))


def _importance_weights(ns, nch):
    ratio = SLC_BLOCK // CMP_STRIDE
    span = CMP_BLOCK // CMP_STRIDE
    w = np.zeros((ns, nch), np.float32)
    for j in range(ns):
        for m in range(ratio):
            for n in range(span):
                c = ratio * j + m + n
                if c < nch - 1:
                    w[j, c] += 1.0
    return jnp.asarray(w, dtype=BF16)


def _block_expansion(nt, ns):
    key_blk = (np.arange(nt)[:, None] * ATT_TK + np.arange(ATT_TK)[None, :]) // SLC_BLOCK
    e = (np.arange(ns)[None, :, None] == key_blk[:, None, :]).astype(np.float32)
    return jnp.asarray(e, dtype=BF16)


def _gla_layer(x2, bsz, seq, w_in, w_gate_up, b_gate, norm_g, w_out, ln_g, ln_b):
    hk = GLA_HEADS * GLA_DK
    hv = GLA_HEADS * GLA_DV
    cuts = np.cumsum([hk, hk, hv, GLA_GATE_RANK]).tolist()
    wq, wk, wv, wg, wr = jnp.split(w_in, cuts, axis=1)
    w_main = jnp.concatenate([wq, wk, wv, wr], axis=1).astype(BF16)
    w_glow = jnp.pad(wg, ((0, 0), (0, LANES - GLA_GATE_RANK))).astype(BF16)
    w_gu = jnp.pad(w_gate_up, ((0, LANES - GLA_GATE_RANK), (0, 0))).astype(BF16)
    q, k, v, r, la = _gla_proj(x2, w_main, w_glow, w_gu, b_gate.reshape(1, hk))
    sh = lambda t: t.reshape(bsz, seq, t.shape[-1])
    o = _gla_core(sh(q), sh(k), sh(v), sh(la), sh(r), norm_g.reshape(1, GLA_DV))
    return _outproj_ln(o.reshape(bsz * seq, hv), x2, w_out.astype(BF16),
                       ln_g.reshape(1, D_MODEL), ln_b.reshape(1, D_MODEL))


def _nsa_layer(x2, bsz, seq, w_in, w_ck1, w_ck2, w_cv1, w_cv2, cmp_pe, w_out, ln_g, ln_b):
    h, g, hpg, d = NSA_HEADS, NSA_GROUPS, NSA_HPG, NSA_HEAD_DIM
    gd = g * d
    cuts = np.cumsum([h * d] + [gd] * 6).tolist()
    wq, wkc, wvc, wks, wvs, wkw, wvw, wgt = jnp.split(w_in, cuts, axis=1)
    w_rope = jnp.concatenate([wq, wks, wkw], axis=1).astype(BF16)
    w_plain = jnp.concatenate([wkc, wvc, wvs, wvw], axis=1).astype(BF16)
    w_gates = jnp.pad(wgt.reshape(D_MODEL, g, hpg * 3), ((0, 0), (0, 0), (0, LANES - hpg * 3)))
    w_gates = w_gates.reshape(D_MODEL, g * LANES).astype(BF16)
    rc, rs1, rs2 = _rope_tables(jnp.arange(seq, dtype=jnp.int32))
    q, ks, kw, kc, vc, vs, vw, gates = _nsa_proj(x2, w_rope, w_plain, w_gates, rc, rs1, rs2, seq)

    nch = seq // CMP_STRIDE
    chunks = lambda t: (t.reshape(bsz, nch, CMP_STRIDE, g, d).transpose(0, 3, 1, 2, 4)
                        .reshape(bsz, g, nch, CMP_STRIDE * d))
    pe = cmp_pe.reshape(1, CMP_BLOCK * d)
    half = CMP_STRIDE * d
    pad2 = lambda w: jnp.pad(w, ((0, 0), (0, LANES - d))).astype(BF16)
    cc, cs1, cs2 = _rope_tables(jnp.arange(nch, dtype=jnp.int32) * CMP_STRIDE + (CMP_BLOCK - 1))
    k_cmp, v_cmp = _compress(chunks(kc), chunks(vc), pe[:, :half], pe[:, half:],
                             w_ck1.astype(BF16), pad2(w_ck2), w_cv1.astype(BF16), pad2(w_cv2), cc, cs1, cs2)

    ns = seq // SLC_BLOCK
    sh = lambda t: t.reshape(bsz, seq, t.shape[-1])
    o_cmp, selb = _cmp_select(q, k_cmp, v_cmp, _importance_weights(ns, nch))

    def heads(t):
        return t.astype(BF16).reshape(bsz, seq, g, d).transpose(0, 2, 1, 3)

    def with_ones(t):
        return jnp.concatenate([t, jnp.ones_like(t)], axis=-1)

    nt = seq // ATT_TK
    kst = ks.astype(BF16).reshape(bsz, nt, ATT_TK, g, d).transpose(0, 3, 1, 4, 2)
    vst = with_ones(heads(vs)).reshape(bsz, g, nt, ATT_TK, 2 * d)
    o = _nsa_attn(q, kst, vst, heads(kw), with_ones(heads(vw)),
                  selb, _block_expansion(nt, ns), o_cmp, sh(gates))
    o = o.transpose(0, 2, 1, 3)
    return _outproj_ln(o.reshape(bsz * seq, h * d), x2, w_out.astype(BF16),
                       ln_g.reshape(1, D_MODEL), ln_b.reshape(1, D_MODEL))


def _moe_layer(x2, w_router_pad, b_router_pad, w_gate, w_up, w_down, ln_g, ln_b):
    gate = _router(x2, w_router_pad, b_router_pad)
    wgu = jnp.concatenate([w_gate, w_up], axis=-1).astype(BF16)
    return _moe_ffn_ln(x2, gate, wgu, w_down.astype(BF16), ln_g.reshape(1, D_MODEL), ln_b.reshape(1, D_MODEL))


def kernel(x, gla_w_in, gla_w_gate_up, gla_b_gate, gla_norm_g, gla_w_out, nsa_w_in, nsa_w_cmp_k1, nsa_w_cmp_k2,
           nsa_w_cmp_v1, nsa_w_cmp_v2, nsa_cmp_pe, nsa_w_out, moe_w_router, moe_b_router, moe_w_gate, moe_w_up,
           moe_w_down, ln_g, ln_b):
    bsz, seq, _ = x.shape
    x2 = x.reshape(bsz * seq, D_MODEL)
    w_router_pad = jnp.pad(moe_w_router, ((0, 0), (0, LANES - N_EXPERTS)))
    b_router_pad = jnp.pad(moe_b_router, (0, LANES - N_EXPERTS)).reshape(1, LANES)
    for i in range(DEPTH):
        j = i // 2
        if i % 2 == 0:
            x2 = _gla_layer(x2, bsz, seq, gla_w_in[j], gla_w_gate_up[j], gla_b_gate[j], gla_norm_g[j],
                            gla_w_out[j], ln_g[i, 0], ln_b[i, 0])
        else:
            x2 = _nsa_layer(x2, bsz, seq, nsa_w_in[j], nsa_w_cmp_k1[j], nsa_w_cmp_k2[j], nsa_w_cmp_v1[j],
                            nsa_w_cmp_v2[j], nsa_cmp_pe[j], nsa_w_out[j], ln_g[i, 0], ln_b[i, 0])
        x2 = _moe_layer(x2, w_router_pad, b_router_pad, moe_w_gate[i], moe_w_up[i], moe_w_down[i],
                        ln_g[i, 1], ln_b[i, 1])
    return x2.reshape(bsz, seq, D_MODEL)
```

```python
import functools

import numpy as np
import jax
import jax.numpy as jnp
from jax import lax
from jax.experimental import pallas as pl
from jax.experimental.pallas import tpu as pltpu

F32 = jnp.float32
BF16 = jnp.bfloat16
HIGHEST = lax.Precision.HIGHEST

D_MODEL = 1024
DEPTH = 2

GLA_HEADS = 4
GLA_DK = D_MODEL // 2 // GLA_HEADS
GLA_DV = D_MODEL // GLA_HEADS
GLA_GATE_RANK = 16
GLA_TAU = 16.0
GLA_CHUNK = 64

NSA_HEADS = 16
NSA_GROUPS = 4
NSA_HPG = NSA_HEADS // NSA_GROUPS
NSA_HEAD_DIM = D_MODEL // NSA_HEADS
CMP_BLOCK = 32
CMP_STRIDE = 16
CMP_HIDDEN = 256
SLC_BLOCK = 64
SLC_SHIFT = SLC_BLOCK.bit_length() - 1
SLC_TOP_N = 16
WINDOW = 512

ROPE_THETA = 500000.0
ROT_DIM = NSA_HEAD_DIM // 4
ROT_HALF = ROT_DIM // 2

N_EXPERTS = 16
N_EXPERT_GROUPS = 4
EXPERTS_PER_GROUP = N_EXPERTS // N_EXPERT_GROUPS
D_FF_EXPERT = D_MODEL // 4

DN_ALPHA = (2 * DEPTH) ** 0.25
LN_EPS = 1e-5
NEG = -1e30
BIG = 1e30
LOG2E = 1.4426950408889634

LANES = 128
VMEM_LIMIT = 48 * 1024 * 1024

PROJ_TM = 256
GLA_T = 512
MOE_TM = 512
MOE_VMEM_LIMIT = 56 * 1024 * 1024
CMP_TQ = 256
CMP_CW = 128
ATT_TQ = 128
ATT_TK = 512
WIN_TK = 128


def _cparams(sem):
    return pltpu.CompilerParams(dimension_semantics=sem, vmem_limit_bytes=VMEM_LIMIT)


def _bdot(a, b):
    return jnp.dot(a, b, preferred_element_type=F32)


def _dot_nt(a, b, precision=None):
    return lax.dot_general(a, b, (((1,), (1,)), ((), ())), preferred_element_type=F32, precision=precision)


def _dot_tn(a, b):
    return lax.dot_general(a, b, (((0,), (0,)), ((), ())), preferred_element_type=F32)


def _layer_norm(z, g, b):
    mu = jnp.mean(z, axis=-1, keepdims=True)
    zc = z - mu
    var = jnp.mean(zc * zc, axis=-1, keepdims=True)
    return zc * lax.rsqrt(var + LN_EPS) * g + b


def _silu(t):
    return t * (0.5 * jnp.tanh(0.5 * t) + 0.5)


def _rope128(t, c, s1, s2):
    up = pltpu.roll(t, LANES - ROT_HALF, axis=1)
    dn = pltpu.roll(t, ROT_HALF, axis=1)
    return t * c + up * s1 + dn * s2


def _gla_proj_kernel(x_ref, wm_ref, wg_ref, wgu_ref, bg_ref, q_ref, k_ref, v_ref, r_ref, la_ref):
    xb = x_ref[...].astype(BF16)
    hk = GLA_HEADS * GLA_DK
    hv = GLA_HEADS * GLA_DV
    q_ref[...] = _bdot(xb, wm_ref[:, 0:hk])
    k_ref[...] = _bdot(xb, wm_ref[:, hk:2 * hk])
    v_ref[...] = _bdot(xb, wm_ref[:, 2 * hk:2 * hk + hv])
    r_ref[...] = _bdot(xb, wm_ref[:, 2 * hk + hv:2 * hk + 2 * hv])
    g_low = _bdot(xb, wg_ref[...])
    z = _bdot(g_low.astype(BF16), wgu_ref[...]) + bg_ref[...]
    log_sig = jnp.minimum(z, 0.0) - jnp.log1p(jnp.exp(-jnp.abs(z)))
    la_ref[...] = log_sig * (1.0 / GLA_TAU)


def _gla_proj(x2, w_main, w_glow, w_gu, b_gate):
    n = x2.shape[0]
    hk = GLA_HEADS * GLA_DK
    hv = GLA_HEADS * GLA_DV
    tm = PROJ_TM
    row = lambda i: (i, 0)
    full = lambda i: (0, 0)
    return pl.pallas_call(
        _gla_proj_kernel,
        grid=(n // tm,),
        in_specs=[pl.BlockSpec((tm, D_MODEL), row),
                  pl.BlockSpec(w_main.shape, full),
                  pl.BlockSpec(w_glow.shape, full),
                  pl.BlockSpec(w_gu.shape, full),
                  pl.BlockSpec(b_gate.shape, full)],
        out_specs=[pl.BlockSpec((tm, hk), row), pl.BlockSpec((tm, hk), row),
                   pl.BlockSpec((tm, hv), row), pl.BlockSpec((tm, hv), row),
                   pl.BlockSpec((tm, hk), row)],
        out_shape=[jax.ShapeDtypeStruct((n, hk), F32), jax.ShapeDtypeStruct((n, hk), F32),
                   jax.ShapeDtypeStruct((n, hv), F32), jax.ShapeDtypeStruct((n, hv), F32),
                   jax.ShapeDtypeStruct((n, hk), F32)],
        compiler_params=_cparams(("arbitrary",)),
        name="gla_proj",
    )(x2, w_main, w_glow, w_gu, b_gate)


def _dot_01(m, x):
    x1 = x.astype(BF16)
    r1 = x - x1.astype(F32)
    x2 = r1.astype(BF16)
    x3 = (r1 - x2.astype(F32)).astype(BF16)
    return _bdot(m, x1) + _bdot(m, x2) + _bdot(m, x3)


def _gla_core_kernel(q_ref, k_ref, v_ref, la_ref, r_ref, g_ref, o_ref, st_ref):
    @pl.when(pl.program_id(2) == 0)
    def _():
        st_ref[...] = jnp.zeros_like(st_ref)

    c, t = GLA_CHUNK, GLA_T
    shift = c.bit_length() - 1
    row = lax.broadcasted_iota(jnp.int32, (t, t), 0)
    col = lax.broadcasted_iota(jnp.int32, (t, t), 1)
    causal = ((row >> shift) == (col >> shift)) & (row >= col)
    in_chunk = lax.broadcasted_iota(jnp.int32, (t, GLA_DK), 0) & (c - 1)
    b = la_ref[0]
    step = 1
    while step < c:
        b = b + jnp.where(in_chunk >= step, pltpu.roll(b, step, axis=0), 0.0)
        step *= 2
    b_last = jnp.broadcast_to(b.reshape(t // c, c, GLA_DK)[:, c - 1:c, :], (t // c, c, GLA_DK)).reshape(t, GLA_DK)
    q = q_ref[0] * (GLA_DK ** -0.5)
    k = k_ref[0]
    vb = v_ref[0].astype(BF16)
    q_dec = (q * jnp.exp(b)).astype(BF16)
    k_neg = (k * jnp.exp(-b)).astype(BF16)
    k_dec = (k * jnp.exp(b_last - b)).astype(BF16)
    decay = jnp.exp(b_last)
    s = jnp.where(causal, _dot_nt(q_dec, k_neg), 0.0)
    o = _bdot(s.astype(BF16), vb)

    chunks = [slice(ci * c, (ci + 1) * c) for ci in range(t // c)]
    kv = [_dot_tn(vb[rows], k_dec[rows]) for rows in chunks]
    st = st_ref[...]
    entering = []
    for ci, rows in enumerate(chunks):
        entering.append(st.astype(BF16))
        st = decay[ci * c:ci * c + 1] * st + kv[ci]
    st_ref[...] = st
    o = o + jnp.concatenate([_dot_nt(q_dec[rows], s_in) for rows, s_in in zip(chunks, entering)], axis=0)
    ms = jnp.mean(o * o, axis=-1, keepdims=True)
    o = o * lax.rsqrt(ms + LN_EPS) * g_ref[...]
    o_ref[0] = o * _silu(r_ref[0])


def _gla_core(q, k, v, la, r, norm_g):
    bsz, s, _ = q.shape
    t = GLA_T
    kmap = lambda b, h, n: (b, n, h)
    return pl.pallas_call(
        _gla_core_kernel,
        grid=(bsz, GLA_HEADS, s // t),
        in_specs=[pl.BlockSpec((1, t, GLA_DK), kmap), pl.BlockSpec((1, t, GLA_DK), kmap),
                  pl.BlockSpec((1, t, GLA_DV), kmap), pl.BlockSpec((1, t, GLA_DK), kmap),
                  pl.BlockSpec((1, t, GLA_DV), kmap),
                  pl.BlockSpec((1, GLA_DV), lambda b, h, n: (0, 0))],
        out_specs=pl.BlockSpec((1, t, GLA_DV), kmap),
        out_shape=jax.ShapeDtypeStruct((bsz, s, GLA_HEADS * GLA_DV), F32),
        scratch_shapes=[pltpu.VMEM((GLA_DV, GLA_DK), F32)],
        compiler_params=_cparams(("arbitrary", "arbitrary", "arbitrary")),
        name="gla_core",
    )(q, k, v, la, r, norm_g)


def _outproj_ln_kernel(h_ref, x_ref, w_ref, g_ref, b_ref, o_ref):
    y = _bdot(h_ref[...].astype(BF16), w_ref[...])
    o_ref[...] = _layer_norm(DN_ALPHA * x_ref[...] + y, g_ref[...], b_ref[...])


def _outproj_heads_ln_kernel(h_ref, x_ref, w_ref, g_ref, b_ref, o_ref):
    h = jnp.concatenate([h_ref[0, i].astype(BF16) for i in range(h_ref.shape[1])], axis=1)
    o_ref[...] = _layer_norm(DN_ALPHA * x_ref[...] + _bdot(h, w_ref[...]), g_ref[...], b_ref[...])


def _outproj_heads_ln(h4, x2, w, g, b):
    n = x2.shape[0]
    _, heads, seq, d = h4.shape
    tm = PROJ_TM
    per = seq // tm
    row = lambda i: (i, 0)
    full = lambda i: (0, 0)
    return pl.pallas_call(
        _outproj_heads_ln_kernel,
        grid=(n // tm,),
        in_specs=[pl.BlockSpec((1, heads, tm, d), lambda i: (i // per, 0, i % per, 0)),
                  pl.BlockSpec((tm, D_MODEL), row),
                  pl.BlockSpec(w.shape, full), pl.BlockSpec((1, D_MODEL), full), pl.BlockSpec((1, D_MODEL), full)],
        out_specs=pl.BlockSpec((tm, D_MODEL), row),
        out_shape=jax.ShapeDtypeStruct((n, D_MODEL), F32),
        compiler_params=_cparams(("arbitrary",)),
        name="outproj_heads_ln",
    )(h4, x2, w, g, b)


def _outproj_ln(h2, x2, w, g, b):
    n = x2.shape[0]
    tm = PROJ_TM
    row = lambda i: (i, 0)
    full = lambda i: (0, 0)
    return pl.pallas_call(
        _outproj_ln_kernel,
        grid=(n // tm,),
        in_specs=[pl.BlockSpec((tm, h2.shape[1]), row), pl.BlockSpec((tm, D_MODEL), row),
                  pl.BlockSpec(w.shape, full), pl.BlockSpec((1, D_MODEL), full), pl.BlockSpec((1, D_MODEL), full)],
        out_specs=pl.BlockSpec((tm, D_MODEL), row),
        out_shape=jax.ShapeDtypeStruct((n, D_MODEL), F32),
        compiler_params=_cparams(("arbitrary",)),
        name="outproj_ln",
    )(h2, x2, w, g, b)


def _lane_xor(v, k, lane):
    up = pltpu.roll(v, LANES - k, axis=1)
    dn = pltpu.roll(v, k, axis=1)
    return jnp.where((lane & k) == 0, up, dn)


def _route(x, w_hi, w_lo, bias):
    x_hi = x.astype(BF16)
    x_lo = (x - x_hi.astype(F32)).astype(BF16)
    logits = _bdot(x_hi, w_hi) + (_bdot(x_hi, w_lo) + _bdot(x_lo, w_hi))
    s = jax.nn.sigmoid(logits)
    a = s + bias
    lane = lax.broadcasted_iota(jnp.int32, a.shape, 1)
    a1 = _lane_xor(a, 1, lane)
    a2 = _lane_xor(a, 2, lane)
    a3 = _lane_xor(a1, 2, lane)
    p, q = jnp.maximum(a, a1), jnp.minimum(a, a1)
    r, t = jnp.maximum(a2, a3), jnp.minimum(a2, a3)
    gs = jnp.maximum(p, r) + jnp.maximum(jnp.minimum(p, r), jnp.maximum(q, t))
    g1 = _lane_xor(gs, 4, lane)
    g2 = _lane_xor(gs, 8, lane)
    g3 = _lane_xor(g1, 8, lane)

    def beats(other, other_first):
        return (other > gs) | ((other == gs) & other_first)

    lose = (beats(g1, (lane & 4) != 0) | beats(g2, (lane & 8) != 0) | beats(g3, (lane & 8) != 0))

    def ahead(other, other_first):
        return jnp.where((other > a) | ((other == a) & other_first), 1.0, 0.0)

    rank = ahead(a1, (lane & 1) != 0) + ahead(a2, (lane & 2) != 0) + ahead(a3, (lane & 2) != 0)
    sel = jnp.logical_not(lose) & (rank < 2.0) & (lane < N_EXPERTS)
    ssel = jnp.where(sel, s, 0.0)
    s1 = _lane_xor(ssel, 1, lane)
    tot = (ssel + s1) + (_lane_xor(ssel, 2, lane) + _lane_xor(s1, 2, lane))
    return jnp.where(sel, s / tot, 0.0)


def _moe_kernel(x_ref, wrh_ref, wrl_ref, br_ref, wgu_ref, wd_ref, lg_ref, lb_ref, o_ref, h_ref):
    x = x_ref[...]
    gate = _route(x, wrh_ref[...], wrl_ref[...], br_ref[...])
    xb = x.astype(BF16)
    f = D_FF_EXPERT
    for e in range(N_EXPERTS):
        hgu = _bdot(xb, wgu_ref[e])
        h_ref[:, e * f:(e + 1) * f] = (_silu(hgu[:, :f]) * hgu[:, f:] * gate[:, e:e + 1]).astype(BF16)
    y = _bdot(h_ref[...], wd_ref[...])
    o_ref[...] = _layer_norm(DN_ALPHA * x + y, lg_ref[...], lb_ref[...])


def _moe_ffn_ln(x2, wr_hi, wr_lo, br, wgu, wd, lg, lb):
    n = x2.shape[0]
    tm = min(MOE_TM, n)
    row = lambda i: (i, 0)
    full = lambda i: (0, 0)
    resident = pl.Buffered(1)
    return pl.pallas_call(
        _moe_kernel,
        grid=(n // tm,),
        in_specs=[pl.BlockSpec((tm, D_MODEL), row),
                  pl.BlockSpec(wr_hi.shape, full), pl.BlockSpec(wr_lo.shape, full), pl.BlockSpec(br.shape, full),
                  pl.BlockSpec(wgu.shape, lambda i: (0, 0, 0), pipeline_mode=resident),
                  pl.BlockSpec(wd.shape, lambda i: (0, 0), pipeline_mode=resident),
                  pl.BlockSpec((1, D_MODEL), lambda i: (0, 0)),
                  pl.BlockSpec((1, D_MODEL), lambda i: (0, 0))],
        out_specs=pl.BlockSpec((tm, D_MODEL), row),
        out_shape=jax.ShapeDtypeStruct((n, D_MODEL), F32),
        scratch_shapes=[pltpu.VMEM((tm, N_EXPERTS * D_FF_EXPERT), BF16)],
        compiler_params=pltpu.CompilerParams(dimension_semantics=("arbitrary",), vmem_limit_bytes=MOE_VMEM_LIMIT),
        name="moe_ffn",
    )(x2, wr_hi, wr_lo, br, wgu, wd, lg, lb)


def _nsa_proj_kernel(x_ref, wr_ref, wp_ref, wgt_ref, c_ref, s1_ref, s2_ref,
                     q_ref, ks_ref, kw_ref, kc_ref, vc_ref, vs_ref, vw_ref, gt_ref):
    xb = x_ref[...].astype(BF16)
    c, s1, s2 = c_ref[...], s1_ref[...], s2_ref[...]
    nq = NSA_HEADS * NSA_HEAD_DIM // LANES
    ng = NSA_GROUPS * NSA_HEAD_DIM // LANES
    for j in range(nq + 2 * ng):
        t = _rope128(_bdot(xb, wr_ref[:, j * LANES:(j + 1) * LANES]), c, s1, s2)
        if j < nq:
            t = (t * (NSA_HEAD_DIM ** -0.5 * LOG2E)).astype(BF16)
            q_ref[0, 2 * j] = t[:, :NSA_HEAD_DIM]
            q_ref[0, 2 * j + 1] = t[:, NSA_HEAD_DIM:]
        elif j < nq + ng:
            ks_ref[:, (j - nq) * LANES:(j - nq + 1) * LANES] = t
        else:
            kw_ref[:, (j - nq - ng) * LANES:(j - nq - ng + 1) * LANES] = t
    gd = NSA_GROUPS * NSA_HEAD_DIM
    for j, ref in enumerate((kc_ref, vc_ref, vs_ref, vw_ref)):
        ref[...] = _bdot(xb, wp_ref[:, j * gd:(j + 1) * gd])
    gt_ref[...] = jax.nn.sigmoid(_bdot(xb, wgt_ref[...]))


def _nsa_proj(x2, w_rope, w_plain, w_gates, rc, rs1, rs2, seq):
    n = x2.shape[0]
    tm = PROJ_TM
    gd = NSA_GROUPS * NSA_HEAD_DIM
    row = lambda i: (i, 0)
    full = lambda i: (0, 0)
    per = seq // tm
    pos = lambda i: (i % per, 0)
    out_w = [gd, gd, gd, gd, gd, gd, NSA_GROUPS * LANES]
    q_spec = pl.BlockSpec((1, NSA_HEADS, tm, NSA_HEAD_DIM), lambda i: (i // per, 0, i % per, 0))
    q_shape = jax.ShapeDtypeStruct((n // seq, NSA_HEADS, seq, NSA_HEAD_DIM), BF16)
    return pl.pallas_call(
        _nsa_proj_kernel,
        grid=(n // tm,),
        in_specs=[pl.BlockSpec((tm, D_MODEL), row),
                  pl.BlockSpec(w_rope.shape, full), pl.BlockSpec(w_plain.shape, full),
                  pl.BlockSpec(w_gates.shape, full),
                  pl.BlockSpec((tm, LANES), pos), pl.BlockSpec((tm, LANES), pos), pl.BlockSpec((tm, LANES), pos)],
        out_specs=[q_spec] + [pl.BlockSpec((tm, w), row) for w in out_w],
        out_shape=[q_shape] + [jax.ShapeDtypeStruct((n, w), F32) for w in out_w],
        compiler_params=_cparams(("arbitrary",)),
        name="nsa_proj",
    )(x2, w_rope, w_plain, w_gates, rc, rs1, rs2)


def _compress_kernel(ck_ref, cv_ref, pet_ref, peb_ref, wk1_ref, wk2_ref, wv1_ref, wv2_ref,
                     c_ref, s1_ref, s2_ref, ko_ref, vo_ref):
    nch = ck_ref.shape[2]
    half = CMP_STRIDE * NSA_HEAD_DIM
    pet, peb = pet_ref[...], peb_ref[...]

    def mlp(ch, w1_ref, w2_ref):
        a = _bdot((ch + pet).astype(BF16), w1_ref[0:half, :])
        bm = _bdot((ch + peb).astype(BF16), w1_ref[half:2 * half, :])
        h = _silu(a + pltpu.roll(bm, nch - 1, axis=0))
        return _bdot(h.astype(BF16), w2_ref[...])

    kc = _rope128(mlp(ck_ref[0, 0], wk1_ref, wk2_ref), c_ref[...], s1_ref[...], s2_ref[...])
    ko_ref[0, 0] = kc[:, :NSA_HEAD_DIM].astype(BF16)
    vo_ref[0, 0] = mlp(cv_ref[0, 0], wv1_ref, wv2_ref)[:, :NSA_HEAD_DIM].astype(BF16)


def _compress(ck, cv, pet, peb, wk1, wk2, wv1, wv2, cc, cs1, cs2):
    bsz, g, nch, width = ck.shape
    blk = lambda b, gi: (b, gi, 0, 0)
    full = lambda b, gi: (0, 0)
    return pl.pallas_call(
        _compress_kernel,
        grid=(bsz, g),
        in_specs=[pl.BlockSpec((1, 1, nch, width), blk), pl.BlockSpec((1, 1, nch, width), blk),
                  pl.BlockSpec(pet.shape, full), pl.BlockSpec(peb.shape, full),
                  pl.BlockSpec(wk1.shape, full), pl.BlockSpec(wk2.shape, full),
                  pl.BlockSpec(wv1.shape, full), pl.BlockSpec(wv2.shape, full),
                  pl.BlockSpec(cc.shape, full), pl.BlockSpec(cs1.shape, full), pl.BlockSpec(cs2.shape, full)],
        out_specs=[pl.BlockSpec((1, 1, nch, NSA_HEAD_DIM), blk), pl.BlockSpec((1, 1, nch, NSA_HEAD_DIM), blk)],
        out_shape=[jax.ShapeDtypeStruct((bsz, g, nch, NSA_HEAD_DIM), BF16)] * 2,
        compiler_params=_cparams(("arbitrary", "arbitrary")),
        name="nsa_compress",
    )(ck, cv, pet, peb, wk1, wk2, wv1, wv2, cc, cs1, cs2)


def _cmp_select_kernel(q_ref, kc_ref, vc_ref, wimp_ref, o_ref, sel_ref):
    tq = q_ref.shape[2]
    nch = kc_ref.shape[2]
    ns = wimp_ref.shape[0]
    s0 = pl.program_id(2) * tq
    ratio = SLC_BLOCK // CMP_STRIDE
    n_vis = (s0 + tq - CMP_BLOCK) // CMP_STRIDE + 1
    n_chunks = (n_vis + CMP_CW - 1) // CMP_CW

    def body(w):
        nb = w // ratio
        qpos_c = s0 + lax.broadcasted_iota(jnp.int32, (tq, 1), 0)
        cend = lax.broadcasted_iota(jnp.int32, (1, w), 1) * CMP_STRIDE + (CMP_BLOCK - 1)
        cmask = cend <= qpos_c
        anyvis = (qpos_c >= CMP_BLOCK - 1).astype(F32)
        kc = kc_ref[0, 0, :w, :]
        vc = vc_ref[0, 0, :w, :]
        pg = jnp.zeros((tq, w), F32)
        for h in range(NSA_HPG):
            s = jnp.where(cmask, _dot_nt(q_ref[0, h], kc), NEG)
            e = jnp.exp2(s - jnp.max(s, axis=-1, keepdims=True))
            p = e * (anyvis / jnp.sum(e, axis=-1, keepdims=True))
            o_ref[0, h] = _bdot(p.astype(BF16), vc)
            pg = pg + p

        pg_hi = pg.astype(BF16)
        pg_lo = (pg - pg_hi.astype(F32)).astype(BF16)
        wimp = wimp_ref[:nb, :w]
        imp = _dot_nt(wimp, pg_hi) + _dot_nt(wimp, pg_lo)
        blk = lax.broadcasted_iota(jnp.int32, (nb, tq), 0)
        cur = (s0 + lax.broadcasted_iota(jnp.int32, (nb, tq), 1)) >> SLC_SHIFT
        forced = (blk == 0) | (blk == cur) | (blk == cur - 1)
        imp = jnp.where(blk <= cur, jnp.where(forced, BIG, imp), NEG)
        blk_f = blk.astype(F32)
        sel = jnp.zeros((nb, tq), F32)
        for _ in range(min(SLC_TOP_N, nb)):
            m = jnp.max(imp, axis=0, keepdims=True)
            first = jnp.min(jnp.where(imp == m, blk_f, float(nb)), axis=0, keepdims=True)
            hit = blk_f == first
            sel = jnp.where(hit, 1.0, sel)
            imp = jnp.where(hit, -jnp.inf, imp)
        if nb < ns:
            sel = jnp.concatenate([sel, jnp.zeros((ns - nb, tq), F32)], axis=0)
        sel_ref[0, 0] = jnp.where(sel.T > 0.0, 0.0, NEG).astype(BF16)

    for nv in range(1, nch // CMP_CW + 1):
        pl.when(n_chunks == nv)(functools.partial(body, nv * CMP_CW))


def _cmp_select(q, k_cmp, v_cmp, wimp_t):
    bsz, _, s, d = q.shape
    g = NSA_GROUPS
    nch = k_cmp.shape[2]
    ns = wimp_t.shape[0]
    tq = min(CMP_TQ, s)
    hmap = lambda b, gi, qi: (b, gi, qi, 0)
    return pl.pallas_call(
        _cmp_select_kernel,
        grid=(bsz, g, s // tq),
        in_specs=[pl.BlockSpec((1, NSA_HPG, tq, d), hmap),
                  pl.BlockSpec((1, 1, nch, d), lambda b, gi, qi: (b, gi, 0, 0)),
                  pl.BlockSpec((1, 1, nch, d), lambda b, gi, qi: (b, gi, 0, 0)),
                  pl.BlockSpec(wimp_t.shape, lambda b, gi, qi: (0, 0))],
        out_specs=[pl.BlockSpec((1, NSA_HPG, tq, d), hmap),
                   pl.BlockSpec((1, 1, tq, ns), hmap)],
        out_shape=[jax.ShapeDtypeStruct((bsz, NSA_HEADS, s, d), F32),
                   jax.ShapeDtypeStruct((bsz, g, s, ns), BF16)],
        compiler_params=_cparams(("arbitrary", "arbitrary", "arbitrary")),
        name="nsa_cmp_select",
    )(q, k_cmp, v_cmp, wimp_t)


def _nsa_attn_kernel(q_ref, kst_ref, vs_ref, kw_ref, vw_ref, selb_ref, eexp_ref, ocmp_ref, gt_ref,
                     o_ref, m_ref, acc_ref, s_ref, p_ref, a_ref):
    hpg, tq, d = q_ref.shape[1:]
    rows = hpg * tq
    s0 = pl.program_id(2) * tq
    q4 = q_ref[0].reshape(rows, d)
    q4a = jnp.concatenate([jnp.concatenate([selb_ref[0, 0]] * hpg, axis=0), q4], axis=1)
    qpos = s0 + lax.broadcasted_iota(jnp.int32, (tq, 1), 0)

    def scores(t):
        return _bdot(q4a, jnp.concatenate([eexp_ref[t], kst_ref[0, 0, t]], axis=0))

    def flush(t):
        acc_ref[...] = a_ref[...] * acc_ref[...] + _bdot(p_ref[...], vs_ref[0, 0, t])

    def softmax_step(s):
        m_prev = m_ref[...]
        m_new = jnp.maximum(m_prev, jnp.max(s, axis=-1, keepdims=True))
        a_ref[...] = jnp.exp2(m_prev - m_new)
        p_ref[...] = jnp.exp2(s - m_new).astype(BF16)
        m_ref[...] = m_new

    m_ref[...] = jnp.full_like(m_ref, NEG)
    acc_ref[...] = jnp.zeros_like(acc_ref)
    p_ref[...] = jnp.zeros_like(p_ref)
    a_ref[...] = jnp.ones_like(a_ref)
    t_diag = s0 // ATT_TK
    s_ref[...] = scores(0)

    def sel_body(t, carry):
        s = s_ref[...]
        s_next = scores(t + 1)
        flush(jnp.maximum(t - 1, 0))
        softmax_step(s)
        s_ref[...] = s_next
        return carry

    lax.fori_loop(0, t_diag, sel_body, 0)

    span = WINDOW + tq
    w0 = pl.multiple_of(jnp.maximum(s0 - WINDOW, 0), tq)
    rel = qpos - (w0 + lax.broadcasted_iota(jnp.int32, (1, span), 1))
    wbias = jnp.where((rel >= 0) & (rel < WINDOW), 0.0, NEG)
    sw = _dot_nt(q4, kw_ref[0, 0, pl.ds(w0, span), :])
    sw = (sw.reshape(hpg, tq, span) + wbias[None]).reshape(rows, span)
    pw = jnp.exp2(sw - jnp.max(sw, axis=-1, keepdims=True)).astype(BF16)
    accw = _bdot(pw, vw_ref[0, 0, pl.ds(w0, span), :])
    o_win = (accw[:, :d] / accw[:, d:d + 1]).reshape(hpg, tq, d)

    flush(jnp.maximum(t_diag - 1, 0))
    kpos = t_diag * ATT_TK + lax.broadcasted_iota(jnp.int32, (1, ATT_TK), 1)
    s = s_ref[...].reshape(hpg, tq, ATT_TK)
    softmax_step(jnp.where((kpos <= qpos)[None], s, NEG).reshape(rows, ATT_TK))
    flush(t_diag)
    acc = acc_ref[...]
    o_slc = (acc[:, :d] / acc[:, d:d + 1]).reshape(hpg, tq, d)

    gts = gt_ref[0]
    for h in range(hpg):
        g0, g1, g2 = (gts[:, 3 * h + j:3 * h + j + 1] for j in range(3))
        o_ref[0, h] = g0 * ocmp_ref[0, h] + g1 * o_slc[h] + g2 * o_win[h]


def _nsa_attn(q, kst, vst, kw, vw, selb, eexp, o_cmp, gates):
    bsz, _, s, d = q.shape
    g = NSA_GROUPS
    tq = ATT_TQ
    rows = NSA_HPG * tq
    ns = selb.shape[-1]
    hmap = lambda b, gi, qi: (b, gi, qi, 0)
    tmap = lambda b, gi, qi: (b, gi, 0, 0, 0)
    rmap = lambda b, gi, qi: (b, gi, 0, 0)
    return pl.pallas_call(
        _nsa_attn_kernel,
        grid=(bsz, g, s // tq),
        in_specs=[pl.BlockSpec((1, NSA_HPG, tq, d), hmap),
                  pl.BlockSpec((1, 1) + kst.shape[2:], tmap), pl.BlockSpec((1, 1) + vst.shape[2:], tmap),
                  pl.BlockSpec((1, 1) + kw.shape[2:], rmap), pl.BlockSpec((1, 1) + vw.shape[2:], rmap),
                  pl.BlockSpec((1, 1, tq, ns), hmap),
                  pl.BlockSpec(eexp.shape, lambda b, gi, qi: (0, 0, 0)),
                  pl.BlockSpec((1, NSA_HPG, tq, d), hmap),
                  pl.BlockSpec((1, tq, LANES), lambda b, gi, qi: (b, qi, gi))],
        out_specs=pl.BlockSpec((1, NSA_HPG, tq, d), hmap),
        out_shape=jax.ShapeDtypeStruct((bsz, NSA_HEADS, s, d), F32),
        scratch_shapes=[pltpu.VMEM((rows, 1), F32),
                        pltpu.VMEM((rows, 2 * d), F32),
                        pltpu.VMEM((rows, ATT_TK), F32),
                        pltpu.VMEM((rows, ATT_TK), BF16),
                        pltpu.VMEM((rows, 1), F32)],
        compiler_params=_cparams(("arbitrary", "arbitrary", "arbitrary")),
        name="nsa_attn",
    )(q, kst, vst, kw, vw, selb, eexp, o_cmp, gates)


def _rope_tables(pos):
    inv_freq = jnp.power(ROPE_THETA, -jnp.arange(ROT_HALF, dtype=F32) * (2.0 / ROT_DIM))
    ang = pos.astype(F32)[:, None] * inv_freq[None, :]
    cos, sin = jnp.cos(ang), jnp.sin(ang)
    n = pos.shape[0]
    one = jnp.ones((n, NSA_HEAD_DIM - ROT_DIM), F32)
    zero8 = jnp.zeros((n, ROT_HALF), F32)
    zero = jnp.zeros((n, NSA_HEAD_DIM - ROT_DIM), F32)
    c = jnp.concatenate([cos, cos, one], axis=1)
    s1 = jnp.concatenate([-sin, zero8, zero], axis=1)
    s2 = jnp.concatenate([zero8, sin, zero], axis=1)
    rep = LANES // NSA_HEAD_DIM
    return tuple(jnp.tile(t, (1, rep)) for t in (c, s1, s2))


def _importance_weights(ns, nch):
    ratio = SLC_BLOCK // CMP_STRIDE
    span = CMP_BLOCK // CMP_STRIDE
    w = np.zeros((ns, nch), np.float32)
    for j in range(ns):
        for m in range(ratio):
            for n in range(span):
                c = ratio * j + m + n
                if c < nch - 1:
                    w[j, c] += 1.0
    return jnp.asarray(w, dtype=BF16)


def _block_expansion(nt, ns):
    key_blk = (np.arange(nt)[:, None] * ATT_TK + np.arange(ATT_TK)[None, :]) // SLC_BLOCK
    e = (np.arange(ns)[None, :, None] == key_blk[:, None, :]).astype(np.float32)
    return jnp.asarray(e, dtype=BF16)


def _gla_layer(x2, bsz, seq, w_in, w_gate_up, b_gate, norm_g, w_out, ln_g, ln_b):
    hk = GLA_HEADS * GLA_DK
    hv = GLA_HEADS * GLA_DV
    cuts = np.cumsum([hk, hk, hv, GLA_GATE_RANK]).tolist()
    wq, wk, wv, wg, wr = jnp.split(w_in, cuts, axis=1)
    w_main = jnp.concatenate([wq, wk, wv, wr], axis=1).astype(BF16)
    w_glow = jnp.pad(wg, ((0, 0), (0, LANES - GLA_GATE_RANK))).astype(BF16)
    w_gu = jnp.pad(w_gate_up, ((0, LANES - GLA_GATE_RANK), (0, 0))).astype(BF16)
    q, k, v, r, la = _gla_proj(x2, w_main, w_glow, w_gu, b_gate.reshape(1, hk))
    sh = lambda t: t.reshape(bsz, seq, t.shape[-1])
    o = _gla_core(sh(q), sh(k), sh(v), sh(la), sh(r), norm_g.reshape(1, GLA_DV))
    return _outproj_ln(o.reshape(bsz * seq, hv), x2, w_out.astype(BF16),
                       ln_g.reshape(1, D_MODEL), ln_b.reshape(1, D_MODEL))


def _nsa_layer(x2, bsz, seq, w_in, w_ck1, w_ck2, w_cv1, w_cv2, cmp_pe, w_out, ln_g, ln_b):
    h, g, hpg, d = NSA_HEADS, NSA_GROUPS, NSA_HPG, NSA_HEAD_DIM
    gd = g * d
    cuts = np.cumsum([h * d] + [gd] * 6).tolist()
    wq, wkc, wvc, wks, wvs, wkw, wvw, wgt = jnp.split(w_in, cuts, axis=1)
    w_rope = jnp.concatenate([wq, wks, wkw], axis=1).astype(BF16)
    w_plain = jnp.concatenate([wkc, wvc, wvs, wvw], axis=1).astype(BF16)
    w_gates = jnp.pad(wgt.reshape(D_MODEL, g, hpg * 3), ((0, 0), (0, 0), (0, LANES - hpg * 3)))
    w_gates = w_gates.reshape(D_MODEL, g * LANES).astype(BF16)
    rc, rs1, rs2 = _rope_tables(jnp.arange(seq, dtype=jnp.int32))
    q, ks, kw, kc, vc, vs, vw, gates = _nsa_proj(x2, w_rope, w_plain, w_gates, rc, rs1, rs2, seq)

    nch = seq // CMP_STRIDE
    chunks = lambda t: (t.reshape(bsz, nch, CMP_STRIDE, g, d).transpose(0, 3, 1, 2, 4)
                        .reshape(bsz, g, nch, CMP_STRIDE * d))
    pe = cmp_pe.reshape(1, CMP_BLOCK * d)
    half = CMP_STRIDE * d
    pad2 = lambda w: jnp.pad(w, ((0, 0), (0, LANES - d))).astype(BF16)
    cc, cs1, cs2 = _rope_tables(jnp.arange(nch, dtype=jnp.int32) * CMP_STRIDE + (CMP_BLOCK - 1))
    k_cmp, v_cmp = _compress(chunks(kc), chunks(vc), pe[:, :half], pe[:, half:],
                             w_ck1.astype(BF16), pad2(w_ck2), w_cv1.astype(BF16), pad2(w_cv2), cc, cs1, cs2)

    ns = seq // SLC_BLOCK
    sh = lambda t: t.reshape(bsz, seq, t.shape[-1])
    o_cmp, selb = _cmp_select(q, k_cmp, v_cmp, _importance_weights(ns, nch))

    def heads(t):
        return t.astype(BF16).reshape(bsz, seq, g, d).transpose(0, 2, 1, 3)

    def with_ones(t):
        return jnp.concatenate([t, jnp.ones_like(t)], axis=-1)

    nt = seq // ATT_TK
    kst = ks.astype(BF16).reshape(bsz, nt, ATT_TK, g, d).transpose(0, 3, 1, 4, 2)
    vst = with_ones(heads(vs)).reshape(bsz, g, nt, ATT_TK, 2 * d)
    o = _nsa_attn(q, kst, vst, heads(kw), with_ones(heads(vw)),
                  selb, _block_expansion(nt, ns), o_cmp, sh(gates))
    return _outproj_heads_ln(o, x2, w_out.astype(BF16), ln_g.reshape(1, D_MODEL), ln_b.reshape(1, D_MODEL))


def _moe_layer(x2, wr_hi, wr_lo, b_router_pad, w_gate, w_up, w_down, ln_g, ln_b):
    wgu = jnp.concatenate([w_gate, w_up], axis=-1).astype(BF16)
    wd = w_down.astype(BF16).reshape(N_EXPERTS * D_FF_EXPERT, D_MODEL)
    return _moe_ffn_ln(x2, wr_hi, wr_lo, b_router_pad, wgu, wd, ln_g.reshape(1, D_MODEL), ln_b.reshape(1, D_MODEL))


def kernel(x, gla_w_in, gla_w_gate_up, gla_b_gate, gla_norm_g, gla_w_out, nsa_w_in, nsa_w_cmp_k1, nsa_w_cmp_k2,
           nsa_w_cmp_v1, nsa_w_cmp_v2, nsa_cmp_pe, nsa_w_out, moe_w_router, moe_b_router, moe_w_gate, moe_w_up,
           moe_w_down, ln_g, ln_b):
    bsz, seq, _ = x.shape
    x2 = x.reshape(bsz * seq, D_MODEL)
    w_router_pad = jnp.pad(moe_w_router, ((0, 0), (0, LANES - N_EXPERTS)))
    wr_hi = w_router_pad.astype(BF16)
    wr_lo = (w_router_pad - wr_hi.astype(F32)).astype(BF16)
    b_router_pad = jnp.pad(moe_b_router, (0, LANES - N_EXPERTS)).reshape(1, LANES)
    for i in range(DEPTH):
        j = i // 2
        if i % 2 == 0:
            x2 = _gla_layer(x2, bsz, seq, gla_w_in[j], gla_w_gate_up[j], gla_b_gate[j], gla_norm_g[j],
                            gla_w_out[j], ln_g[i, 0], ln_b[i, 0])
        else:
            x2 = _nsa_layer(x2, bsz, seq, nsa_w_in[j], nsa_w_cmp_k1[j], nsa_w_cmp_k2[j], nsa_w_cmp_v1[j],
                            nsa_w_cmp_v2[j], nsa_cmp_pe[j], nsa_w_out[j], ln_g[i, 0], ln_b[i, 0])
        x2 = _moe_layer(x2, wr_hi, wr_lo, b_router_pad, moe_w_gate[i], moe_w_up[i], moe_w_down[i],
                        ln_g[i, 1], ln_b[i, 1])
    return x2.reshape(bsz, seq, D_MODEL)
```

```python
import functools

import numpy as np
import jax
import jax.numpy as jnp
from jax import lax
from jax.experimental import pallas as pl
from jax.experimental.pallas import tpu as pltpu

F32 = jnp.float32
BF16 = jnp.bfloat16
HIGHEST = lax.Precision.HIGHEST

D_MODEL = 1024
DEPTH = 2

GLA_HEADS = 4
GLA_DK = D_MODEL // 2 // GLA_HEADS
GLA_DV = D_MODEL // GLA_HEADS
GLA_GATE_RANK = 16
GLA_TAU = 16.0
GLA_CHUNK = 64

NSA_HEADS = 16
NSA_GROUPS = 4
NSA_HPG = NSA_HEADS // NSA_GROUPS
NSA_HEAD_DIM = D_MODEL // NSA_HEADS
CMP_BLOCK = 32
CMP_STRIDE = 16
CMP_HIDDEN = 256
SLC_BLOCK = 64
SLC_SHIFT = SLC_BLOCK.bit_length() - 1
SLC_TOP_N = 16
WINDOW = 512

ROPE_THETA = 500000.0
ROT_DIM = NSA_HEAD_DIM // 4
ROT_HALF = ROT_DIM // 2

N_EXPERTS = 16
N_EXPERT_GROUPS = 4
EXPERTS_PER_GROUP = N_EXPERTS // N_EXPERT_GROUPS
D_FF_EXPERT = D_MODEL // 4

DN_ALPHA = (2 * DEPTH) ** 0.25
LN_EPS = 1e-5
NEG = -1e30
BIG = 1e30
LOG2E = 1.4426950408889634

LANES = 128
VMEM_LIMIT = 48 * 1024 * 1024

PROJ_TM = 256
OUT_TM = 512
GLA_T = 512
MOE_TM = 512
MOE_VMEM_LIMIT = 56 * 1024 * 1024
CMP_TQ = 256
CMP_CW = 128
ATT_TQ = 128
ATT_TK = 512
WIN_TK = 128


def _cparams(sem):
    return pltpu.CompilerParams(dimension_semantics=sem, vmem_limit_bytes=VMEM_LIMIT)


def _bdot(a, b):
    return jnp.dot(a, b, preferred_element_type=F32)


def _dot_nt(a, b, precision=None):
    return lax.dot_general(a, b, (((1,), (1,)), ((), ())), preferred_element_type=F32, precision=precision)


def _dot_tn(a, b):
    return lax.dot_general(a, b, (((0,), (0,)), ((), ())), preferred_element_type=F32)


def _layer_norm(z, g, b):
    mu = jnp.mean(z, axis=-1, keepdims=True)
    zc = z - mu
    var = jnp.mean(zc * zc, axis=-1, keepdims=True)
    return zc * lax.rsqrt(var + LN_EPS) * g + b


def _silu(t):
    return t * (0.5 * jnp.tanh(0.5 * t) + 0.5)


def _rope128(t, c, s1, s2):
    up = pltpu.roll(t, LANES - ROT_HALF, axis=1)
    dn = pltpu.roll(t, ROT_HALF, axis=1)
    return t * c + up * s1 + dn * s2


def _gla_proj_kernel(x_ref, wm_ref, wg_ref, wgu_ref, bg_ref, q_ref, k_ref, v_ref, r_ref, la_ref):
    xb = x_ref[...].astype(BF16)
    hk = GLA_HEADS * GLA_DK
    hv = GLA_HEADS * GLA_DV
    q_ref[...] = _bdot(xb, wm_ref[:, 0:hk])
    k_ref[...] = _bdot(xb, wm_ref[:, hk:2 * hk])
    v_ref[...] = _bdot(xb, wm_ref[:, 2 * hk:2 * hk + hv])
    r_ref[...] = _bdot(xb, wm_ref[:, 2 * hk + hv:2 * hk + 2 * hv])
    g_low = _bdot(xb, wg_ref[...])
    z = _bdot(g_low.astype(BF16), wgu_ref[...]) + bg_ref[...]
    log_sig = jnp.minimum(z, 0.0) - jnp.log1p(jnp.exp(-jnp.abs(z)))
    la_ref[...] = log_sig * (1.0 / GLA_TAU)


def _gla_proj(x2, w_main, w_glow, w_gu, b_gate):
    n = x2.shape[0]
    hk = GLA_HEADS * GLA_DK
    hv = GLA_HEADS * GLA_DV
    tm = PROJ_TM
    row = lambda i: (i, 0)
    full = lambda i: (0, 0)
    return pl.pallas_call(
        _gla_proj_kernel,
        grid=(n // tm,),
        in_specs=[pl.BlockSpec((tm, D_MODEL), row),
                  pl.BlockSpec(w_main.shape, full),
                  pl.BlockSpec(w_glow.shape, full),
                  pl.BlockSpec(w_gu.shape, full),
                  pl.BlockSpec(b_gate.shape, full)],
        out_specs=[pl.BlockSpec((tm, hk), row), pl.BlockSpec((tm, hk), row),
                   pl.BlockSpec((tm, hv), row), pl.BlockSpec((tm, hv), row),
                   pl.BlockSpec((tm, hk), row)],
        out_shape=[jax.ShapeDtypeStruct((n, hk), F32), jax.ShapeDtypeStruct((n, hk), F32),
                   jax.ShapeDtypeStruct((n, hv), F32), jax.ShapeDtypeStruct((n, hv), F32),
                   jax.ShapeDtypeStruct((n, hk), F32)],
        compiler_params=_cparams(("arbitrary",)),
        name="gla_proj",
    )(x2, w_main, w_glow, w_gu, b_gate)


def _dot_01(m, x):
    x1 = x.astype(BF16)
    r1 = x - x1.astype(F32)
    x2 = r1.astype(BF16)
    x3 = (r1 - x2.astype(F32)).astype(BF16)
    return _bdot(m, x1) + _bdot(m, x2) + _bdot(m, x3)


def _gla_core_kernel(q_ref, k_ref, v_ref, la_ref, r_ref, g_ref, o_ref, st_ref):
    @pl.when(pl.program_id(2) == 0)
    def _():
        st_ref[...] = jnp.zeros_like(st_ref)

    c, t = GLA_CHUNK, GLA_T
    shift = c.bit_length() - 1
    row = lax.broadcasted_iota(jnp.int32, (t, t), 0)
    col = lax.broadcasted_iota(jnp.int32, (t, t), 1)
    causal = ((row >> shift) == (col >> shift)) & (row >= col)
    in_chunk = lax.broadcasted_iota(jnp.int32, (t, GLA_DK), 0) & (c - 1)
    b = la_ref[0]
    step = 1
    while step < c:
        b = b + jnp.where(in_chunk >= step, pltpu.roll(b, step, axis=0), 0.0)
        step *= 2
    b_last = jnp.broadcast_to(b.reshape(t // c, c, GLA_DK)[:, c - 1:c, :], (t // c, c, GLA_DK)).reshape(t, GLA_DK)
    q = q_ref[0] * (GLA_DK ** -0.5)
    k = k_ref[0]
    vb = v_ref[0].astype(BF16)
    q_dec = (q * jnp.exp(b)).astype(BF16)
    k_neg = (k * jnp.exp(-b)).astype(BF16)
    k_dec = (k * jnp.exp(b_last - b)).astype(BF16)
    decay = jnp.exp(b_last)
    s = jnp.where(causal, _dot_nt(q_dec, k_neg), 0.0)
    o = _bdot(s.astype(BF16), vb)

    chunks = [slice(ci * c, (ci + 1) * c) for ci in range(t // c)]
    kv = [_dot_tn(vb[rows], k_dec[rows]) for rows in chunks]
    st = st_ref[...]
    entering = []
    for ci, rows in enumerate(chunks):
        entering.append(st.astype(BF16))
        st = decay[ci * c:ci * c + 1] * st + kv[ci]
    st_ref[...] = st
    o = o + jnp.concatenate([_dot_nt(q_dec[rows], s_in) for rows, s_in in zip(chunks, entering)], axis=0)
    ms = jnp.mean(o * o, axis=-1, keepdims=True)
    o = o * lax.rsqrt(ms + LN_EPS) * g_ref[...]
    o_ref[0] = o * _silu(r_ref[0])


def _gla_core(q, k, v, la, r, norm_g):
    bsz, s, _ = q.shape
    t = GLA_T
    kmap = lambda b, h, n: (b, n, h)
    return pl.pallas_call(
        _gla_core_kernel,
        grid=(bsz, GLA_HEADS, s // t),
        in_specs=[pl.BlockSpec((1, t, GLA_DK), kmap), pl.BlockSpec((1, t, GLA_DK), kmap),
                  pl.BlockSpec((1, t, GLA_DV), kmap), pl.BlockSpec((1, t, GLA_DK), kmap),
                  pl.BlockSpec((1, t, GLA_DV), kmap),
                  pl.BlockSpec((1, GLA_DV), lambda b, h, n: (0, 0))],
        out_specs=pl.BlockSpec((1, t, GLA_DV), kmap),
        out_shape=jax.ShapeDtypeStruct((bsz, s, GLA_HEADS * GLA_DV), F32),
        scratch_shapes=[pltpu.VMEM((GLA_DV, GLA_DK), F32)],
        compiler_params=_cparams(("arbitrary", "arbitrary", "arbitrary")),
        name="gla_core",
    )(q, k, v, la, r, norm_g)


def _outproj_ln_kernel(h_ref, x_ref, w_ref, g_ref, b_ref, o_ref):
    y = _bdot(h_ref[...].astype(BF16), w_ref[...])
    o_ref[...] = _layer_norm(DN_ALPHA * x_ref[...] + y, g_ref[...], b_ref[...])


def _outproj_heads_ln_kernel(h_ref, x_ref, w_ref, g_ref, b_ref, o_ref):
    h = jnp.concatenate([h_ref[0, i].astype(BF16) for i in range(h_ref.shape[1])], axis=1)
    o_ref[...] = _layer_norm(DN_ALPHA * x_ref[...] + _bdot(h, w_ref[...]), g_ref[...], b_ref[...])


def _outproj_heads_ln(h4, x2, w, g, b):
    n = x2.shape[0]
    _, heads, seq, d = h4.shape
    tm = OUT_TM
    per = seq // tm
    row = lambda i: (i, 0)
    full = lambda i: (0, 0)
    return pl.pallas_call(
        _outproj_heads_ln_kernel,
        grid=(n // tm,),
        in_specs=[pl.BlockSpec((1, heads, tm, d), lambda i: (i // per, 0, i % per, 0)),
                  pl.BlockSpec((tm, D_MODEL), row),
                  pl.BlockSpec(w.shape, full), pl.BlockSpec((1, D_MODEL), full), pl.BlockSpec((1, D_MODEL), full)],
        out_specs=pl.BlockSpec((tm, D_MODEL), row),
        out_shape=jax.ShapeDtypeStruct((n, D_MODEL), F32),
        compiler_params=_cparams(("arbitrary",)),
        name="outproj_heads_ln",
    )(h4, x2, w, g, b)


def _outproj_ln(h2, x2, w, g, b):
    n = x2.shape[0]
    tm = OUT_TM
    row = lambda i: (i, 0)
    full = lambda i: (0, 0)
    return pl.pallas_call(
        _outproj_ln_kernel,
        grid=(n // tm,),
        in_specs=[pl.BlockSpec((tm, h2.shape[1]), row), pl.BlockSpec((tm, D_MODEL), row),
                  pl.BlockSpec(w.shape, full), pl.BlockSpec((1, D_MODEL), full), pl.BlockSpec((1, D_MODEL), full)],
        out_specs=pl.BlockSpec((tm, D_MODEL), row),
        out_shape=jax.ShapeDtypeStruct((n, D_MODEL), F32),
        compiler_params=_cparams(("arbitrary",)),
        name="outproj_ln",
    )(h2, x2, w, g, b)


def _lane_xor(v, k, lane):
    up = pltpu.roll(v, LANES - k, axis=1)
    dn = pltpu.roll(v, k, axis=1)
    return jnp.where((lane & k) == 0, up, dn)


def _route(x, w_hi, w_lo, bias):
    x_hi = x.astype(BF16)
    x_lo = (x - x_hi.astype(F32)).astype(BF16)
    logits = _bdot(x_hi, w_hi) + (_bdot(x_hi, w_lo) + _bdot(x_lo, w_hi))
    s = jax.nn.sigmoid(logits)
    a = s + bias
    lane = lax.broadcasted_iota(jnp.int32, a.shape, 1)
    a1 = _lane_xor(a, 1, lane)
    a2 = _lane_xor(a, 2, lane)
    a3 = _lane_xor(a1, 2, lane)
    p, q = jnp.maximum(a, a1), jnp.minimum(a, a1)
    r, t = jnp.maximum(a2, a3), jnp.minimum(a2, a3)
    gs = jnp.maximum(p, r) + jnp.maximum(jnp.minimum(p, r), jnp.maximum(q, t))
    g1 = _lane_xor(gs, 4, lane)
    g2 = _lane_xor(gs, 8, lane)
    g3 = _lane_xor(g1, 8, lane)

    def beats(other, other_first):
        return (other > gs) | ((other == gs) & other_first)

    lose = (beats(g1, (lane & 4) != 0) | beats(g2, (lane & 8) != 0) | beats(g3, (lane & 8) != 0))

    def ahead(other, other_first):
        return jnp.where((other > a) | ((other == a) & other_first), 1.0, 0.0)

    rank = ahead(a1, (lane & 1) != 0) + ahead(a2, (lane & 2) != 0) + ahead(a3, (lane & 2) != 0)
    sel = jnp.logical_not(lose) & (rank < 2.0) & (lane < N_EXPERTS)
    ssel = jnp.where(sel, s, 0.0)
    s1 = _lane_xor(ssel, 1, lane)
    tot = (ssel + s1) + (_lane_xor(ssel, 2, lane) + _lane_xor(s1, 2, lane))
    return jnp.where(sel, s / tot, 0.0)


def _moe_kernel(x_ref, wrh_ref, wrl_ref, br_ref, wgu_ref, wd_ref, lg_ref, lb_ref, o_ref, h_ref):
    x = x_ref[...]
    gate = _route(x, wrh_ref[...], wrl_ref[...], br_ref[...])
    xb = x.astype(BF16)
    f = D_FF_EXPERT
    for e in range(N_EXPERTS):
        hgu = _bdot(xb, wgu_ref[e])
        h_ref[:, e * f:(e + 1) * f] = (_silu(hgu[:, :f]) * hgu[:, f:] * gate[:, e:e + 1]).astype(BF16)
    y = _bdot(h_ref[...], wd_ref[...])
    o_ref[...] = _layer_norm(DN_ALPHA * x + y, lg_ref[...], lb_ref[...])


def _moe_ffn_ln(x2, wr_hi, wr_lo, br, wgu, wd, lg, lb):
    n = x2.shape[0]
    tm = min(MOE_TM, n)
    row = lambda i: (i, 0)
    full = lambda i: (0, 0)
    resident = pl.Buffered(1)
    return pl.pallas_call(
        _moe_kernel,
        grid=(n // tm,),
        in_specs=[pl.BlockSpec((tm, D_MODEL), row),
                  pl.BlockSpec(wr_hi.shape, full), pl.BlockSpec(wr_lo.shape, full), pl.BlockSpec(br.shape, full),
                  pl.BlockSpec(wgu.shape, lambda i: (0, 0, 0), pipeline_mode=resident),
                  pl.BlockSpec(wd.shape, lambda i: (0, 0), pipeline_mode=resident),
                  pl.BlockSpec((1, D_MODEL), lambda i: (0, 0)),
                  pl.BlockSpec((1, D_MODEL), lambda i: (0, 0))],
        out_specs=pl.BlockSpec((tm, D_MODEL), row),
        out_shape=jax.ShapeDtypeStruct((n, D_MODEL), F32),
        scratch_shapes=[pltpu.VMEM((tm, N_EXPERTS * D_FF_EXPERT), BF16)],
        compiler_params=pltpu.CompilerParams(dimension_semantics=("arbitrary",), vmem_limit_bytes=MOE_VMEM_LIMIT),
        name="moe_ffn",
    )(x2, wr_hi, wr_lo, br, wgu, wd, lg, lb)


def _nsa_proj_kernel(x_ref, wr_ref, wp_ref, wgt_ref, c_ref, s1_ref, s2_ref,
                     q_ref, kst_ref, kw_ref, kc_ref, vc_ref, vs_ref, vw_ref, gt_ref):
    xb = x_ref[...].astype(BF16)
    c, s1, s2 = c_ref[...], s1_ref[...], s2_ref[...]
    d = NSA_HEAD_DIM
    nq = NSA_HEADS * d // LANES
    ng = NSA_GROUPS * d // LANES
    first_head = lax.broadcasted_iota(jnp.int32, (x_ref.shape[0], LANES), 1) < d

    def put_rows(ref, j, t):
        ref[0, 2 * j] = t[:, :d].astype(ref.dtype)
        ref[0, 2 * j + 1] = t[:, d:].astype(ref.dtype)

    def put_with_ones(ref, j, t):
        ref[0, 2 * j] = jnp.where(first_head, t, 1.0).astype(ref.dtype)
        ref[0, 2 * j + 1] = jnp.where(first_head, pltpu.roll(t, d, axis=1), 1.0).astype(ref.dtype)

    for j in range(nq + 2 * ng):
        t = _rope128(_bdot(xb, wr_ref[:, j * LANES:(j + 1) * LANES]), c, s1, s2)
        if j < nq:
            put_rows(q_ref, j, t * (d ** -0.5 * LOG2E))
        elif j < nq + ng:
            tt = t.T.astype(BF16)
            kst_ref[0, 2 * (j - nq), 0] = tt[:d]
            kst_ref[0, 2 * (j - nq) + 1, 0] = tt[d:]
        else:
            put_rows(kw_ref, j - nq - ng, t)
    for idx, (ref, put) in enumerate(((kc_ref, put_rows), (vc_ref, put_rows),
                                      (vs_ref, put_with_ones), (vw_ref, put_with_ones))):
        for j in range(ng):
            col = (idx * ng + j) * LANES
            put(ref, j, _bdot(xb, wp_ref[:, col:col + LANES]))
    gt_ref[...] = jax.nn.sigmoid(_bdot(xb, wgt_ref[...]))


def _nsa_proj(x2, w_rope, w_plain, w_gates, rc, rs1, rs2, seq):
    n = x2.shape[0]
    tm = PROJ_TM
    gd = NSA_GROUPS * NSA_HEAD_DIM
    row = lambda i: (i, 0)
    full = lambda i: (0, 0)
    per = seq // tm
    pos = lambda i: (i % per, 0)
    bsz, g, d = n // seq, NSA_GROUPS, NSA_HEAD_DIM
    sub = ATT_TK // tm
    hmap = lambda i: (i // per, 0, i % per, 0)

    def rows_out(heads, width, dtype):
        return pl.BlockSpec((1, heads, tm, width), hmap), jax.ShapeDtypeStruct((bsz, heads, seq, width), dtype)

    outs = [rows_out(NSA_HEADS, d, BF16),
            (pl.BlockSpec((1, g, 1, d, tm), lambda i: (i // per, 0, (i % per) // sub, 0, (i % per) % sub)),
             jax.ShapeDtypeStruct((bsz, g, seq // ATT_TK, d, ATT_TK), BF16)),
            rows_out(g, d, BF16),
            rows_out(g, d, F32), rows_out(g, d, F32),
            rows_out(g, 2 * d, BF16), rows_out(g, 2 * d, BF16),
            (pl.BlockSpec((tm, g * LANES), row), jax.ShapeDtypeStruct((n, g * LANES), F32))]
    return pl.pallas_call(
        _nsa_proj_kernel,
        grid=(n // tm,),
        in_specs=[pl.BlockSpec((tm, D_MODEL), row),
                  pl.BlockSpec(w_rope.shape, full), pl.BlockSpec(w_plain.shape, full),
                  pl.BlockSpec(w_gates.shape, full),
                  pl.BlockSpec((tm, LANES), pos), pl.BlockSpec((tm, LANES), pos), pl.BlockSpec((tm, LANES), pos)],
        out_specs=[o[0] for o in outs],
        out_shape=[o[1] for o in outs],
        compiler_params=_cparams(("arbitrary",)),
        name="nsa_proj",
    )(x2, w_rope, w_plain, w_gates, rc, rs1, rs2)


def _compress_kernel(ck_ref, cv_ref, pet_ref, peb_ref, wk1_ref, wk2_ref, wv1_ref, wv2_ref,
                     c_ref, s1_ref, s2_ref, ko_ref, vo_ref):
    nch = ck_ref.shape[2]
    half = CMP_STRIDE * NSA_HEAD_DIM
    pet, peb = pet_ref[...], peb_ref[...]

    def mlp(ch, w1_ref, w2_ref):
        a = _bdot((ch + pet).astype(BF16), w1_ref[0:half, :])
        bm = _bdot((ch + peb).astype(BF16), w1_ref[half:2 * half, :])
        h = _silu(a + pltpu.roll(bm, nch - 1, axis=0))
        return _bdot(h.astype(BF16), w2_ref[...])

    kc = _rope128(mlp(ck_ref[0, 0], wk1_ref, wk2_ref), c_ref[...], s1_ref[...], s2_ref[...])
    ko_ref[0, 0] = kc[:, :NSA_HEAD_DIM].astype(BF16)
    vo_ref[0, 0] = mlp(cv_ref[0, 0], wv1_ref, wv2_ref)[:, :NSA_HEAD_DIM].astype(BF16)


def _compress(ck, cv, pet, peb, wk1, wk2, wv1, wv2, cc, cs1, cs2):
    bsz, g, nch, width = ck.shape
    blk = lambda b, gi: (b, gi, 0, 0)
    full = lambda b, gi: (0, 0)
    return pl.pallas_call(
        _compress_kernel,
        grid=(bsz, g),
        in_specs=[pl.BlockSpec((1, 1, nch, width), blk), pl.BlockSpec((1, 1, nch, width), blk),
                  pl.BlockSpec(pet.shape, full), pl.BlockSpec(peb.shape, full),
                  pl.BlockSpec(wk1.shape, full), pl.BlockSpec(wk2.shape, full),
                  pl.BlockSpec(wv1.shape, full), pl.BlockSpec(wv2.shape, full),
                  pl.BlockSpec(cc.shape, full), pl.BlockSpec(cs1.shape, full), pl.BlockSpec(cs2.shape, full)],
        out_specs=[pl.BlockSpec((1, 1, nch, NSA_HEAD_DIM), blk), pl.BlockSpec((1, 1, nch, NSA_HEAD_DIM), blk)],
        out_shape=[jax.ShapeDtypeStruct((bsz, g, nch, NSA_HEAD_DIM), BF16)] * 2,
        compiler_params=_cparams(("arbitrary", "arbitrary")),
        name="nsa_compress",
    )(ck, cv, pet, peb, wk1, wk2, wv1, wv2, cc, cs1, cs2)


def _cmp_select_kernel(q_ref, kc_ref, vc_ref, wimp_ref, o_ref, sel_ref):
    tq = q_ref.shape[2]
    nch = kc_ref.shape[2]
    ns = wimp_ref.shape[0]
    s0 = pl.program_id(2) * tq
    ratio = SLC_BLOCK // CMP_STRIDE
    n_vis = (s0 + tq - CMP_BLOCK) // CMP_STRIDE + 1
    n_chunks = (n_vis + CMP_CW - 1) // CMP_CW

    def body(w):
        nb = w // ratio
        qpos_c = s0 + lax.broadcasted_iota(jnp.int32, (tq, 1), 0)
        cend = lax.broadcasted_iota(jnp.int32, (1, w), 1) * CMP_STRIDE + (CMP_BLOCK - 1)
        cmask = cend <= qpos_c
        anyvis = (qpos_c >= CMP_BLOCK - 1).astype(F32)
        kc = kc_ref[0, 0, :w, :]
        vc = vc_ref[0, 0, :w, :]
        pg = jnp.zeros((tq, w), F32)
        for h in range(NSA_HPG):
            s = jnp.where(cmask, _dot_nt(q_ref[0, h], kc), NEG)
            e = jnp.exp2(s - jnp.max(s, axis=-1, keepdims=True))
            p = e * (anyvis / jnp.sum(e, axis=-1, keepdims=True))
            o_ref[0, h] = _bdot(p.astype(BF16), vc)
            pg = pg + p

        pg_hi = pg.astype(BF16)
        pg_lo = (pg - pg_hi.astype(F32)).astype(BF16)
        wimp = wimp_ref[:nb, :w]
        imp = _dot_nt(wimp, pg_hi) + _dot_nt(wimp, pg_lo)
        blk = lax.broadcasted_iota(jnp.int32, (nb, tq), 0)
        cur = (s0 + lax.broadcasted_iota(jnp.int32, (nb, tq), 1)) >> SLC_SHIFT
        forced = (blk == 0) | (blk == cur) | (blk == cur - 1)
        imp = jnp.where(blk <= cur, jnp.where(forced, BIG, imp), NEG)
        parts = [slice(i, i + LANES) for i in range(0, tq, LANES)]
        blk_f = blk[:, :LANES].astype(F32)
        imps = [imp[:, pt] for pt in parts]
        sels = [jnp.zeros((nb, LANES), F32) for _ in parts]
        for _ in range(min(SLC_TOP_N, nb)):
            for i in range(len(parts)):
                m = jnp.max(imps[i], axis=0, keepdims=True)
                first = jnp.min(jnp.where(imps[i] == m, blk_f, float(nb)), axis=0, keepdims=True)
                hit = blk_f == first
                sels[i] = jnp.where(hit, 1.0, sels[i])
                imps[i] = jnp.where(hit, -jnp.inf, imps[i])
        sel = jnp.concatenate(sels, axis=1)
        if nb < ns:
            sel = jnp.concatenate([sel, jnp.zeros((ns - nb, tq), F32)], axis=0)
        sel_ref[0, 0] = jnp.where(sel.T > 0.0, 0.0, NEG).astype(BF16)

    for nv in range(1, nch // CMP_CW + 1):
        pl.when(n_chunks == nv)(functools.partial(body, nv * CMP_CW))


def _cmp_select(q, k_cmp, v_cmp, wimp_t):
    bsz, _, s, d = q.shape
    g = NSA_GROUPS
    nch = k_cmp.shape[2]
    ns = wimp_t.shape[0]
    tq = min(CMP_TQ, s)
    hmap = lambda b, gi, qi: (b, gi, qi, 0)
    return pl.pallas_call(
        _cmp_select_kernel,
        grid=(bsz, g, s // tq),
        in_specs=[pl.BlockSpec((1, NSA_HPG, tq, d), hmap),
                  pl.BlockSpec((1, 1, nch, d), lambda b, gi, qi: (b, gi, 0, 0)),
                  pl.BlockSpec((1, 1, nch, d), lambda b, gi, qi: (b, gi, 0, 0)),
                  pl.BlockSpec(wimp_t.shape, lambda b, gi, qi: (0, 0))],
        out_specs=[pl.BlockSpec((1, NSA_HPG, tq, d), hmap),
                   pl.BlockSpec((1, 1, tq, ns), hmap)],
        out_shape=[jax.ShapeDtypeStruct((bsz, NSA_HEADS, s, d), F32),
                   jax.ShapeDtypeStruct((bsz, g, s, ns), BF16)],
        compiler_params=_cparams(("arbitrary", "arbitrary", "arbitrary")),
        name="nsa_cmp_select",
    )(q, k_cmp, v_cmp, wimp_t)


def _nsa_attn_kernel(q_ref, kst_ref, vs_ref, kw_ref, vw_ref, selb_ref, eexp_ref, ocmp_ref, gt_ref,
                     o_ref, m_ref, acc_ref, s_ref, p_ref, a_ref):
    hpg, tq, d = q_ref.shape[1:]
    rows = hpg * tq
    s0 = pl.program_id(2) * tq
    q4 = q_ref[0].reshape(rows, d)
    q4a = jnp.concatenate([jnp.concatenate([selb_ref[0, 0]] * hpg, axis=0), q4], axis=1)
    qpos = s0 + lax.broadcasted_iota(jnp.int32, (tq, 1), 0)

    def scores(t):
        return _bdot(q4a, jnp.concatenate([eexp_ref[t], kst_ref[0, 0, t]], axis=0))

    def flush(t):
        acc_ref[...] = a_ref[...] * acc_ref[...] + _bdot(p_ref[...], vs_ref[0, 0, t])

    def softmax_step(s):
        m_prev = m_ref[...]
        m_new = jnp.maximum(m_prev, jnp.max(s, axis=-1, keepdims=True))
        a_ref[...] = jnp.exp2(m_prev - m_new)
        p_ref[...] = jnp.exp2(s - m_new).astype(BF16)
        m_ref[...] = m_new

    m_ref[...] = jnp.full_like(m_ref, NEG)
    acc_ref[...] = jnp.zeros_like(acc_ref)
    p_ref[...] = jnp.zeros_like(p_ref)
    a_ref[...] = jnp.ones_like(a_ref)
    t_diag = s0 // ATT_TK
    s_ref[...] = scores(0)

    def step(t):
        s = s_ref[...]
        s_next = scores(t + 1)
        flush(jnp.maximum(t - 1, 0))
        softmax_step(s)
        s_ref[...] = s_next

    def pair_body(i, carry):
        step(2 * i)
        step(2 * i + 1)
        return carry

    lax.fori_loop(0, t_diag // 2, pair_body, 0)
    pl.when(t_diag % 2 == 1)(lambda: step(t_diag - 1))

    span = WINDOW + tq
    w0 = pl.multiple_of(jnp.maximum(s0 - WINDOW, 0), tq)
    rel = qpos - (w0 + lax.broadcasted_iota(jnp.int32, (1, span), 1))
    wbias = jnp.where((rel >= 0) & (rel < WINDOW), 0.0, NEG)
    sw = _dot_nt(q4, kw_ref[0, 0, pl.ds(w0, span), :])
    sw = (sw.reshape(hpg, tq, span) + wbias[None]).reshape(rows, span)
    pw = jnp.exp2(sw - jnp.max(sw, axis=-1, keepdims=True)).astype(BF16)
    accw = _bdot(pw, vw_ref[0, 0, pl.ds(w0, span), :])
    o_win = (accw[:, :d] / accw[:, d:d + 1]).reshape(hpg, tq, d)

    flush(jnp.maximum(t_diag - 1, 0))
    kpos = t_diag * ATT_TK + lax.broadcasted_iota(jnp.int32, (1, ATT_TK), 1)
    s = s_ref[...].reshape(hpg, tq, ATT_TK)
    softmax_step(jnp.where((kpos <= qpos)[None], s, NEG).reshape(rows, ATT_TK))
    flush(t_diag)
    acc = acc_ref[...]
    o_slc = (acc[:, :d] / acc[:, d:d + 1]).reshape(hpg, tq, d)

    gts = gt_ref[0]
    for h in range(hpg):
        g0, g1, g2 = (gts[:, 3 * h + j:3 * h + j + 1] for j in range(3))
        o_ref[0, h] = g0 * ocmp_ref[0, h] + g1 * o_slc[h] + g2 * o_win[h]


def _nsa_attn(q, kst, vst, kw, vw, selb, eexp, o_cmp, gates):
    bsz, _, s, d = q.shape
    g = NSA_GROUPS
    tq = ATT_TQ
    rows = NSA_HPG * tq
    ns = selb.shape[-1]
    hmap = lambda b, gi, qi: (b, gi, qi, 0)
    tmap = lambda b, gi, qi: (b, gi, 0, 0, 0)
    rmap = lambda b, gi, qi: (b, gi, 0, 0)
    return pl.pallas_call(
        _nsa_attn_kernel,
        grid=(bsz, g, s // tq),
        in_specs=[pl.BlockSpec((1, NSA_HPG, tq, d), hmap),
                  pl.BlockSpec((1, 1) + kst.shape[2:], tmap), pl.BlockSpec((1, 1) + vst.shape[2:], tmap),
                  pl.BlockSpec((1, 1) + kw.shape[2:], rmap), pl.BlockSpec((1, 1) + vw.shape[2:], rmap),
                  pl.BlockSpec((1, 1, tq, ns), hmap),
                  pl.BlockSpec(eexp.shape, lambda b, gi, qi: (0, 0, 0)),
                  pl.BlockSpec((1, NSA_HPG, tq, d), hmap),
                  pl.BlockSpec((1, tq, LANES), lambda b, gi, qi: (b, qi, gi))],
        out_specs=pl.BlockSpec((1, NSA_HPG, tq, d), hmap),
        out_shape=jax.ShapeDtypeStruct((bsz, NSA_HEADS, s, d), F32),
        scratch_shapes=[pltpu.VMEM((rows, 1), F32),
                        pltpu.VMEM((rows, 2 * d), F32),
                        pltpu.VMEM((rows, ATT_TK), F32),
                        pltpu.VMEM((rows, ATT_TK), BF16),
                        pltpu.VMEM((rows, 1), F32)],
        compiler_params=_cparams(("arbitrary", "arbitrary", "arbitrary")),
        name="nsa_attn",
    )(q, kst, vst, kw, vw, selb, eexp, o_cmp, gates)


def _rope_tables(pos):
    inv_freq = jnp.power(ROPE_THETA, -jnp.arange(ROT_HALF, dtype=F32) * (2.0 / ROT_DIM))
    ang = pos.astype(F32)[:, None] * inv_freq[None, :]
    cos, sin = jnp.cos(ang), jnp.sin(ang)
    n = pos.shape[0]
    one = jnp.ones((n, NSA_HEAD_DIM - ROT_DIM), F32)
    zero8 = jnp.zeros((n, ROT_HALF), F32)
    zero = jnp.zeros((n, NSA_HEAD_DIM - ROT_DIM), F32)
    c = jnp.concatenate([cos, cos, one], axis=1)
    s1 = jnp.concatenate([-sin, zero8, zero], axis=1)
    s2 = jnp.concatenate([zero8, sin, zero], axis=1)
    rep = LANES // NSA_HEAD_DIM
    return tuple(jnp.tile(t, (1, rep)) for t in (c, s1, s2))


def _importance_weights(ns, nch):
    ratio = SLC_BLOCK // CMP_STRIDE
    span = CMP_BLOCK // CMP_STRIDE
    w = np.zeros((ns, nch), np.float32)
    for j in range(ns):
        for m in range(ratio):
            for n in range(span):
                c = ratio * j + m + n
                if c < nch - 1:
                    w[j, c] += 1.0
    return jnp.asarray(w, dtype=BF16)


def _block_expansion(nt, ns):
    key_blk = (np.arange(nt)[:, None] * ATT_TK + np.arange(ATT_TK)[None, :]) // SLC_BLOCK
    e = (np.arange(ns)[None, :, None] == key_blk[:, None, :]).astype(np.float32)
    return jnp.asarray(e, dtype=BF16)


def _gla_layer(x2, bsz, seq, w_in, w_gate_up, b_gate, norm_g, w_out, ln_g, ln_b):
    hk = GLA_HEADS * GLA_DK
    hv = GLA_HEADS * GLA_DV
    cuts = np.cumsum([hk, hk, hv, GLA_GATE_RANK]).tolist()
    wq, wk, wv, wg, wr = jnp.split(w_in, cuts, axis=1)
    w_main = jnp.concatenate([wq, wk, wv, wr], axis=1).astype(BF16)
    w_glow = jnp.pad(wg, ((0, 0), (0, LANES - GLA_GATE_RANK))).astype(BF16)
    w_gu = jnp.pad(w_gate_up, ((0, LANES - GLA_GATE_RANK), (0, 0))).astype(BF16)
    q, k, v, r, la = _gla_proj(x2, w_main, w_glow, w_gu, b_gate.reshape(1, hk))
    sh = lambda t: t.reshape(bsz, seq, t.shape[-1])
    o = _gla_core(sh(q), sh(k), sh(v), sh(la), sh(r), norm_g.reshape(1, GLA_DV))
    return _outproj_ln(o.reshape(bsz * seq, hv), x2, w_out.astype(BF16),
                       ln_g.reshape(1, D_MODEL), ln_b.reshape(1, D_MODEL))


def _nsa_layer(x2, bsz, seq, w_in, w_ck1, w_ck2, w_cv1, w_cv2, cmp_pe, w_out, ln_g, ln_b):
    h, g, hpg, d = NSA_HEADS, NSA_GROUPS, NSA_HPG, NSA_HEAD_DIM
    gd = g * d
    cuts = np.cumsum([h * d] + [gd] * 6).tolist()
    wq, wkc, wvc, wks, wvs, wkw, wvw, wgt = jnp.split(w_in, cuts, axis=1)
    w_rope = jnp.concatenate([wq, wks, wkw], axis=1).astype(BF16)
    w_plain = jnp.concatenate([wkc, wvc, wvs, wvw], axis=1).astype(BF16)
    w_gates = jnp.pad(wgt.reshape(D_MODEL, g, hpg * 3), ((0, 0), (0, 0), (0, LANES - hpg * 3)))
    w_gates = w_gates.reshape(D_MODEL, g * LANES).astype(BF16)
    rc, rs1, rs2 = _rope_tables(jnp.arange(seq, dtype=jnp.int32))
    q, kst, kw, kc, vc, vs1, vw1, gates = _nsa_proj(x2, w_rope, w_plain, w_gates, rc, rs1, rs2, seq)

    nch = seq // CMP_STRIDE
    chunks = lambda t: t.reshape(bsz, g, nch, CMP_STRIDE * d)
    pe = cmp_pe.reshape(1, CMP_BLOCK * d)
    half = CMP_STRIDE * d
    pad2 = lambda w: jnp.pad(w, ((0, 0), (0, LANES - d))).astype(BF16)
    cc, cs1, cs2 = _rope_tables(jnp.arange(nch, dtype=jnp.int32) * CMP_STRIDE + (CMP_BLOCK - 1))
    k_cmp, v_cmp = _compress(chunks(kc), chunks(vc), pe[:, :half], pe[:, half:],
                             w_ck1.astype(BF16), pad2(w_ck2), w_cv1.astype(BF16), pad2(w_cv2), cc, cs1, cs2)

    ns = seq // SLC_BLOCK
    sh = lambda t: t.reshape(bsz, seq, t.shape[-1])
    o_cmp, selb = _cmp_select(q, k_cmp, v_cmp, _importance_weights(ns, nch))

    nt = seq // ATT_TK
    o = _nsa_attn(q, kst, vs1.reshape(bsz, g, nt, ATT_TK, 2 * d), kw, vw1,
                  selb, _block_expansion(nt, ns), o_cmp, sh(gates))
    return _outproj_heads_ln(o, x2, w_out.astype(BF16), ln_g.reshape(1, D_MODEL), ln_b.reshape(1, D_MODEL))


def _moe_layer(x2, wr_hi, wr_lo, b_router_pad, w_gate, w_up, w_down, ln_g, ln_b):
    wgu = jnp.concatenate([w_gate, w_up], axis=-1).astype(BF16)
    wd = w_down.astype(BF16).reshape(N_EXPERTS * D_FF_EXPERT, D_MODEL)
    return _moe_ffn_ln(x2, wr_hi, wr_lo, b_router_pad, wgu, wd, ln_g.reshape(1, D_MODEL), ln_b.reshape(1, D_MODEL))


def kernel(x, gla_w_in, gla_w_gate_up, gla_b_gate, gla_norm_g, gla_w_out, nsa_w_in, nsa_w_cmp_k1, nsa_w_cmp_k2,
           nsa_w_cmp_v1, nsa_w_cmp_v2, nsa_cmp_pe, nsa_w_out, moe_w_router, moe_b_router, moe_w_gate, moe_w_up,
           moe_w_down, ln_g, ln_b):
    bsz, seq, _ = x.shape
    x2 = x.reshape(bsz * seq, D_MODEL)
    w_router_pad = jnp.pad(moe_w_router, ((0, 0), (0, LANES - N_EXPERTS)))
    wr_hi = w_router_pad.astype(BF16)
    wr_lo = (w_router_pad - wr_hi.astype(F32)).astype(BF16)
    b_router_pad = jnp.pad(moe_b_router, (0, LANES - N_EXPERTS)).reshape(1, LANES)
    for i in range(DEPTH):
        j = i // 2
        if i % 2 == 0:
            x2 = _gla_layer(x2, bsz, seq, gla_w_in[j], gla_w_gate_up[j], gla_b_gate[j], gla_norm_g[j],
                            gla_w_out[j], ln_g[i, 0], ln_b[i, 0])
        else:
            x2 = _nsa_layer(x2, bsz, seq, nsa_w_in[j], nsa_w_cmp_k1[j], nsa_w_cmp_k2[j], nsa_w_cmp_v1[j],
                            nsa_w_cmp_v2[j], nsa_cmp_pe[j], nsa_w_out[j], ln_g[i, 0], ln_b[i, 0])
        x2 = _moe_layer(x2, wr_hi, wr_lo, b_router_pad, moe_w_gate[i], moe_w_up[i], moe_w_down[i],
                        ln_g[i, 1], ln_b[i, 1])
    return x2.reshape(bsz, seq, D_MODEL)
```

```python
import functools

import numpy as np
import jax
import jax.numpy as jnp
from jax import lax
from jax.experimental import pallas as pl
from jax.experimental.pallas import tpu as pltpu

F32 = jnp.float32
BF16 = jnp.bfloat16
HIGHEST = lax.Precision.HIGHEST

D_MODEL = 1024
DEPTH = 2

GLA_HEADS = 4
GLA_DK = D_MODEL // 2 // GLA_HEADS
GLA_DV = D_MODEL // GLA_HEADS
GLA_GATE_RANK = 16
GLA_TAU = 16.0
GLA_CHUNK = 64

NSA_HEADS = 16
NSA_GROUPS = 4
NSA_HPG = NSA_HEADS // NSA_GROUPS
NSA_HEAD_DIM = D_MODEL // NSA_HEADS
CMP_BLOCK = 32
CMP_STRIDE = 16
CMP_HIDDEN = 256
SLC_BLOCK = 64
SLC_SHIFT = SLC_BLOCK.bit_length() - 1
SLC_TOP_N = 16
WINDOW = 512

ROPE_THETA = 500000.0
ROT_DIM = NSA_HEAD_DIM // 4
ROT_HALF = ROT_DIM // 2

N_EXPERTS = 16
N_EXPERT_GROUPS = 4
EXPERTS_PER_GROUP = N_EXPERTS // N_EXPERT_GROUPS
D_FF_EXPERT = D_MODEL // 4

DN_ALPHA = (2 * DEPTH) ** 0.25
LN_EPS = 1e-5
NEG = -1e30
BIG = 1e30
LOG2E = 1.4426950408889634

LANES = 128
VMEM_LIMIT = 48 * 1024 * 1024

PROJ_TM = 256
OUT_TM = 512
GLA_T = 512
MOE_TS = 512
MOE_VMEM_LIMIT = 56 * 1024 * 1024
GROUP_LANE = N_EXPERTS
XA_WIDTH = D_MODEL + LANES
CMP_TQ = 256
CMP_CW = 128
ATT_TQ = 128
ATT_TK = 512
WIN_TK = 128


def _cparams(sem):
    return pltpu.CompilerParams(dimension_semantics=sem, vmem_limit_bytes=VMEM_LIMIT)


def _bdot(a, b):
    return jnp.dot(a, b, preferred_element_type=F32)


def _dot_nt(a, b, precision=None):
    return lax.dot_general(a, b, (((1,), (1,)), ((), ())), preferred_element_type=F32, precision=precision)


def _dot_tn(a, b):
    return lax.dot_general(a, b, (((0,), (0,)), ((), ())), preferred_element_type=F32)


def _layer_norm(z, g, b):
    mu = jnp.mean(z, axis=-1, keepdims=True)
    zc = z - mu
    var = jnp.mean(zc * zc, axis=-1, keepdims=True)
    return zc * lax.rsqrt(var + LN_EPS) * g + b


def _silu(t):
    return t * (0.5 * jnp.tanh(0.5 * t) + 0.5)


def _rope128(t, c, s1, s2):
    up = pltpu.roll(t, LANES - ROT_HALF, axis=1)
    dn = pltpu.roll(t, ROT_HALF, axis=1)
    return t * c + up * s1 + dn * s2


def _gla_proj_kernel(x_ref, wm_ref, wg_ref, wgu_ref, bg_ref, q_ref, k_ref, v_ref, r_ref, la_ref):
    xb = x_ref[...].astype(BF16)
    hk = GLA_HEADS * GLA_DK
    hv = GLA_HEADS * GLA_DV
    q_ref[...] = _bdot(xb, wm_ref[:, 0:hk])
    k_ref[...] = _bdot(xb, wm_ref[:, hk:2 * hk])
    v_ref[...] = _bdot(xb, wm_ref[:, 2 * hk:2 * hk + hv])
    r_ref[...] = _bdot(xb, wm_ref[:, 2 * hk + hv:2 * hk + 2 * hv])
    g_low = _bdot(xb, wg_ref[...])
    z = _bdot(g_low.astype(BF16), wgu_ref[...]) + bg_ref[...]
    log_sig = jnp.minimum(z, 0.0) - jnp.log1p(jnp.exp(-jnp.abs(z)))
    la_ref[...] = log_sig * (1.0 / GLA_TAU)


def _gla_proj(x2, w_main, w_glow, w_gu, b_gate):
    n = x2.shape[0]
    hk = GLA_HEADS * GLA_DK
    hv = GLA_HEADS * GLA_DV
    tm = PROJ_TM
    row = lambda i: (i, 0)
    full = lambda i: (0, 0)
    return pl.pallas_call(
        _gla_proj_kernel,
        grid=(n // tm,),
        in_specs=[pl.BlockSpec((tm, D_MODEL), row),
                  pl.BlockSpec(w_main.shape, full),
                  pl.BlockSpec(w_glow.shape, full),
                  pl.BlockSpec(w_gu.shape, full),
                  pl.BlockSpec(b_gate.shape, full)],
        out_specs=[pl.BlockSpec((tm, hk), row), pl.BlockSpec((tm, hk), row),
                   pl.BlockSpec((tm, hv), row), pl.BlockSpec((tm, hv), row),
                   pl.BlockSpec((tm, hk), row)],
        out_shape=[jax.ShapeDtypeStruct((n, hk), F32), jax.ShapeDtypeStruct((n, hk), F32),
                   jax.ShapeDtypeStruct((n, hv), F32), jax.ShapeDtypeStruct((n, hv), F32),
                   jax.ShapeDtypeStruct((n, hk), F32)],
        compiler_params=_cparams(("arbitrary",)),
        name="gla_proj",
    )(x2, w_main, w_glow, w_gu, b_gate)


def _dot_01(m, x):
    x1 = x.astype(BF16)
    r1 = x - x1.astype(F32)
    x2 = r1.astype(BF16)
    x3 = (r1 - x2.astype(F32)).astype(BF16)
    return _bdot(m, x1) + _bdot(m, x2) + _bdot(m, x3)


def _gla_core_kernel(q_ref, k_ref, v_ref, la_ref, r_ref, g_ref, o_ref, st_ref):
    @pl.when(pl.program_id(2) == 0)
    def _():
        st_ref[...] = jnp.zeros_like(st_ref)

    c, t = GLA_CHUNK, GLA_T
    shift = c.bit_length() - 1
    row = lax.broadcasted_iota(jnp.int32, (t, t), 0)
    col = lax.broadcasted_iota(jnp.int32, (t, t), 1)
    causal = ((row >> shift) == (col >> shift)) & (row >= col)
    in_chunk = lax.broadcasted_iota(jnp.int32, (t, GLA_DK), 0) & (c - 1)
    b = la_ref[0]
    step = 1
    while step < c:
        b = b + jnp.where(in_chunk >= step, pltpu.roll(b, step, axis=0), 0.0)
        step *= 2
    b_last = jnp.broadcast_to(b.reshape(t // c, c, GLA_DK)[:, c - 1:c, :], (t // c, c, GLA_DK)).reshape(t, GLA_DK)
    q = q_ref[0] * (GLA_DK ** -0.5)
    k = k_ref[0]
    vb = v_ref[0].astype(BF16)
    q_dec = (q * jnp.exp(b)).astype(BF16)
    k_neg = (k * jnp.exp(-b)).astype(BF16)
    k_dec = (k * jnp.exp(b_last - b)).astype(BF16)
    decay = jnp.exp(b_last)
    s = jnp.where(causal, _dot_nt(q_dec, k_neg), 0.0)
    o = _bdot(s.astype(BF16), vb)

    chunks = [slice(ci * c, (ci + 1) * c) for ci in range(t // c)]
    kv = [_dot_tn(vb[rows], k_dec[rows]) for rows in chunks]
    st = st_ref[...]
    entering = []
    for ci, rows in enumerate(chunks):
        entering.append(st.astype(BF16))
        st = decay[ci * c:ci * c + 1] * st + kv[ci]
    st_ref[...] = st
    o = o + jnp.concatenate([_dot_nt(q_dec[rows], s_in) for rows, s_in in zip(chunks, entering)], axis=0)
    ms = jnp.mean(o * o, axis=-1, keepdims=True)
    o = o * lax.rsqrt(ms + LN_EPS) * g_ref[...]
    o_ref[0] = o * _silu(r_ref[0])


def _gla_core(q, k, v, la, r, norm_g):
    bsz, s, _ = q.shape
    t = GLA_T
    kmap = lambda b, h, n: (b, n, h)
    return pl.pallas_call(
        _gla_core_kernel,
        grid=(bsz, GLA_HEADS, s // t),
        in_specs=[pl.BlockSpec((1, t, GLA_DK), kmap), pl.BlockSpec((1, t, GLA_DK), kmap),
                  pl.BlockSpec((1, t, GLA_DV), kmap), pl.BlockSpec((1, t, GLA_DK), kmap),
                  pl.BlockSpec((1, t, GLA_DV), kmap),
                  pl.BlockSpec((1, GLA_DV), lambda b, h, n: (0, 0))],
        out_specs=pl.BlockSpec((1, t, GLA_DV), kmap),
        out_shape=jax.ShapeDtypeStruct((bsz, s, GLA_HEADS * GLA_DV), F32),
        scratch_shapes=[pltpu.VMEM((GLA_DV, GLA_DK), F32)],
        compiler_params=_cparams(("arbitrary", "arbitrary", "arbitrary")),
        name="gla_core",
    )(q, k, v, la, r, norm_g)


def _row_xor(v, k, row):
    n = v.shape[0]
    up = pltpu.roll(v, n - k, axis=0)
    dn = pltpu.roll(v, k, axis=0)
    return jnp.where((row & k) == 0, up, dn)


def _route(x, wt_hi, wt_lo, bias):
    x_hi = x.astype(BF16)
    x_lo = (x - x_hi.astype(F32)).astype(BF16)
    logits = _dot_nt(wt_hi, x_hi) + (_dot_nt(wt_lo, x_hi) + _dot_nt(wt_hi, x_lo))
    s = jax.nn.sigmoid(logits)
    a = s + bias
    row = lax.broadcasted_iota(jnp.int32, a.shape, 0)
    a1 = _row_xor(a, 1, row)
    a2 = _row_xor(a, 2, row)
    a3 = _row_xor(a1, 2, row)
    p, q = jnp.maximum(a, a1), jnp.minimum(a, a1)
    r, t = jnp.maximum(a2, a3), jnp.minimum(a2, a3)
    gs = jnp.maximum(p, r) + jnp.maximum(jnp.minimum(p, r), jnp.maximum(q, t))
    g1 = _row_xor(gs, 4, row)
    g2 = _row_xor(gs, 8, row)
    g3 = _row_xor(g1, 8, row)

    def beats(other, other_first):
        return (other > gs) | ((other == gs) & other_first)

    lose = (beats(g1, (row & 4) != 0) | beats(g2, (row & 8) != 0) | beats(g3, (row & 8) != 0))

    def ahead(other, other_first):
        return jnp.where((other > a) | ((other == a) & other_first), 1.0, 0.0)

    rank = ahead(a1, (row & 1) != 0) + ahead(a2, (row & 2) != 0) + ahead(a3, (row & 2) != 0)
    sel = jnp.logical_not(lose) & (rank < 2.0)
    ssel = jnp.where(sel, s, 0.0)
    s1 = _row_xor(ssel, 1, row)
    tot = (ssel + s1) + (_row_xor(ssel, 2, row) + _row_xor(s1, 2, row))
    gate = jnp.where(sel, s / tot, 0.0)
    group = jnp.max(jnp.where(sel, (row >> 2).astype(F32), 0.0), axis=0, keepdims=True)
    tokens = x.shape[0]
    record = jnp.concatenate([gate, jnp.broadcast_to(group, (8, tokens)),
                              jnp.zeros((LANES - N_EXPERTS - 8, tokens), F32)], axis=0)
    lane = lax.broadcasted_iota(jnp.int32, (tokens, LANES), 1)
    return jnp.where(lane <= GROUP_LANE, record.T, 0.0)


def _norm_route_store(z, g_ref, b_ref, wrh_ref, wrl_ref, br_ref, o_ref):
    x = _layer_norm(z, g_ref[...], b_ref[...])
    o_ref[:, :D_MODEL] = x
    o_ref[:, D_MODEL:] = _route(x, wrh_ref[...], wrl_ref[...], br_ref[...])


def _outproj_ln_kernel(h_ref, x_ref, w_ref, g_ref, b_ref, wrh_ref, wrl_ref, br_ref, o_ref):
    y = _bdot(h_ref[...].astype(BF16), w_ref[...])
    _norm_route_store(DN_ALPHA * x_ref[...] + y, g_ref, b_ref, wrh_ref, wrl_ref, br_ref, o_ref)


def _outproj_heads_ln_kernel(h_ref, x_ref, w_ref, g_ref, b_ref, wrh_ref, wrl_ref, br_ref, o_ref):
    h = jnp.concatenate([h_ref[0, i].astype(BF16) for i in range(h_ref.shape[1])], axis=1)
    _norm_route_store(DN_ALPHA * x_ref[...] + _bdot(h, w_ref[...]), g_ref, b_ref, wrh_ref, wrl_ref, br_ref, o_ref)


def _outproj_call(kernel, name, h, h_spec, x2, w, g, b, router):
    n = x2.shape[0]
    tm = OUT_TM
    row = lambda i: (i, 0)
    full = lambda i: (0, 0)
    return pl.pallas_call(
        kernel,
        grid=(n // tm,),
        in_specs=[h_spec, pl.BlockSpec((tm, D_MODEL), row),
                  pl.BlockSpec(w.shape, full), pl.BlockSpec((1, D_MODEL), full), pl.BlockSpec((1, D_MODEL), full)]
                 + [pl.BlockSpec(r.shape, full) for r in router],
        out_specs=pl.BlockSpec((tm, XA_WIDTH), row),
        out_shape=jax.ShapeDtypeStruct((n, XA_WIDTH), F32),
        compiler_params=_cparams(("arbitrary",)),
        name=name,
    )(h, x2, w, g, b, *router)


def _outproj_heads_ln(h4, x2, w, g, b, router):
    _, heads, seq, d = h4.shape
    per = seq // OUT_TM
    spec = pl.BlockSpec((1, heads, OUT_TM, d), lambda i: (i // per, 0, i % per, 0))
    return _outproj_call(_outproj_heads_ln_kernel, "outproj_heads_ln", h4, spec, x2, w, g, b, router)


def _outproj_ln(h2, x2, w, g, b, router):
    spec = pl.BlockSpec((OUT_TM, h2.shape[1]), lambda i: (i, 0))
    return _outproj_call(_outproj_ln_kernel, "outproj_ln", h2, spec, x2, w, g, b, router)


def _moe_sorted_kernel(src_ref, glo_ref, ghi_ref, xa_hbm, wgu_ref, wd_ref, lg_ref, lb_ref, out_hbm,
                       xbuf, obuf, in_sem, out_sem):
    t = pl.program_id(0)
    nt = pl.num_programs(0)
    ts = MOE_TS
    slot = t & 1

    def row_in(tile, sl, r):
        return pltpu.make_async_copy(xa_hbm.at[src_ref[tile * ts + r]], xbuf.at[sl, r], in_sem.at[sl])

    def row_out(tile, sl, r):
        return pltpu.make_async_copy(obuf.at[sl, r], out_hbm.at[src_ref[tile * ts + r]], out_sem.at[sl])

    def wait_in(sl):
        pltpu.make_async_copy(xa_hbm.at[pl.ds(0, ts)], xbuf.at[sl], in_sem.at[sl]).wait()

    def wait_out(sl):
        pltpu.make_async_copy(obuf.at[sl], out_hbm.at[pl.ds(0, ts)], out_sem.at[sl]).wait()

    @pl.when(t == 0)
    def _():
        for r in range(ts):
            row_in(0, 0, r).start()

    @pl.when(t >= 2)
    def _():
        wait_out(slot)

    wait_in(slot)
    nxt = jnp.minimum(t + 1, nt - 1)
    for r in range(ts):
        row_in(nxt, 1 - slot, r).start()

    xa = xbuf[slot]
    x = xa[:, :D_MODEL]
    gate = xa[:, D_MODEL:]
    xb = x.astype(BF16)
    f = D_FF_EXPERT
    epg = EXPERTS_PER_GROUP

    def group_ffn(g):
        hgu = _bdot(xb, wgu_ref[g])
        hs = []
        for e in range(epg):
            col = gate[:, e:e + 1]
            for other in range(1, N_EXPERT_GROUPS):
                col = jnp.where(g == other, gate[:, other * epg + e:other * epg + e + 1], col)
            hs.append((_silu(hgu[:, e * f:(e + 1) * f]) * hgu[:, (epg + e) * f:(epg + e + 1) * f] * col).astype(BF16))
        return _bdot(jnp.concatenate(hs, axis=1), wd_ref[g])

    y = group_ffn(glo_ref[t])
    y = lax.fori_loop(glo_ref[t] + 1, ghi_ref[t] + 1, lambda g, acc: acc + group_ffn(g), y)
    obuf[slot] = _layer_norm(DN_ALPHA * x + y, lg_ref[...], lb_ref[...])
    for r in range(ts):
        row_out(t, slot, r).start()

    @pl.when(t == nt - 1)
    def _():
        wait_out(slot)
        wait_in(1 - slot)

    @pl.when((t == nt - 1) & (nt >= 2))
    def _():
        wait_out(1 - slot)


def _moe_sorted(xa, src, g_lo, g_hi, wgu, wd, lg, lb):
    n = xa.shape[0]
    ts = MOE_TS
    resident = pl.Buffered(1)
    grid_spec = pltpu.PrefetchScalarGridSpec(
        num_scalar_prefetch=3,
        grid=(n // ts,),
        in_specs=[pl.BlockSpec(memory_space=pl.ANY),
                  pl.BlockSpec(wgu.shape, lambda i, *_: (0, 0, 0), pipeline_mode=resident),
                  pl.BlockSpec(wd.shape, lambda i, *_: (0, 0, 0), pipeline_mode=resident),
                  pl.BlockSpec((1, D_MODEL), lambda i, *_: (0, 0)),
                  pl.BlockSpec((1, D_MODEL), lambda i, *_: (0, 0))],
        out_specs=pl.BlockSpec(memory_space=pl.ANY),
        scratch_shapes=[pltpu.VMEM((2, ts, XA_WIDTH), F32), pltpu.VMEM((2, ts, D_MODEL), F32),
                        pltpu.SemaphoreType.DMA((2,)), pltpu.SemaphoreType.DMA((2,))])
    return pl.pallas_call(
        _moe_sorted_kernel,
        grid_spec=grid_spec,
        out_shape=jax.ShapeDtypeStruct((n, D_MODEL), F32),
        compiler_params=pltpu.CompilerParams(dimension_semantics=("arbitrary",), vmem_limit_bytes=MOE_VMEM_LIMIT),
        name="moe_sorted",
    )(src, g_lo, g_hi, xa, wgu, wd, lg, lb)


def _group_sort_plan(xa):
    n = xa.shape[0]
    grp = xa[:, D_MODEL + GROUP_LANE].astype(jnp.int32)
    onehot = (grp[:, None] == jnp.arange(N_EXPERT_GROUPS, dtype=jnp.int32)[None, :]).astype(jnp.int32)
    csum = jnp.cumsum(onehot, axis=0)
    ends = jnp.cumsum(csum[-1])
    dest = (ends - csum[-1])[grp] + jnp.take_along_axis(csum, grp[:, None], axis=1)[:, 0] - 1
    src = jnp.zeros((n,), jnp.int32).at[dest].set(jnp.arange(n, dtype=jnp.int32))
    first_row = jnp.arange(n // MOE_TS, dtype=jnp.int32) * MOE_TS
    group_of_row = lambda rows: jnp.sum(ends[None, :] <= rows[:, None], axis=1).astype(jnp.int32)
    return src, group_of_row(first_row), group_of_row(first_row + (MOE_TS - 1))


def _nsa_proj_kernel(x_ref, wr_ref, wp_ref, wgt_ref, c_ref, s1_ref, s2_ref,
                     q_ref, kst_ref, kw_ref, kc_ref, vc_ref, vs_ref, vw_ref, gt_ref):
    xb = x_ref[...].astype(BF16)
    c, s1, s2 = c_ref[...], s1_ref[...], s2_ref[...]
    d = NSA_HEAD_DIM
    nq = NSA_HEADS * d // LANES
    ng = NSA_GROUPS * d // LANES
    first_head = lax.broadcasted_iota(jnp.int32, (x_ref.shape[0], LANES), 1) < d

    def put_rows(ref, j, t):
        ref[0, 2 * j] = t[:, :d].astype(ref.dtype)
        ref[0, 2 * j + 1] = t[:, d:].astype(ref.dtype)

    def put_with_ones(ref, j, t):
        ref[0, 2 * j] = jnp.where(first_head, t, 1.0).astype(ref.dtype)
        ref[0, 2 * j + 1] = jnp.where(first_head, pltpu.roll(t, d, axis=1), 1.0).astype(ref.dtype)

    for j in range(nq + 2 * ng):
        t = _rope128(_bdot(xb, wr_ref[:, j * LANES:(j + 1) * LANES]), c, s1, s2)
        if j < nq:
            put_rows(q_ref, j, t * (d ** -0.5 * LOG2E))
        elif j < nq + ng:
            tt = t.T.astype(BF16)
            kst_ref[0, 2 * (j - nq), 0] = tt[:d]
            kst_ref[0, 2 * (j - nq) + 1, 0] = tt[d:]
        else:
            put_rows(kw_ref, j - nq - ng, t)
    for idx, (ref, put) in enumerate(((kc_ref, put_rows), (vc_ref, put_rows),
                                      (vs_ref, put_with_ones), (vw_ref, put_with_ones))):
        for j in range(ng):
            col = (idx * ng + j) * LANES
            put(ref, j, _bdot(xb, wp_ref[:, col:col + LANES]))
    gt_ref[...] = jax.nn.sigmoid(_bdot(xb, wgt_ref[...]))


def _nsa_proj(x2, w_rope, w_plain, w_gates, rc, rs1, rs2, seq):
    n = x2.shape[0]
    tm = PROJ_TM
    gd = NSA_GROUPS * NSA_HEAD_DIM
    row = lambda i: (i, 0)
    full = lambda i: (0, 0)
    per = seq // tm
    pos = lambda i: (i % per, 0)
    bsz, g, d = n // seq, NSA_GROUPS, NSA_HEAD_DIM
    sub = ATT_TK // tm
    hmap = lambda i: (i // per, 0, i % per, 0)

    def rows_out(heads, width, dtype):
        return pl.BlockSpec((1, heads, tm, width), hmap), jax.ShapeDtypeStruct((bsz, heads, seq, width), dtype)

    outs = [rows_out(NSA_HEADS, d, BF16),
            (pl.BlockSpec((1, g, 1, d, tm), lambda i: (i // per, 0, (i % per) // sub, 0, (i % per) % sub)),
             jax.ShapeDtypeStruct((bsz, g, seq // ATT_TK, d, ATT_TK), BF16)),
            rows_out(g, d, BF16),
            rows_out(g, d, F32), rows_out(g, d, F32),
            rows_out(g, 2 * d, BF16), rows_out(g, 2 * d, BF16),
            (pl.BlockSpec((tm, g * LANES), row), jax.ShapeDtypeStruct((n, g * LANES), F32))]
    return pl.pallas_call(
        _nsa_proj_kernel,
        grid=(n // tm,),
        in_specs=[pl.BlockSpec((tm, D_MODEL), row),
                  pl.BlockSpec(w_rope.shape, full), pl.BlockSpec(w_plain.shape, full),
                  pl.BlockSpec(w_gates.shape, full),
                  pl.BlockSpec((tm, LANES), pos), pl.BlockSpec((tm, LANES), pos), pl.BlockSpec((tm, LANES), pos)],
        out_specs=[o[0] for o in outs],
        out_shape=[o[1] for o in outs],
        compiler_params=_cparams(("arbitrary",)),
        name="nsa_proj",
    )(x2, w_rope, w_plain, w_gates, rc, rs1, rs2)


def _compress_kernel(ck_ref, cv_ref, pet_ref, peb_ref, wk1_ref, wk2_ref, wv1_ref, wv2_ref,
                     c_ref, s1_ref, s2_ref, ko_ref, vo_ref):
    nch = ck_ref.shape[2]
    half = CMP_STRIDE * NSA_HEAD_DIM
    pet, peb = pet_ref[...], peb_ref[...]

    def mlp(ch, w1_ref, w2_ref):
        a = _bdot((ch + pet).astype(BF16), w1_ref[0:half, :])
        bm = _bdot((ch + peb).astype(BF16), w1_ref[half:2 * half, :])
        h = _silu(a + pltpu.roll(bm, nch - 1, axis=0))
        return _bdot(h.astype(BF16), w2_ref[...])

    kc = _rope128(mlp(ck_ref[0, 0], wk1_ref, wk2_ref), c_ref[...], s1_ref[...], s2_ref[...])
    ko_ref[0, 0] = kc[:, :NSA_HEAD_DIM].astype(BF16)
    vo_ref[0, 0] = mlp(cv_ref[0, 0], wv1_ref, wv2_ref)[:, :NSA_HEAD_DIM].astype(BF16)


def _compress(ck, cv, pet, peb, wk1, wk2, wv1, wv2, cc, cs1, cs2):
    bsz, g, nch, width = ck.shape
    blk = lambda b, gi: (b, gi, 0, 0)
    full = lambda b, gi: (0, 0)
    return pl.pallas_call(
        _compress_kernel,
        grid=(bsz, g),
        in_specs=[pl.BlockSpec((1, 1, nch, width), blk), pl.BlockSpec((1, 1, nch, width), blk),
                  pl.BlockSpec(pet.shape, full), pl.BlockSpec(peb.shape, full),
                  pl.BlockSpec(wk1.shape, full), pl.BlockSpec(wk2.shape, full),
                  pl.BlockSpec(wv1.shape, full), pl.BlockSpec(wv2.shape, full),
                  pl.BlockSpec(cc.shape, full), pl.BlockSpec(cs1.shape, full), pl.BlockSpec(cs2.shape, full)],
        out_specs=[pl.BlockSpec((1, 1, nch, NSA_HEAD_DIM), blk), pl.BlockSpec((1, 1, nch, NSA_HEAD_DIM), blk)],
        out_shape=[jax.ShapeDtypeStruct((bsz, g, nch, NSA_HEAD_DIM), BF16)] * 2,
        compiler_params=_cparams(("arbitrary", "arbitrary")),
        name="nsa_compress",
    )(ck, cv, pet, peb, wk1, wk2, wv1, wv2, cc, cs1, cs2)


def _cmp_select_kernel(q_ref, kc_ref, vc_ref, wimp_ref, o_ref, sel_ref):
    tq = q_ref.shape[2]
    nch = kc_ref.shape[2]
    ns = wimp_ref.shape[0]
    s0 = pl.program_id(2) * tq
    ratio = SLC_BLOCK // CMP_STRIDE
    n_vis = (s0 + tq - CMP_BLOCK) // CMP_STRIDE + 1
    n_chunks = (n_vis + CMP_CW - 1) // CMP_CW

    def body(w):
        nb = w // ratio
        qpos_c = s0 + lax.broadcasted_iota(jnp.int32, (tq, 1), 0)
        cend = lax.broadcasted_iota(jnp.int32, (1, w), 1) * CMP_STRIDE + (CMP_BLOCK - 1)
        cmask = cend <= qpos_c
        anyvis = (qpos_c >= CMP_BLOCK - 1).astype(F32)
        hpg, d = q_ref.shape[1], q_ref.shape[3]
        s = _dot_nt(q_ref[0].reshape(hpg * tq, d), kc_ref[0, 0, :w, :]).reshape(hpg, tq, w)
        s = jnp.where(cmask[None], s, NEG)
        e = jnp.exp2(s - jnp.max(s, axis=-1, keepdims=True))
        p = e * (anyvis[None] / jnp.sum(e, axis=-1, keepdims=True))
        o_ref[0] = _bdot(p.reshape(hpg * tq, w).astype(BF16), vc_ref[0, 0, :w, :]).reshape(hpg, tq, d)
        pg = jnp.sum(p, axis=0)

        pg_hi = pg.astype(BF16)
        pg_lo = (pg - pg_hi.astype(F32)).astype(BF16)
        wimp = wimp_ref[:nb, :w]
        imp = _dot_nt(wimp, pg_hi) + _dot_nt(wimp, pg_lo)
        blk = lax.broadcasted_iota(jnp.int32, (nb, tq), 0)
        cur = (s0 + lax.broadcasted_iota(jnp.int32, (nb, tq), 1)) >> SLC_SHIFT
        forced = (blk == 0) | (blk == cur) | (blk == cur - 1)
        imp = jnp.where(blk <= cur, jnp.where(forced, BIG, imp), NEG)
        parts = [slice(i, i + LANES) for i in range(0, tq, LANES)]
        blk_f = blk[:, :LANES].astype(F32)
        imps = [imp[:, pt] for pt in parts]
        sels = [jnp.zeros((nb, LANES), F32) for _ in parts]
        for _ in range(min(SLC_TOP_N, nb)):
            for i in range(len(parts)):
                m = jnp.max(imps[i], axis=0, keepdims=True)
                first = jnp.min(jnp.where(imps[i] == m, blk_f, float(nb)), axis=0, keepdims=True)
                hit = blk_f == first
                sels[i] = jnp.where(hit, 1.0, sels[i])
                imps[i] = jnp.where(hit, -jnp.inf, imps[i])
        sel = jnp.concatenate(sels, axis=1)
        if nb < ns:
            sel = jnp.concatenate([sel, jnp.zeros((ns - nb, tq), F32)], axis=0)
        sel_ref[0, 0] = jnp.where(sel.T > 0.0, 0.0, NEG).astype(BF16)

    for nv in range(1, nch // CMP_CW + 1):
        pl.when(n_chunks == nv)(functools.partial(body, nv * CMP_CW))


def _cmp_select(q, k_cmp, v_cmp, wimp_t):
    bsz, _, s, d = q.shape
    g = NSA_GROUPS
    nch = k_cmp.shape[2]
    ns = wimp_t.shape[0]
    tq = min(CMP_TQ, s)
    hmap = lambda b, gi, qi: (b, gi, qi, 0)
    return pl.pallas_call(
        _cmp_select_kernel,
        grid=(bsz, g, s // tq),
        in_specs=[pl.BlockSpec((1, NSA_HPG, tq, d), hmap),
                  pl.BlockSpec((1, 1, nch, d), lambda b, gi, qi: (b, gi, 0, 0)),
                  pl.BlockSpec((1, 1, nch, d), lambda b, gi, qi: (b, gi, 0, 0)),
                  pl.BlockSpec(wimp_t.shape, lambda b, gi, qi: (0, 0))],
        out_specs=[pl.BlockSpec((1, NSA_HPG, tq, d), hmap),
                   pl.BlockSpec((1, 1, tq, ns), hmap)],
        out_shape=[jax.ShapeDtypeStruct((bsz, NSA_HEADS, s, d), F32),
                   jax.ShapeDtypeStruct((bsz, g, s, ns), BF16)],
        compiler_params=_cparams(("arbitrary", "arbitrary", "arbitrary")),
        name="nsa_cmp_select",
    )(q, k_cmp, v_cmp, wimp_t)


def _nsa_attn_kernel(q_ref, kst_ref, vs_ref, kw_ref, vw_ref, selb_ref, eexp_ref, ocmp_ref, gt_ref,
                     o_ref, m_ref, acc_ref, s_ref, p_ref, a_ref):
    hpg, tq, d = q_ref.shape[1:]
    rows = hpg * tq
    s0 = pl.program_id(2) * tq
    q4 = q_ref[0].reshape(rows, d)
    q4a = jnp.concatenate([jnp.concatenate([selb_ref[0, 0]] * hpg, axis=0), q4], axis=1)
    qpos = s0 + lax.broadcasted_iota(jnp.int32, (tq, 1), 0)

    def scores(t):
        return _bdot(q4a, jnp.concatenate([eexp_ref[t], kst_ref[0, 0, t]], axis=0))

    def flush(t):
        acc_ref[...] = a_ref[...] * acc_ref[...] + _bdot(p_ref[...], vs_ref[0, 0, t])

    def softmax_step(s):
        m_prev = m_ref[...]
        m_new = jnp.maximum(m_prev, jnp.max(s, axis=-1, keepdims=True))
        a_ref[...] = jnp.exp2(m_prev - m_new)
        p_ref[...] = jnp.exp2(s - m_new).astype(BF16)
        m_ref[...] = m_new

    m_ref[...] = jnp.full_like(m_ref, NEG)
    acc_ref[...] = jnp.zeros_like(acc_ref)
    p_ref[...] = jnp.zeros_like(p_ref)
    a_ref[...] = jnp.ones_like(a_ref)
    t_diag = s0 // ATT_TK
    s_ref[...] = scores(0)

    def step(t):
        s = s_ref[...]
        s_next = scores(t + 1)
        flush(jnp.maximum(t - 1, 0))
        softmax_step(s)
        s_ref[...] = s_next

    def pair_body(i, carry):
        step(2 * i)
        step(2 * i + 1)
        return carry

    lax.fori_loop(0, t_diag // 2, pair_body, 0)
    pl.when(t_diag % 2 == 1)(lambda: step(t_diag - 1))

    span = WINDOW + tq
    w0 = pl.multiple_of(jnp.maximum(s0 - WINDOW, 0), tq)
    rel = qpos - (w0 + lax.broadcasted_iota(jnp.int32, (1, span), 1))
    wbias = jnp.where((rel >= 0) & (rel < WINDOW), 0.0, NEG)
    sw = _dot_nt(q4, kw_ref[0, 0, pl.ds(w0, span), :])
    sw = (sw.reshape(hpg, tq, span) + wbias[None]).reshape(rows, span)
    pw = jnp.exp2(sw - jnp.max(sw, axis=-1, keepdims=True)).astype(BF16)
    accw = _bdot(pw, vw_ref[0, 0, pl.ds(w0, span), :])
    o_win = (accw[:, :d] / accw[:, d:d + 1]).reshape(hpg, tq, d)

    flush(jnp.maximum(t_diag - 1, 0))
    kpos = t_diag * ATT_TK + lax.broadcasted_iota(jnp.int32, (1, ATT_TK), 1)
    s = s_ref[...].reshape(hpg, tq, ATT_TK)
    softmax_step(jnp.where((kpos <= qpos)[None], s, NEG).reshape(rows, ATT_TK))
    flush(t_diag)
    acc = acc_ref[...]
    o_slc = (acc[:, :d] / acc[:, d:d + 1]).reshape(hpg, tq, d)

    gts = gt_ref[0]
    for h in range(hpg):
        g0, g1, g2 = (gts[:, 3 * h + j:3 * h + j + 1] for j in range(3))
        o_ref[0, h] = g0 * ocmp_ref[0, h] + g1 * o_slc[h] + g2 * o_win[h]


def _nsa_attn(q, kst, vst, kw, vw, selb, eexp, o_cmp, gates):
    bsz, _, s, d = q.shape
    g = NSA_GROUPS
    tq = ATT_TQ
    rows = NSA_HPG * tq
    ns = selb.shape[-1]
    hmap = lambda b, gi, qi: (b, gi, qi, 0)
    tmap = lambda b, gi, qi: (b, gi, 0, 0, 0)
    rmap = lambda b, gi, qi: (b, gi, 0, 0)
    return pl.pallas_call(
        _nsa_attn_kernel,
        grid=(bsz, g, s // tq),
        in_specs=[pl.BlockSpec((1, NSA_HPG, tq, d), hmap),
                  pl.BlockSpec((1, 1) + kst.shape[2:], tmap), pl.BlockSpec((1, 1) + vst.shape[2:], tmap),
                  pl.BlockSpec((1, 1) + kw.shape[2:], rmap), pl.BlockSpec((1, 1) + vw.shape[2:], rmap),
                  pl.BlockSpec((1, 1, tq, ns), hmap),
                  pl.BlockSpec(eexp.shape, lambda b, gi, qi: (0, 0, 0)),
                  pl.BlockSpec((1, NSA_HPG, tq, d), hmap),
                  pl.BlockSpec((1, tq, LANES), lambda b, gi, qi: (b, qi, gi))],
        out_specs=pl.BlockSpec((1, NSA_HPG, tq, d), hmap),
        out_shape=jax.ShapeDtypeStruct((bsz, NSA_HEADS, s, d), F32),
        scratch_shapes=[pltpu.VMEM((rows, 1), F32),
                        pltpu.VMEM((rows, 2 * d), F32),
                        pltpu.VMEM((rows, ATT_TK), F32),
                        pltpu.VMEM((rows, ATT_TK), BF16),
                        pltpu.VMEM((rows, 1), F32)],
        compiler_params=_cparams(("arbitrary", "arbitrary", "arbitrary")),
        name="nsa_attn",
    )(q, kst, vst, kw, vw, selb, eexp, o_cmp, gates)


def _rope_tables(pos):
    inv_freq = jnp.power(ROPE_THETA, -jnp.arange(ROT_HALF, dtype=F32) * (2.0 / ROT_DIM))
    ang = pos.astype(F32)[:, None] * inv_freq[None, :]
    cos, sin = jnp.cos(ang), jnp.sin(ang)
    n = pos.shape[0]
    one = jnp.ones((n, NSA_HEAD_DIM - ROT_DIM), F32)
    zero8 = jnp.zeros((n, ROT_HALF), F32)
    zero = jnp.zeros((n, NSA_HEAD_DIM - ROT_DIM), F32)
    c = jnp.concatenate([cos, cos, one], axis=1)
    s1 = jnp.concatenate([-sin, zero8, zero], axis=1)
    s2 = jnp.concatenate([zero8, sin, zero], axis=1)
    rep = LANES // NSA_HEAD_DIM
    return tuple(jnp.tile(t, (1, rep)) for t in (c, s1, s2))


def _importance_weights(ns, nch):
    ratio = SLC_BLOCK // CMP_STRIDE
    span = CMP_BLOCK // CMP_STRIDE
    w = np.zeros((ns, nch), np.float32)
    for j in range(ns):
        for m in range(ratio):
            for n in range(span):
                c = ratio * j + m + n
                if c < nch - 1:
                    w[j, c] += 1.0
    return jnp.asarray(w, dtype=BF16)


def _block_expansion(nt, ns):
    key_blk = (np.arange(nt)[:, None] * ATT_TK + np.arange(ATT_TK)[None, :]) // SLC_BLOCK
    e = (np.arange(ns)[None, :, None] == key_blk[:, None, :]).astype(np.float32)
    return jnp.asarray(e, dtype=BF16)


def _gla_layer(x2, bsz, seq, w_in, w_gate_up, b_gate, norm_g, w_out, ln_g, ln_b, router):
    hk = GLA_HEADS * GLA_DK
    hv = GLA_HEADS * GLA_DV
    cuts = np.cumsum([hk, hk, hv, GLA_GATE_RANK]).tolist()
    wq, wk, wv, wg, wr = jnp.split(w_in, cuts, axis=1)
    w_main = jnp.concatenate([wq, wk, wv, wr], axis=1).astype(BF16)
    w_glow = jnp.pad(wg, ((0, 0), (0, LANES - GLA_GATE_RANK))).astype(BF16)
    w_gu = jnp.pad(w_gate_up, ((0, LANES - GLA_GATE_RANK), (0, 0))).astype(BF16)
    q, k, v, r, la = _gla_proj(x2, w_main, w_glow, w_gu, b_gate.reshape(1, hk))
    sh = lambda t: t.reshape(bsz, seq, t.shape[-1])
    o = _gla_core(sh(q), sh(k), sh(v), sh(la), sh(r), norm_g.reshape(1, GLA_DV))
    return _outproj_ln(o.reshape(bsz * seq, hv), x2, w_out.astype(BF16),
                       ln_g.reshape(1, D_MODEL), ln_b.reshape(1, D_MODEL), router)


def _nsa_layer(x2, bsz, seq, w_in, w_ck1, w_ck2, w_cv1, w_cv2, cmp_pe, w_out, ln_g, ln_b, router):
    h, g, hpg, d = NSA_HEADS, NSA_GROUPS, NSA_HPG, NSA_HEAD_DIM
    gd = g * d
    cuts = np.cumsum([h * d] + [gd] * 6).tolist()
    wq, wkc, wvc, wks, wvs, wkw, wvw, wgt = jnp.split(w_in, cuts, axis=1)
    w_rope = jnp.concatenate([wq, wks, wkw], axis=1).astype(BF16)
    w_plain = jnp.concatenate([wkc, wvc, wvs, wvw], axis=1).astype(BF16)
    w_gates = jnp.pad(wgt.reshape(D_MODEL, g, hpg * 3), ((0, 0), (0, 0), (0, LANES - hpg * 3)))
    w_gates = w_gates.reshape(D_MODEL, g * LANES).astype(BF16)
    rc, rs1, rs2 = _rope_tables(jnp.arange(seq, dtype=jnp.int32))
    q, kst, kw, kc, vc, vs1, vw1, gates = _nsa_proj(x2, w_rope, w_plain, w_gates, rc, rs1, rs2, seq)

    nch = seq // CMP_STRIDE
    chunks = lambda t: t.reshape(bsz, g, nch, CMP_STRIDE * d)
    pe = cmp_pe.reshape(1, CMP_BLOCK * d)
    half = CMP_STRIDE * d
    pad2 = lambda w: jnp.pad(w, ((0, 0), (0, LANES - d))).astype(BF16)
    cc, cs1, cs2 = _rope_tables(jnp.arange(nch, dtype=jnp.int32) * CMP_STRIDE + (CMP_BLOCK - 1))
    k_cmp, v_cmp = _compress(chunks(kc), chunks(vc), pe[:, :half], pe[:, half:],
                             w_ck1.astype(BF16), pad2(w_ck2), w_cv1.astype(BF16), pad2(w_cv2), cc, cs1, cs2)

    ns = seq // SLC_BLOCK
    sh = lambda t: t.reshape(bsz, seq, t.shape[-1])
    o_cmp, selb = _cmp_select(q, k_cmp, v_cmp, _importance_weights(ns, nch))

    nt = seq // ATT_TK
    o = _nsa_attn(q, kst, vs1.reshape(bsz, g, nt, ATT_TK, 2 * d), kw, vw1,
                  selb, _block_expansion(nt, ns), o_cmp, sh(gates))
    return _outproj_heads_ln(o, x2, w_out.astype(BF16), ln_g.reshape(1, D_MODEL), ln_b.reshape(1, D_MODEL), router)


def _moe_layer(xa, w_gate, w_up, w_down, ln_g, ln_b):
    ng, epg, f = N_EXPERT_GROUPS, EXPERTS_PER_GROUP, D_FF_EXPERT

    def by_group(w):
        return w.astype(BF16).reshape(ng, epg, D_MODEL, f).transpose(0, 2, 1, 3).reshape(ng, D_MODEL, epg * f)

    wgu = jnp.concatenate([by_group(w_gate), by_group(w_up)], axis=-1)
    wd = w_down.astype(BF16).reshape(ng, epg * f, D_MODEL)
    src, g_lo, g_hi = _group_sort_plan(xa)
    return _moe_sorted(xa, src, g_lo, g_hi, wgu, wd, ln_g.reshape(1, D_MODEL), ln_b.reshape(1, D_MODEL))


def kernel(x, gla_w_in, gla_w_gate_up, gla_b_gate, gla_norm_g, gla_w_out, nsa_w_in, nsa_w_cmp_k1, nsa_w_cmp_k2,
           nsa_w_cmp_v1, nsa_w_cmp_v2, nsa_cmp_pe, nsa_w_out, moe_w_router, moe_b_router, moe_w_gate, moe_w_up,
           moe_w_down, ln_g, ln_b):
    bsz, seq, _ = x.shape
    x2 = x.reshape(bsz * seq, D_MODEL)
    w_router_t = moe_w_router.T
    wr_hi = w_router_t.astype(BF16)
    wr_lo = (w_router_t - wr_hi.astype(F32)).astype(BF16)
    router = (wr_hi, wr_lo, moe_b_router.reshape(N_EXPERTS, 1))
    for i in range(DEPTH):
        j = i // 2
        if i % 2 == 0:
            xa = _gla_layer(x2, bsz, seq, gla_w_in[j], gla_w_gate_up[j], gla_b_gate[j], gla_norm_g[j],
                            gla_w_out[j], ln_g[i, 0], ln_b[i, 0], router)
        else:
            xa = _nsa_layer(x2, bsz, seq, nsa_w_in[j], nsa_w_cmp_k1[j], nsa_w_cmp_k2[j], nsa_w_cmp_v1[j],
                            nsa_w_cmp_v2[j], nsa_cmp_pe[j], nsa_w_out[j], ln_g[i, 0], ln_b[i, 0], router)
        x2 = _moe_layer(xa, moe_w_gate[i], moe_w_up[i], moe_w_down[i], ln_g[i, 1], ln_b[i, 1])
    return x2.reshape(bsz, seq, D_MODEL)
```

```python
import functools

import numpy as np
import jax
import jax.numpy as jnp
from jax import lax
from jax.experimental import pallas as pl
from jax.experimental.pallas import tpu as pltpu

F32 = jnp.float32
BF16 = jnp.bfloat16
HIGHEST = lax.Precision.HIGHEST

D_MODEL = 1024
DEPTH = 2

GLA_HEADS = 4
GLA_DK = D_MODEL // 2 // GLA_HEADS
GLA_DV = D_MODEL // GLA_HEADS
GLA_GATE_RANK = 16
GLA_TAU = 16.0
GLA_CHUNK = 64

NSA_HEADS = 16
NSA_GROUPS = 4
NSA_HPG = NSA_HEADS // NSA_GROUPS
NSA_HEAD_DIM = D_MODEL // NSA_HEADS
CMP_BLOCK = 32
CMP_STRIDE = 16
CMP_HIDDEN = 256
SLC_BLOCK = 64
SLC_SHIFT = SLC_BLOCK.bit_length() - 1
SLC_TOP_N = 16
WINDOW = 512

ROPE_THETA = 500000.0
ROT_DIM = NSA_HEAD_DIM // 4
ROT_HALF = ROT_DIM // 2

N_EXPERTS = 16
N_EXPERT_GROUPS = 4
EXPERTS_PER_GROUP = N_EXPERTS // N_EXPERT_GROUPS
D_FF_EXPERT = D_MODEL // 4

DN_ALPHA = (2 * DEPTH) ** 0.25
LN_EPS = 1e-5
NEG = -1e30
BIG = 1e30
LOG2E = 1.4426950408889634

LANES = 128
VMEM_LIMIT = 48 * 1024 * 1024

PROJ_TM = 256
OUT_TM = 512
GLA_T = 512
MOE_TS = 512
MOE_VMEM_LIMIT = 56 * 1024 * 1024
XA_WIDTH = D_MODEL + LANES
CMP_TQ = 256
CMP_CW = 128
ATT_TQ = 128
ATT_TK = 512
WIN_TK = 128


def _cparams(sem):
    return pltpu.CompilerParams(dimension_semantics=sem, vmem_limit_bytes=VMEM_LIMIT)


def _bdot(a, b):
    return jnp.dot(a, b, preferred_element_type=F32)


def _dot_nt(a, b, precision=None):
    return lax.dot_general(a, b, (((1,), (1,)), ((), ())), preferred_element_type=F32, precision=precision)


def _dot_tn(a, b):
    return lax.dot_general(a, b, (((0,), (0,)), ((), ())), preferred_element_type=F32)


def _layer_norm(z, g, b):
    mu = jnp.mean(z, axis=-1, keepdims=True)
    zc = z - mu
    var = jnp.mean(zc * zc, axis=-1, keepdims=True)
    return zc * lax.rsqrt(var + LN_EPS) * g + b


def _silu(t):
    return t * (0.5 * jnp.tanh(0.5 * t) + 0.5)


def _rope128(t, c, s1, s2):
    up = pltpu.roll(t, LANES - ROT_HALF, axis=1)
    dn = pltpu.roll(t, ROT_HALF, axis=1)
    return t * c + up * s1 + dn * s2


def _gla_proj_kernel(x_ref, wm_ref, wg_ref, wgu_ref, bg_ref, q_ref, k_ref, v_ref, r_ref, la_ref):
    xb = x_ref[...].astype(BF16)
    hk = GLA_HEADS * GLA_DK
    hv = GLA_HEADS * GLA_DV
    q_ref[...] = _bdot(xb, wm_ref[:, 0:hk])
    k_ref[...] = _bdot(xb, wm_ref[:, hk:2 * hk])
    v_ref[...] = _bdot(xb, wm_ref[:, 2 * hk:2 * hk + hv])
    r_ref[...] = _bdot(xb, wm_ref[:, 2 * hk + hv:2 * hk + 2 * hv])
    g_low = _bdot(xb, wg_ref[...])
    z = _bdot(g_low.astype(BF16), wgu_ref[...]) + bg_ref[...]
    log_sig = jnp.minimum(z, 0.0) - jnp.log1p(jnp.exp(-jnp.abs(z)))
    la_ref[...] = log_sig * (1.0 / GLA_TAU)


def _gla_proj(x2, w_main, w_glow, w_gu, b_gate):
    n = x2.shape[0]
    hk = GLA_HEADS * GLA_DK
    hv = GLA_HEADS * GLA_DV
    tm = PROJ_TM
    row = lambda i: (i, 0)
    full = lambda i: (0, 0)
    return pl.pallas_call(
        _gla_proj_kernel,
        grid=(n // tm,),
        in_specs=[pl.BlockSpec((tm, D_MODEL), row),
                  pl.BlockSpec(w_main.shape, full),
                  pl.BlockSpec(w_glow.shape, full),
                  pl.BlockSpec(w_gu.shape, full),
                  pl.BlockSpec(b_gate.shape, full)],
        out_specs=[pl.BlockSpec((tm, hk), row), pl.BlockSpec((tm, hk), row),
                   pl.BlockSpec((tm, hv), row), pl.BlockSpec((tm, hv), row),
                   pl.BlockSpec((tm, hk), row)],
        out_shape=[jax.ShapeDtypeStruct((n, hk), F32), jax.ShapeDtypeStruct((n, hk), F32),
                   jax.ShapeDtypeStruct((n, hv), F32), jax.ShapeDtypeStruct((n, hv), F32),
                   jax.ShapeDtypeStruct((n, hk), F32)],
        compiler_params=_cparams(("arbitrary",)),
        name="gla_proj",
    )(x2, w_main, w_glow, w_gu, b_gate)


def _dot_01(m, x):
    x1 = x.astype(BF16)
    r1 = x - x1.astype(F32)
    x2 = r1.astype(BF16)
    x3 = (r1 - x2.astype(F32)).astype(BF16)
    return _bdot(m, x1) + _bdot(m, x2) + _bdot(m, x3)


def _gla_core_kernel(q_ref, k_ref, v_ref, la_ref, r_ref, g_ref, o_ref, st_ref):
    @pl.when(pl.program_id(2) == 0)
    def _():
        st_ref[...] = jnp.zeros_like(st_ref)

    c, t = GLA_CHUNK, GLA_T
    shift = c.bit_length() - 1
    row = lax.broadcasted_iota(jnp.int32, (t, t), 0)
    col = lax.broadcasted_iota(jnp.int32, (t, t), 1)
    causal = ((row >> shift) == (col >> shift)) & (row >= col)
    in_chunk = lax.broadcasted_iota(jnp.int32, (t, GLA_DK), 0) & (c - 1)
    b = la_ref[0]
    step = 1
    while step < c:
        b = b + jnp.where(in_chunk >= step, pltpu.roll(b, step, axis=0), 0.0)
        step *= 2
    b_last = jnp.broadcast_to(b.reshape(t // c, c, GLA_DK)[:, c - 1:c, :], (t // c, c, GLA_DK)).reshape(t, GLA_DK)
    q = q_ref[0] * (GLA_DK ** -0.5)
    k = k_ref[0]
    vb = v_ref[0].astype(BF16)
    q_dec = (q * jnp.exp(b)).astype(BF16)
    k_neg = (k * jnp.exp(-b)).astype(BF16)
    k_dec = (k * jnp.exp(b_last - b)).astype(BF16)
    decay = jnp.exp(b_last)
    s = jnp.where(causal, _dot_nt(q_dec, k_neg), 0.0)
    o = _bdot(s.astype(BF16), vb)

    chunks = [slice(ci * c, (ci + 1) * c) for ci in range(t // c)]
    kv = [_dot_tn(vb[rows], k_dec[rows]) for rows in chunks]
    st = st_ref[...]
    entering = []
    for ci, rows in enumerate(chunks):
        entering.append(st.astype(BF16))
        st = decay[ci * c:ci * c + 1] * st + kv[ci]
    st_ref[...] = st
    o = o + jnp.concatenate([_dot_nt(q_dec[rows], s_in) for rows, s_in in zip(chunks, entering)], axis=0)
    ms = jnp.mean(o * o, axis=-1, keepdims=True)
    o = o * lax.rsqrt(ms + LN_EPS) * g_ref[...]
    o_ref[0] = o * _silu(r_ref[0])


def _gla_core(q, k, v, la, r, norm_g):
    bsz, s, _ = q.shape
    t = GLA_T
    kmap = lambda b, h, n: (b, n, h)
    return pl.pallas_call(
        _gla_core_kernel,
        grid=(bsz, GLA_HEADS, s // t),
        in_specs=[pl.BlockSpec((1, t, GLA_DK), kmap), pl.BlockSpec((1, t, GLA_DK), kmap),
                  pl.BlockSpec((1, t, GLA_DV), kmap), pl.BlockSpec((1, t, GLA_DK), kmap),
                  pl.BlockSpec((1, t, GLA_DV), kmap),
                  pl.BlockSpec((1, GLA_DV), lambda b, h, n: (0, 0))],
        out_specs=pl.BlockSpec((1, t, GLA_DV), kmap),
        out_shape=jax.ShapeDtypeStruct((bsz, s, GLA_HEADS * GLA_DV), F32),
        scratch_shapes=[pltpu.VMEM((GLA_DV, GLA_DK), F32)],
        compiler_params=_cparams(("arbitrary", "arbitrary", "arbitrary")),
        name="gla_core",
    )(q, k, v, la, r, norm_g)


def _row_xor(v, k, row):
    n = v.shape[0]
    up = pltpu.roll(v, n - k, axis=0)
    dn = pltpu.roll(v, k, axis=0)
    return jnp.where((row & k) == 0, up, dn)


def _route(x, wt_hi, wt_lo, bias):
    x_hi = x.astype(BF16)
    x_lo = (x - x_hi.astype(F32)).astype(BF16)
    logits = _dot_nt(wt_hi, x_hi) + (_dot_nt(wt_lo, x_hi) + _dot_nt(wt_hi, x_lo))
    s = jax.nn.sigmoid(logits)
    a = s + bias
    row = lax.broadcasted_iota(jnp.int32, a.shape, 0)
    a1 = _row_xor(a, 1, row)
    a2 = _row_xor(a, 2, row)
    a3 = _row_xor(a1, 2, row)
    p, q = jnp.maximum(a, a1), jnp.minimum(a, a1)
    r, t = jnp.maximum(a2, a3), jnp.minimum(a2, a3)
    gs = jnp.maximum(p, r) + jnp.maximum(jnp.minimum(p, r), jnp.maximum(q, t))
    g1 = _row_xor(gs, 4, row)
    g2 = _row_xor(gs, 8, row)
    g3 = _row_xor(g1, 8, row)

    def beats(other, other_first):
        return (other > gs) | ((other == gs) & other_first)

    lose = (beats(g1, (row & 4) != 0) | beats(g2, (row & 8) != 0) | beats(g3, (row & 8) != 0))

    def ahead(other, other_first):
        return jnp.where((other > a) | ((other == a) & other_first), 1.0, 0.0)

    rank = ahead(a1, (row & 1) != 0) + ahead(a2, (row & 2) != 0) + ahead(a3, (row & 2) != 0)
    sel = jnp.logical_not(lose) & (rank < 2.0)
    ssel = jnp.where(sel, s, 0.0)
    s1 = _row_xor(ssel, 1, row)
    tot = (ssel + s1) + (_row_xor(ssel, 2, row) + _row_xor(s1, 2, row))
    gate = jnp.where(sel, s / tot, 0.0)
    group = jnp.max(jnp.where(sel, (row >> 2).astype(F32), 0.0), axis=0, keepdims=True)
    tokens = x.shape[0]
    record = jnp.concatenate([gate, jnp.zeros((LANES - N_EXPERTS, tokens), F32)], axis=0)
    return record.T, group.astype(jnp.int32)


def _norm_route_store(z, g_ref, b_ref, wrh_ref, wrl_ref, br_ref, o_ref, grp_ref):
    x = _layer_norm(z, g_ref[...], b_ref[...])
    o_ref[:, :D_MODEL] = x
    o_ref[:, D_MODEL:], grp_ref[...] = _route(x, wrh_ref[...], wrl_ref[...], br_ref[...])


def _outproj_ln_kernel(h_ref, x_ref, w_ref, g_ref, b_ref, wrh_ref, wrl_ref, br_ref, o_ref, grp_ref):
    y = _bdot(h_ref[...].astype(BF16), w_ref[...])
    _norm_route_store(DN_ALPHA * x_ref[...] + y, g_ref, b_ref, wrh_ref, wrl_ref, br_ref, o_ref, grp_ref)


def _outproj_heads_ln_kernel(h_ref, x_ref, w_ref, g_ref, b_ref, wrh_ref, wrl_ref, br_ref, o_ref, grp_ref):
    h = jnp.concatenate([h_ref[0, i].astype(BF16) for i in range(h_ref.shape[1])], axis=1)
    _norm_route_store(DN_ALPHA * x_ref[...] + _bdot(h, w_ref[...]), g_ref, b_ref, wrh_ref, wrl_ref, br_ref,
                      o_ref, grp_ref)


def _outproj_call(kernel, name, h, h_spec, x2, w, g, b, router):
    n = x2.shape[0]
    tm = OUT_TM
    row = lambda i: (i, 0)
    full = lambda i: (0, 0)
    return pl.pallas_call(
        kernel,
        grid=(n // tm,),
        in_specs=[h_spec, pl.BlockSpec((tm, D_MODEL), row),
                  pl.BlockSpec(w.shape, full), pl.BlockSpec((1, D_MODEL), full), pl.BlockSpec((1, D_MODEL), full)]
                 + [pl.BlockSpec(r.shape, full) for r in router],
        out_specs=[pl.BlockSpec((tm, XA_WIDTH), row), pl.BlockSpec((1, tm), lambda i: (0, i))],
        out_shape=[jax.ShapeDtypeStruct((n, XA_WIDTH), F32), jax.ShapeDtypeStruct((1, n), jnp.int32)],
        compiler_params=_cparams(("arbitrary",)),
        name=name,
    )(h, x2, w, g, b, *router)


def _outproj_heads_ln(h4, x2, w, g, b, router):
    _, heads, seq, d = h4.shape
    per = seq // OUT_TM
    spec = pl.BlockSpec((1, heads, OUT_TM, d), lambda i: (i // per, 0, i % per, 0))
    return _outproj_call(_outproj_heads_ln_kernel, "outproj_heads_ln", h4, spec, x2, w, g, b, router)


def _outproj_ln(h2, x2, w, g, b, router):
    spec = pl.BlockSpec((OUT_TM, h2.shape[1]), lambda i: (i, 0))
    return _outproj_call(_outproj_ln_kernel, "outproj_ln", h2, spec, x2, w, g, b, router)


def _moe_sorted_kernel(dest_ref, glo_ref, ghi_ref, xa_hbm, wg_ref, wu_ref, wd_ref, lg_ref, lb_ref, out_hbm,
                       src_ref, xbuf, obuf, in_sem, out_sem):
    t = pl.program_id(0)
    nt = pl.num_programs(0)
    ts = MOE_TS
    slot = t & 1

    def row_in(tile, sl, r):
        return pltpu.make_async_copy(xa_hbm.at[src_ref[tile * ts + r]], xbuf.at[sl, r], in_sem.at[sl])

    def row_out(tile, sl, r):
        return pltpu.make_async_copy(obuf.at[sl, r], out_hbm.at[src_ref[tile * ts + r]], out_sem.at[sl])

    def wait_in(sl):
        pltpu.make_async_copy(xa_hbm.at[pl.ds(0, ts)], xbuf.at[sl], in_sem.at[sl]).wait()

    def wait_out(sl):
        pltpu.make_async_copy(obuf.at[sl], out_hbm.at[pl.ds(0, ts)], out_sem.at[sl]).wait()

    @pl.when(t == 0)
    def _():
        def invert(i, carry):
            src_ref[dest_ref[i]] = i
            return carry

        lax.fori_loop(0, dest_ref.shape[0], invert, 0, unroll=8)
        for r in range(ts):
            row_in(0, 0, r).start()

    @pl.when(t >= 2)
    def _():
        wait_out(slot)

    wait_in(slot)
    nxt = jnp.minimum(t + 1, nt - 1)
    for r in range(ts):
        row_in(nxt, 1 - slot, r).start()

    xa = xbuf[slot]
    x = xa[:, :D_MODEL]
    gate = xa[:, D_MODEL:]
    xb = x.astype(BF16)
    f = D_FF_EXPERT
    epg = EXPERTS_PER_GROUP

    def group_ffn(g):
        hs = []
        for e in range(epg):
            col = gate[:, e:e + 1]
            for other in range(1, N_EXPERT_GROUPS):
                col = jnp.where(g == other, gate[:, other * epg + e:other * epg + e + 1], col)
            expert = g * epg + e
            hs.append((_silu(_bdot(xb, wg_ref[expert])) * _bdot(xb, wu_ref[expert]) * col).astype(BF16))
        return _bdot(jnp.concatenate(hs, axis=1), wd_ref[g])

    y = group_ffn(glo_ref[t])
    y = lax.fori_loop(glo_ref[t] + 1, ghi_ref[t] + 1, lambda g, acc: acc + group_ffn(g), y)
    obuf[slot] = _layer_norm(DN_ALPHA * x + y, lg_ref[...], lb_ref[...])
    for r in range(ts):
        row_out(t, slot, r).start()

    @pl.when(t == nt - 1)
    def _():
        wait_out(slot)
        wait_in(1 - slot)

    @pl.when((t == nt - 1) & (nt >= 2))
    def _():
        wait_out(1 - slot)


def _moe_sorted(xa, dest, g_lo, g_hi, wg, wu, wd, lg, lb):
    n = xa.shape[0]
    ts = MOE_TS
    resident = pl.Buffered(1)
    whole = lambda i, *_: (0, 0, 0)
    grid_spec = pltpu.PrefetchScalarGridSpec(
        num_scalar_prefetch=3,
        grid=(n // ts,),
        in_specs=[pl.BlockSpec(memory_space=pl.ANY),
                  pl.BlockSpec(wg.shape, whole, pipeline_mode=resident),
                  pl.BlockSpec(wu.shape, whole, pipeline_mode=resident),
                  pl.BlockSpec(wd.shape, whole, pipeline_mode=resident),
                  pl.BlockSpec((1, D_MODEL), lambda i, *_: (0, 0)),
                  pl.BlockSpec((1, D_MODEL), lambda i, *_: (0, 0))],
        out_specs=pl.BlockSpec(memory_space=pl.ANY),
        scratch_shapes=[pltpu.SMEM((n,), jnp.int32),
                        pltpu.VMEM((2, ts, XA_WIDTH), F32), pltpu.VMEM((2, ts, D_MODEL), F32),
                        pltpu.SemaphoreType.DMA((2,)), pltpu.SemaphoreType.DMA((2,))])
    return pl.pallas_call(
        _moe_sorted_kernel,
        grid_spec=grid_spec,
        out_shape=jax.ShapeDtypeStruct((n, D_MODEL), F32),
        compiler_params=pltpu.CompilerParams(dimension_semantics=("arbitrary",), vmem_limit_bytes=MOE_VMEM_LIMIT),
        name="moe_sorted",
    )(dest, g_lo, g_hi, xa, wg, wu, wd, lg, lb)


def _group_sort_plan(grp):
    n = grp.shape[1]
    onehot = (grp == jnp.arange(N_EXPERT_GROUPS, dtype=jnp.int32)[:, None]).astype(jnp.int32)
    csum = jnp.cumsum(onehot, axis=1)
    ends = jnp.cumsum(csum[:, -1])
    dest = jnp.sum(onehot * (csum - 1 + (ends - csum[:, -1])[:, None]), axis=0)
    first_row = jnp.arange(n // MOE_TS, dtype=jnp.int32) * MOE_TS
    group_of_row = lambda rows: jnp.sum(ends[None, :] <= rows[:, None], axis=1).astype(jnp.int32)
    return dest, group_of_row(first_row), group_of_row(first_row + (MOE_TS - 1))


def _nsa_proj_kernel(x_ref, wr_ref, wp_ref, wgt_ref, c_ref, s1_ref, s2_ref,
                     q_ref, kst_ref, kw_ref, kc_ref, vc_ref, vs_ref, vw_ref, gt_ref):
    xb = x_ref[...].astype(BF16)
    c, s1, s2 = c_ref[...], s1_ref[...], s2_ref[...]
    d = NSA_HEAD_DIM
    nq = NSA_HEADS * d // LANES
    ng = NSA_GROUPS * d // LANES
    first_head = lax.broadcasted_iota(jnp.int32, (x_ref.shape[0], LANES), 1) < d

    def put_rows(ref, j, t):
        ref[0, 2 * j] = t[:, :d].astype(ref.dtype)
        ref[0, 2 * j + 1] = t[:, d:].astype(ref.dtype)

    def put_with_ones(ref, j, t):
        ref[0, 2 * j] = jnp.where(first_head, t, 1.0).astype(ref.dtype)
        ref[0, 2 * j + 1] = jnp.where(first_head, pltpu.roll(t, d, axis=1), 1.0).astype(ref.dtype)

    for j in range(nq + 2 * ng):
        t = _rope128(_bdot(xb, wr_ref[:, j * LANES:(j + 1) * LANES]), c, s1, s2)
        if j < nq:
            put_rows(q_ref, j, t * (d ** -0.5 * LOG2E))
        elif j < nq + ng:
            tt = t.T.astype(BF16)
            kst_ref[0, 2 * (j - nq), 0] = tt[:d]
            kst_ref[0, 2 * (j - nq) + 1, 0] = tt[d:]
        else:
            put_rows(kw_ref, j - nq - ng, t)
    for idx, (ref, put) in enumerate(((kc_ref, put_rows), (vc_ref, put_rows),
                                      (vs_ref, put_with_ones), (vw_ref, put_with_ones))):
        for j in range(ng):
            col = (idx * ng + j) * LANES
            put(ref, j, _bdot(xb, wp_ref[:, col:col + LANES]))
    gt_ref[...] = jax.nn.sigmoid(_bdot(xb, wgt_ref[...]))


def _nsa_proj(x2, w_rope, w_plain, w_gates, rc, rs1, rs2, seq):
    n = x2.shape[0]
    tm = PROJ_TM
    gd = NSA_GROUPS * NSA_HEAD_DIM
    row = lambda i: (i, 0)
    full = lambda i: (0, 0)
    per = seq // tm
    pos = lambda i: (i % per, 0)
    bsz, g, d = n // seq, NSA_GROUPS, NSA_HEAD_DIM
    sub = ATT_TK // tm
    hmap = lambda i: (i // per, 0, i % per, 0)

    def rows_out(heads, width, dtype):
        return pl.BlockSpec((1, heads, tm, width), hmap), jax.ShapeDtypeStruct((bsz, heads, seq, width), dtype)

    outs = [rows_out(NSA_HEADS, d, BF16),
            (pl.BlockSpec((1, g, 1, d, tm), lambda i: (i // per, 0, (i % per) // sub, 0, (i % per) % sub)),
             jax.ShapeDtypeStruct((bsz, g, seq // ATT_TK, d, ATT_TK), BF16)),
            rows_out(g, d, BF16),
            rows_out(g, d, F32), rows_out(g, d, F32),
            rows_out(g, 2 * d, BF16), rows_out(g, 2 * d, BF16),
            (pl.BlockSpec((tm, g * LANES), row), jax.ShapeDtypeStruct((n, g * LANES), F32))]
    return pl.pallas_call(
        _nsa_proj_kernel,
        grid=(n // tm,),
        in_specs=[pl.BlockSpec((tm, D_MODEL), row),
                  pl.BlockSpec(w_rope.shape, full), pl.BlockSpec(w_plain.shape, full),
                  pl.BlockSpec(w_gates.shape, full),
                  pl.BlockSpec((tm, LANES), pos), pl.BlockSpec((tm, LANES), pos), pl.BlockSpec((tm, LANES), pos)],
        out_specs=[o[0] for o in outs],
        out_shape=[o[1] for o in outs],
        compiler_params=_cparams(("arbitrary",)),
        name="nsa_proj",
    )(x2, w_rope, w_plain, w_gates, rc, rs1, rs2)


def _compress_kernel(ck_ref, cv_ref, pet_ref, peb_ref, wk1_ref, wk2_ref, wv1_ref, wv2_ref,
                     c_ref, s1_ref, s2_ref, ko_ref, vo_ref):
    nch = ck_ref.shape[2]
    half = CMP_STRIDE * NSA_HEAD_DIM
    pet, peb = pet_ref[...], peb_ref[...]

    def mlp(ch, w1_ref, w2_ref):
        a = _bdot((ch + pet).astype(BF16), w1_ref[0:half, :])
        bm = _bdot((ch + peb).astype(BF16), w1_ref[half:2 * half, :])
        h = _silu(a + pltpu.roll(bm, nch - 1, axis=0))
        return _bdot(h.astype(BF16), w2_ref[...])

    kc = _rope128(mlp(ck_ref[0, 0], wk1_ref, wk2_ref), c_ref[...], s1_ref[...], s2_ref[...])
    ko_ref[0, 0] = kc[:, :NSA_HEAD_DIM].astype(BF16)
    vo_ref[0, 0] = mlp(cv_ref[0, 0], wv1_ref, wv2_ref)[:, :NSA_HEAD_DIM].astype(BF16)


def _compress(ck, cv, pet, peb, wk1, wk2, wv1, wv2, cc, cs1, cs2):
    bsz, g, nch, width = ck.shape
    blk = lambda b, gi: (b, gi, 0, 0)
    full = lambda b, gi: (0, 0)
    return pl.pallas_call(
        _compress_kernel,
        grid=(bsz, g),
        in_specs=[pl.BlockSpec((1, 1, nch, width), blk), pl.BlockSpec((1, 1, nch, width), blk),
                  pl.BlockSpec(pet.shape, full), pl.BlockSpec(peb.shape, full),
                  pl.BlockSpec(wk1.shape, full), pl.BlockSpec(wk2.shape, full),
                  pl.BlockSpec(wv1.shape, full), pl.BlockSpec(wv2.shape, full),
                  pl.BlockSpec(cc.shape, full), pl.BlockSpec(cs1.shape, full), pl.BlockSpec(cs2.shape, full)],
        out_specs=[pl.BlockSpec((1, 1, nch, NSA_HEAD_DIM), blk), pl.BlockSpec((1, 1, nch, NSA_HEAD_DIM), blk)],
        out_shape=[jax.ShapeDtypeStruct((bsz, g, nch, NSA_HEAD_DIM), BF16)] * 2,
        compiler_params=_cparams(("arbitrary", "arbitrary")),
        name="nsa_compress",
    )(ck, cv, pet, peb, wk1, wk2, wv1, wv2, cc, cs1, cs2)


def _cmp_select_kernel(q_ref, kc_ref, vc_ref, wimp_ref, o_ref, sel_ref):
    tq = q_ref.shape[2]
    nch = kc_ref.shape[2]
    ns = wimp_ref.shape[0]
    s0 = pl.program_id(2) * tq
    ratio = SLC_BLOCK // CMP_STRIDE
    n_vis = (s0 + tq - CMP_BLOCK) // CMP_STRIDE + 1
    n_chunks = (n_vis + CMP_CW - 1) // CMP_CW

    def body(w):
        nb = w // ratio
        qpos_c = s0 + lax.broadcasted_iota(jnp.int32, (tq, 1), 0)
        cend = lax.broadcasted_iota(jnp.int32, (1, w), 1) * CMP_STRIDE + (CMP_BLOCK - 1)
        cmask = cend <= qpos_c
        anyvis = (qpos_c >= CMP_BLOCK - 1).astype(F32)
        hpg, d = q_ref.shape[1], q_ref.shape[3]
        s = _dot_nt(q_ref[0].reshape(hpg * tq, d), kc_ref[0, 0, :w, :]).reshape(hpg, tq, w)
        s = jnp.where(cmask[None], s, NEG)
        e = jnp.exp2(s - jnp.max(s, axis=-1, keepdims=True))
        p = e * (anyvis[None] / jnp.sum(e, axis=-1, keepdims=True))
        o_ref[0] = _bdot(p.reshape(hpg * tq, w).astype(BF16), vc_ref[0, 0, :w, :]).reshape(hpg, tq, d)
        pg = jnp.sum(p, axis=0)

        pg_hi = pg.astype(BF16)
        pg_lo = (pg - pg_hi.astype(F32)).astype(BF16)
        wimp = wimp_ref[:nb, :w]
        imp = _dot_nt(wimp, pg_hi) + _dot_nt(wimp, pg_lo)
        blk = lax.broadcasted_iota(jnp.int32, (nb, tq), 0)
        cur = (s0 + lax.broadcasted_iota(jnp.int32, (nb, tq), 1)) >> SLC_SHIFT
        forced = (blk == 0) | (blk == cur) | (blk == cur - 1)
        imp = jnp.where(blk <= cur, jnp.where(forced, BIG, imp), NEG)
        parts = [slice(i, i + LANES) for i in range(0, tq, LANES)]
        blk_f = blk[:, :LANES].astype(F32)
        imps = [imp[:, pt] for pt in parts]
        sels = [jnp.zeros((nb, LANES), F32) for _ in parts]
        for _ in range(min(SLC_TOP_N, nb)):
            for i in range(len(parts)):
                m = jnp.max(imps[i], axis=0, keepdims=True)
                first = jnp.min(jnp.where(imps[i] == m, blk_f, float(nb)), axis=0, keepdims=True)
                hit = blk_f == first
                sels[i] = jnp.where(hit, 1.0, sels[i])
                imps[i] = jnp.where(hit, -jnp.inf, imps[i])
        sel = jnp.concatenate(sels, axis=1)
        if nb < ns:
            sel = jnp.concatenate([sel, jnp.zeros((ns - nb, tq), F32)], axis=0)
        sel_ref[0, 0] = jnp.where(sel.T > 0.0, 0.0, NEG).astype(BF16)

    for nv in range(1, nch // CMP_CW + 1):
        pl.when(n_chunks == nv)(functools.partial(body, nv * CMP_CW))


def _cmp_select(q, k_cmp, v_cmp, wimp_t):
    bsz, _, s, d = q.shape
    g = NSA_GROUPS
    nch = k_cmp.shape[2]
    ns = wimp_t.shape[0]
    tq = min(CMP_TQ, s)
    hmap = lambda b, gi, qi: (b, gi, qi, 0)
    return pl.pallas_call(
        _cmp_select_kernel,
        grid=(bsz, g, s // tq),
        in_specs=[pl.BlockSpec((1, NSA_HPG, tq, d), hmap),
                  pl.BlockSpec((1, 1, nch, d), lambda b, gi, qi: (b, gi, 0, 0)),
                  pl.BlockSpec((1, 1, nch, d), lambda b, gi, qi: (b, gi, 0, 0)),
                  pl.BlockSpec(wimp_t.shape, lambda b, gi, qi: (0, 0))],
        out_specs=[pl.BlockSpec((1, NSA_HPG, tq, d), hmap),
                   pl.BlockSpec((1, 1, tq, ns), hmap)],
        out_shape=[jax.ShapeDtypeStruct((bsz, NSA_HEADS, s, d), F32),
                   jax.ShapeDtypeStruct((bsz, g, s, ns), BF16)],
        compiler_params=_cparams(("arbitrary", "arbitrary", "arbitrary")),
        name="nsa_cmp_select",
    )(q, k_cmp, v_cmp, wimp_t)


def _nsa_attn_kernel(q_ref, kst_ref, vs_ref, kw_ref, vw_ref, selb_ref, eexp_ref, ocmp_ref, gt_ref,
                     o_ref, m_ref, acc_ref, s_ref, p_ref, a_ref):
    hpg, tq, d = q_ref.shape[1:]
    rows = hpg * tq
    s0 = pl.program_id(2) * tq
    q4 = q_ref[0].reshape(rows, d)
    q4a = jnp.concatenate([jnp.concatenate([selb_ref[0, 0]] * hpg, axis=0), q4], axis=1)
    qpos = s0 + lax.broadcasted_iota(jnp.int32, (tq, 1), 0)

    def scores(t):
        return _bdot(q4a, jnp.concatenate([eexp_ref[t], kst_ref[0, 0, t]], axis=0))

    def flush(t):
        acc_ref[...] = a_ref[...] * acc_ref[...] + _bdot(p_ref[...], vs_ref[0, 0, t])

    def softmax_step(s):
        m_prev = m_ref[...]
        m_new = jnp.maximum(m_prev, jnp.max(s, axis=-1, keepdims=True))
        a_ref[...] = jnp.exp2(m_prev - m_new)
        p_ref[...] = jnp.exp2(s - m_new).astype(BF16)
        m_ref[...] = m_new

    m_ref[...] = jnp.full_like(m_ref, NEG)
    acc_ref[...] = jnp.zeros_like(acc_ref)
    p_ref[...] = jnp.zeros_like(p_ref)
    a_ref[...] = jnp.ones_like(a_ref)
    t_diag = s0 // ATT_TK
    s_ref[...] = scores(0)

    def step(t):
        s = s_ref[...]
        s_next = scores(t + 1)
        flush(jnp.maximum(t - 1, 0))
        softmax_step(s)
        s_ref[...] = s_next

    def pair_body(i, carry):
        step(2 * i)
        step(2 * i + 1)
        return carry

    lax.fori_loop(0, t_diag // 2, pair_body, 0)
    pl.when(t_diag % 2 == 1)(lambda: step(t_diag - 1))

    span = WINDOW + tq
    w0 = pl.multiple_of(jnp.maximum(s0 - WINDOW, 0), tq)
    rel = qpos - (w0 + lax.broadcasted_iota(jnp.int32, (1, span), 1))
    wbias = jnp.where((rel >= 0) & (rel < WINDOW), 0.0, NEG)
    sw = _dot_nt(q4, kw_ref[0, 0, pl.ds(w0, span), :])
    sw = (sw.reshape(hpg, tq, span) + wbias[None]).reshape(rows, span)
    pw = jnp.exp2(sw - jnp.max(sw, axis=-1, keepdims=True)).astype(BF16)
    accw = _bdot(pw, vw_ref[0, 0, pl.ds(w0, span), :])
    o_win = (accw[:, :d] / accw[:, d:d + 1]).reshape(hpg, tq, d)

    flush(jnp.maximum(t_diag - 1, 0))
    kpos = t_diag * ATT_TK + lax.broadcasted_iota(jnp.int32, (1, ATT_TK), 1)
    s = s_ref[...].reshape(hpg, tq, ATT_TK)
    softmax_step(jnp.where((kpos <= qpos)[None], s, NEG).reshape(rows, ATT_TK))
    flush(t_diag)
    acc = acc_ref[...]
    o_slc = (acc[:, :d] / acc[:, d:d + 1]).reshape(hpg, tq, d)

    gts = gt_ref[0]
    for h in range(hpg):
        g0, g1, g2 = (gts[:, 3 * h + j:3 * h + j + 1] for j in range(3))
        o_ref[0, h] = g0 * ocmp_ref[0, h] + g1 * o_slc[h] + g2 * o_win[h]


def _nsa_attn(q, kst, vst, kw, vw, selb, eexp, o_cmp, gates):
    bsz, _, s, d = q.shape
    g = NSA_GROUPS
    tq = ATT_TQ
    rows = NSA_HPG * tq
    ns = selb.shape[-1]
    hmap = lambda b, gi, qi: (b, gi, qi, 0)
    tmap = lambda b, gi, qi: (b, gi, 0, 0, 0)
    rmap = lambda b, gi, qi: (b, gi, 0, 0)
    return pl.pallas_call(
        _nsa_attn_kernel,
        grid=(bsz, g, s // tq),
        in_specs=[pl.BlockSpec((1, NSA_HPG, tq, d), hmap),
                  pl.BlockSpec((1, 1) + kst.shape[2:], tmap), pl.BlockSpec((1, 1) + vst.shape[2:], tmap),
                  pl.BlockSpec((1, 1) + kw.shape[2:], rmap), pl.BlockSpec((1, 1) + vw.shape[2:], rmap),
                  pl.BlockSpec((1, 1, tq, ns), hmap),
                  pl.BlockSpec(eexp.shape, lambda b, gi, qi: (0, 0, 0)),
                  pl.BlockSpec((1, NSA_HPG, tq, d), hmap),
                  pl.BlockSpec((1, tq, LANES), lambda b, gi, qi: (b, qi, gi))],
        out_specs=pl.BlockSpec((1, NSA_HPG, tq, d), hmap),
        out_shape=jax.ShapeDtypeStruct((bsz, NSA_HEADS, s, d), F32),
        scratch_shapes=[pltpu.VMEM((rows, 1), F32),
                        pltpu.VMEM((rows, 2 * d), F32),
                        pltpu.VMEM((rows, ATT_TK), F32),
                        pltpu.VMEM((rows, ATT_TK), BF16),
                        pltpu.VMEM((rows, 1), F32)],
        compiler_params=_cparams(("arbitrary", "arbitrary", "arbitrary")),
        name="nsa_attn",
    )(q, kst, vst, kw, vw, selb, eexp, o_cmp, gates)


def _rope_tables(pos):
    inv_freq = jnp.power(ROPE_THETA, -jnp.arange(ROT_HALF, dtype=F32) * (2.0 / ROT_DIM))
    ang = pos.astype(F32)[:, None] * inv_freq[None, :]
    cos, sin = jnp.cos(ang), jnp.sin(ang)
    n = pos.shape[0]
    one = jnp.ones((n, NSA_HEAD_DIM - ROT_DIM), F32)
    zero8 = jnp.zeros((n, ROT_HALF), F32)
    zero = jnp.zeros((n, NSA_HEAD_DIM - ROT_DIM), F32)
    c = jnp.concatenate([cos, cos, one], axis=1)
    s1 = jnp.concatenate([-sin, zero8, zero], axis=1)
    s2 = jnp.concatenate([zero8, sin, zero], axis=1)
    rep = LANES // NSA_HEAD_DIM
    return tuple(jnp.tile(t, (1, rep)) for t in (c, s1, s2))


def _importance_weights(ns, nch):
    ratio = SLC_BLOCK // CMP_STRIDE
    span = CMP_BLOCK // CMP_STRIDE
    w = np.zeros((ns, nch), np.float32)
    for j in range(ns):
        for m in range(ratio):
            for n in range(span):
                c = ratio * j + m + n
                if c < nch - 1:
                    w[j, c] += 1.0
    return jnp.asarray(w, dtype=BF16)


def _block_expansion(nt, ns):
    key_blk = (np.arange(nt)[:, None] * ATT_TK + np.arange(ATT_TK)[None, :]) // SLC_BLOCK
    e = (np.arange(ns)[None, :, None] == key_blk[:, None, :]).astype(np.float32)
    return jnp.asarray(e, dtype=BF16)


def _gla_layer(x2, bsz, seq, w_in, w_gate_up, b_gate, norm_g, w_out, ln_g, ln_b, router):
    hk = GLA_HEADS * GLA_DK
    hv = GLA_HEADS * GLA_DV
    cuts = np.cumsum([hk, hk, hv, GLA_GATE_RANK]).tolist()
    wq, wk, wv, wg, wr = jnp.split(w_in, cuts, axis=1)
    w_main = jnp.concatenate([wq, wk, wv, wr], axis=1).astype(BF16)
    w_glow = jnp.pad(wg, ((0, 0), (0, LANES - GLA_GATE_RANK))).astype(BF16)
    w_gu = jnp.pad(w_gate_up, ((0, LANES - GLA_GATE_RANK), (0, 0))).astype(BF16)
    q, k, v, r, la = _gla_proj(x2, w_main, w_glow, w_gu, b_gate.reshape(1, hk))
    sh = lambda t: t.reshape(bsz, seq, t.shape[-1])
    o = _gla_core(sh(q), sh(k), sh(v), sh(la), sh(r), norm_g.reshape(1, GLA_DV))
    return _outproj_ln(o.reshape(bsz * seq, hv), x2, w_out.astype(BF16),
                       ln_g.reshape(1, D_MODEL), ln_b.reshape(1, D_MODEL), router)


def _nsa_layer(x2, bsz, seq, w_in, w_ck1, w_ck2, w_cv1, w_cv2, cmp_pe, w_out, ln_g, ln_b, router):
    h, g, hpg, d = NSA_HEADS, NSA_GROUPS, NSA_HPG, NSA_HEAD_DIM
    gd = g * d
    cuts = np.cumsum([h * d] + [gd] * 6).tolist()
    wq, wkc, wvc, wks, wvs, wkw, wvw, wgt = jnp.split(w_in, cuts, axis=1)
    w_rope = jnp.concatenate([wq, wks, wkw], axis=1).astype(BF16)
    w_plain = jnp.concatenate([wkc, wvc, wvs, wvw], axis=1).astype(BF16)
    w_gates = jnp.pad(wgt.reshape(D_MODEL, g, hpg * 3), ((0, 0), (0, 0), (0, LANES - hpg * 3)))
    w_gates = w_gates.reshape(D_MODEL, g * LANES).astype(BF16)
    rc, rs1, rs2 = _rope_tables(jnp.arange(seq, dtype=jnp.int32))
    q, kst, kw, kc, vc, vs1, vw1, gates = _nsa_proj(x2, w_rope, w_plain, w_gates, rc, rs1, rs2, seq)

    nch = seq // CMP_STRIDE
    chunks = lambda t: t.reshape(bsz, g, nch, CMP_STRIDE * d)
    pe = cmp_pe.reshape(1, CMP_BLOCK * d)
    half = CMP_STRIDE * d
    pad2 = lambda w: jnp.pad(w, ((0, 0), (0, LANES - d))).astype(BF16)
    cc, cs1, cs2 = _rope_tables(jnp.arange(nch, dtype=jnp.int32) * CMP_STRIDE + (CMP_BLOCK - 1))
    k_cmp, v_cmp = _compress(chunks(kc), chunks(vc), pe[:, :half], pe[:, half:],
                             w_ck1.astype(BF16), pad2(w_ck2), w_cv1.astype(BF16), pad2(w_cv2), cc, cs1, cs2)

    ns = seq // SLC_BLOCK
    sh = lambda t: t.reshape(bsz, seq, t.shape[-1])
    o_cmp, selb = _cmp_select(q, k_cmp, v_cmp, _importance_weights(ns, nch))

    nt = seq // ATT_TK
    o = _nsa_attn(q, kst, vs1.reshape(bsz, g, nt, ATT_TK, 2 * d), kw, vw1,
                  selb, _block_expansion(nt, ns), o_cmp, sh(gates))
    return _outproj_heads_ln(o, x2, w_out.astype(BF16), ln_g.reshape(1, D_MODEL), ln_b.reshape(1, D_MODEL), router)


def _moe_layer(xa, grp, w_gate, w_up, w_down, ln_g, ln_b):
    wd = w_down.astype(BF16).reshape(N_EXPERT_GROUPS, EXPERTS_PER_GROUP * D_FF_EXPERT, D_MODEL)
    dest, g_lo, g_hi = _group_sort_plan(grp)
    return _moe_sorted(xa, dest, g_lo, g_hi, w_gate.astype(BF16), w_up.astype(BF16), wd,
                       ln_g.reshape(1, D_MODEL), ln_b.reshape(1, D_MODEL))


def kernel(x, gla_w_in, gla_w_gate_up, gla_b_gate, gla_norm_g, gla_w_out, nsa_w_in, nsa_w_cmp_k1, nsa_w_cmp_k2,
           nsa_w_cmp_v1, nsa_w_cmp_v2, nsa_cmp_pe, nsa_w_out, moe_w_router, moe_b_router, moe_w_gate, moe_w_up,
           moe_w_down, ln_g, ln_b):
    bsz, seq, _ = x.shape
    x2 = x.reshape(bsz * seq, D_MODEL)
    w_router_t = moe_w_router.T
    wr_hi = w_router_t.astype(BF16)
    wr_lo = (w_router_t - wr_hi.astype(F32)).astype(BF16)
    router = (wr_hi, wr_lo, moe_b_router.reshape(N_EXPERTS, 1))
    for i in range(DEPTH):
        j = i // 2
        if i % 2 == 0:
            xa, grp = _gla_layer(x2, bsz, seq, gla_w_in[j], gla_w_gate_up[j], gla_b_gate[j], gla_norm_g[j],
                                 gla_w_out[j], ln_g[i, 0], ln_b[i, 0], router)
        else:
            xa, grp = _nsa_layer(x2, bsz, seq, nsa_w_in[j], nsa_w_cmp_k1[j], nsa_w_cmp_k2[j], nsa_w_cmp_v1[j],
                                 nsa_w_cmp_v2[j], nsa_cmp_pe[j], nsa_w_out[j], ln_g[i, 0], ln_b[i, 0], router)
        x2 = _moe_layer(xa, grp, moe_w_gate[i], moe_w_up[i], moe_w_down[i], ln_g[i, 1], ln_b[i, 1])
    return x2.reshape(bsz, seq, D_MODEL)
```

```python
import functools

import numpy as np
import jax
import jax.numpy as jnp
from jax import lax
from jax.experimental import pallas as pl
from jax.experimental.pallas import tpu as pltpu

F32 = jnp.float32
BF16 = jnp.bfloat16
HIGHEST = lax.Precision.HIGHEST

D_MODEL = 1024
DEPTH = 2

GLA_HEADS = 4
GLA_DK = D_MODEL // 2 // GLA_HEADS
GLA_DV = D_MODEL // GLA_HEADS
GLA_GATE_RANK = 16
GLA_TAU = 16.0
GLA_CHUNK = 64

NSA_HEADS = 16
NSA_GROUPS = 4
NSA_HPG = NSA_HEADS // NSA_GROUPS
NSA_HEAD_DIM = D_MODEL // NSA_HEADS
CMP_BLOCK = 32
CMP_STRIDE = 16
CMP_HIDDEN = 256
SLC_BLOCK = 64
SLC_SHIFT = SLC_BLOCK.bit_length() - 1
SLC_TOP_N = 16
WINDOW = 512

ROPE_THETA = 500000.0
ROT_DIM = NSA_HEAD_DIM // 4
ROT_HALF = ROT_DIM // 2

N_EXPERTS = 16
N_EXPERT_GROUPS = 4
EXPERTS_PER_GROUP = N_EXPERTS // N_EXPERT_GROUPS
D_FF_EXPERT = D_MODEL // 4

DN_ALPHA = (2 * DEPTH) ** 0.25
LN_EPS = 1e-5
NEG = -1e30
BIG = 1e30
LOG2E = 1.4426950408889634

LANES = 128
VMEM_LIMIT = 48 * 1024 * 1024

PROJ_TM = 256
OUT_TM = 512
GLA_T = 512
MOE_TS = 512
MOE_VMEM_LIMIT = 56 * 1024 * 1024
XA_WIDTH = D_MODEL + LANES
CMP_TQ = 512
CMP_CW = 128
ATT_TQ = 128
ATT_TK = 512
ATT_GROUPS = 2
WIN_TK = 128


def _cparams(sem):
    return pltpu.CompilerParams(dimension_semantics=sem, vmem_limit_bytes=VMEM_LIMIT)


def _bdot(a, b):
    return jnp.dot(a, b, preferred_element_type=F32)


def _dot_nt(a, b, precision=None):
    return lax.dot_general(a, b, (((1,), (1,)), ((), ())), preferred_element_type=F32, precision=precision)


def _dot_tn(a, b):
    return lax.dot_general(a, b, (((0,), (0,)), ((), ())), preferred_element_type=F32)


def _layer_norm(z, g, b):
    mu = jnp.mean(z, axis=-1, keepdims=True)
    zc = z - mu
    var = jnp.mean(zc * zc, axis=-1, keepdims=True)
    return zc * lax.rsqrt(var + LN_EPS) * g + b


def _silu(t):
    return t * (0.5 * jnp.tanh(0.5 * t) + 0.5)


def _rope128(t, c, s1, s2):
    up = pltpu.roll(t, LANES - ROT_HALF, axis=1)
    dn = pltpu.roll(t, ROT_HALF, axis=1)
    return t * c + up * s1 + dn * s2


def _gla_proj_kernel(x_ref, wm_ref, wg_ref, wgu_ref, bg_ref, q_ref, k_ref, v_ref, r_ref, la_ref):
    xb = x_ref[...].astype(BF16)
    hk = GLA_HEADS * GLA_DK
    hv = GLA_HEADS * GLA_DV
    q_ref[...] = _bdot(xb, wm_ref[:, 0:hk])
    k_ref[...] = _bdot(xb, wm_ref[:, hk:2 * hk])
    v_ref[...] = _bdot(xb, wm_ref[:, 2 * hk:2 * hk + hv])
    r_ref[...] = _bdot(xb, wm_ref[:, 2 * hk + hv:2 * hk + 2 * hv])
    g_low = _bdot(xb, wg_ref[...])
    z = _bdot(g_low.astype(BF16), wgu_ref[...]) + bg_ref[...]
    log_sig = jnp.minimum(z, 0.0) - jnp.log1p(jnp.exp(-jnp.abs(z)))
    la_ref[...] = log_sig * (1.0 / GLA_TAU)


def _gla_proj(x2, w_main, w_glow, w_gu, b_gate):
    n = x2.shape[0]
    hk = GLA_HEADS * GLA_DK
    hv = GLA_HEADS * GLA_DV
    tm = PROJ_TM
    row = lambda i: (i, 0)
    full = lambda i: (0, 0)
    return pl.pallas_call(
        _gla_proj_kernel,
        grid=(n // tm,),
        in_specs=[pl.BlockSpec((tm, D_MODEL), row),
                  pl.BlockSpec(w_main.shape, full),
                  pl.BlockSpec(w_glow.shape, full),
                  pl.BlockSpec(w_gu.shape, full),
                  pl.BlockSpec(b_gate.shape, full)],
        out_specs=[pl.BlockSpec((tm, hk), row), pl.BlockSpec((tm, hk), row),
                   pl.BlockSpec((tm, hv), row), pl.BlockSpec((tm, hv), row),
                   pl.BlockSpec((tm, hk), row)],
        out_shape=[jax.ShapeDtypeStruct((n, hk), F32), jax.ShapeDtypeStruct((n, hk), F32),
                   jax.ShapeDtypeStruct((n, hv), F32), jax.ShapeDtypeStruct((n, hv), F32),
                   jax.ShapeDtypeStruct((n, hk), F32)],
        compiler_params=_cparams(("arbitrary",)),
        name="gla_proj",
    )(x2, w_main, w_glow, w_gu, b_gate)


def _dot_01(m, x):
    x1 = x.astype(BF16)
    r1 = x - x1.astype(F32)
    x2 = r1.astype(BF16)
    x3 = (r1 - x2.astype(F32)).astype(BF16)
    return _bdot(m, x1) + _bdot(m, x2) + _bdot(m, x3)


def _gla_core_kernel(q_ref, k_ref, v_ref, la_ref, r_ref, g_ref, o_ref, st_ref):
    @pl.when(pl.program_id(2) == 0)
    def _():
        st_ref[...] = jnp.zeros_like(st_ref)

    c, t = GLA_CHUNK, GLA_T
    shift = c.bit_length() - 1
    row = lax.broadcasted_iota(jnp.int32, (t, t), 0)
    col = lax.broadcasted_iota(jnp.int32, (t, t), 1)
    causal = ((row >> shift) == (col >> shift)) & (row >= col)
    in_chunk = lax.broadcasted_iota(jnp.int32, (t, GLA_DK), 0) & (c - 1)
    b = la_ref[0]
    step = 1
    while step < c:
        b = b + jnp.where(in_chunk >= step, pltpu.roll(b, step, axis=0), 0.0)
        step *= 2
    b_last = jnp.broadcast_to(b.reshape(t // c, c, GLA_DK)[:, c - 1:c, :], (t // c, c, GLA_DK)).reshape(t, GLA_DK)
    q = q_ref[0] * (GLA_DK ** -0.5)
    k = k_ref[0]
    vb = v_ref[0].astype(BF16)
    q_dec = (q * jnp.exp(b)).astype(BF16)
    k_neg = (k * jnp.exp(-b)).astype(BF16)
    k_dec = (k * jnp.exp(b_last - b)).astype(BF16)
    decay = jnp.exp(b_last)
    s = jnp.where(causal, _dot_nt(q_dec, k_neg), 0.0)
    o = _bdot(s.astype(BF16), vb)

    chunks = [slice(ci * c, (ci + 1) * c) for ci in range(t // c)]
    kv = [_dot_tn(vb[rows], k_dec[rows]) for rows in chunks]
    st = st_ref[...]
    entering = []
    for ci, rows in enumerate(chunks):
        entering.append(st.astype(BF16))
        st = decay[ci * c:ci * c + 1] * st + kv[ci]
    st_ref[...] = st
    o = o + jnp.concatenate([_dot_nt(q_dec[rows], s_in) for rows, s_in in zip(chunks, entering)], axis=0)
    ms = jnp.mean(o * o, axis=-1, keepdims=True)
    o = o * lax.rsqrt(ms + LN_EPS) * g_ref[...]
    o_ref[0] = o * _silu(r_ref[0])


def _gla_core(q, k, v, la, r, norm_g):
    bsz, s, _ = q.shape
    t = GLA_T
    kmap = lambda b, h, n: (b, n, h)
    return pl.pallas_call(
        _gla_core_kernel,
        grid=(bsz, GLA_HEADS, s // t),
        in_specs=[pl.BlockSpec((1, t, GLA_DK), kmap), pl.BlockSpec((1, t, GLA_DK), kmap),
                  pl.BlockSpec((1, t, GLA_DV), kmap), pl.BlockSpec((1, t, GLA_DK), kmap),
                  pl.BlockSpec((1, t, GLA_DV), kmap),
                  pl.BlockSpec((1, GLA_DV), lambda b, h, n: (0, 0))],
        out_specs=pl.BlockSpec((1, t, GLA_DV), kmap),
        out_shape=jax.ShapeDtypeStruct((bsz, s, GLA_HEADS * GLA_DV), F32),
        scratch_shapes=[pltpu.VMEM((GLA_DV, GLA_DK), F32)],
        compiler_params=_cparams(("arbitrary", "arbitrary", "arbitrary")),
        name="gla_core",
    )(q, k, v, la, r, norm_g)


def _row_xor(v, k, row):
    n = v.shape[0]
    up = pltpu.roll(v, n - k, axis=0)
    dn = pltpu.roll(v, k, axis=0)
    return jnp.where((row & k) == 0, up, dn)


def _route(x, wt_hi, wt_lo, bias):
    x_hi = x.astype(BF16)
    x_lo = (x - x_hi.astype(F32)).astype(BF16)
    logits = _dot_nt(wt_hi, x_hi) + (_dot_nt(wt_lo, x_hi) + _dot_nt(wt_hi, x_lo))
    s = jax.nn.sigmoid(logits)
    a = s + bias
    row = lax.broadcasted_iota(jnp.int32, a.shape, 0)
    a1 = _row_xor(a, 1, row)
    a2 = _row_xor(a, 2, row)
    a3 = _row_xor(a1, 2, row)
    p, q = jnp.maximum(a, a1), jnp.minimum(a, a1)
    r, t = jnp.maximum(a2, a3), jnp.minimum(a2, a3)
    gs = jnp.maximum(p, r) + jnp.maximum(jnp.minimum(p, r), jnp.maximum(q, t))
    g1 = _row_xor(gs, 4, row)
    g2 = _row_xor(gs, 8, row)
    g3 = _row_xor(g1, 8, row)

    def beats(other, other_first):
        return (other > gs) | ((other == gs) & other_first)

    lose = (beats(g1, (row & 4) != 0) | beats(g2, (row & 8) != 0) | beats(g3, (row & 8) != 0))

    def ahead(other, other_first):
        return jnp.where((other > a) | ((other == a) & other_first), 1.0, 0.0)

    rank = ahead(a1, (row & 1) != 0) + ahead(a2, (row & 2) != 0) + ahead(a3, (row & 2) != 0)
    sel = jnp.logical_not(lose) & (rank < 2.0)
    ssel = jnp.where(sel, s, 0.0)
    s1 = _row_xor(ssel, 1, row)
    tot = (ssel + s1) + (_row_xor(ssel, 2, row) + _row_xor(s1, 2, row))
    gate = jnp.where(sel, s / tot, 0.0)
    group = jnp.max(jnp.where(sel, (row >> 2).astype(F32), 0.0), axis=0, keepdims=True)
    tokens = x.shape[0]
    record = jnp.concatenate([gate, jnp.zeros((LANES - N_EXPERTS, tokens), F32)], axis=0)
    return record.T, group.astype(jnp.int32)


def _norm_route_store(z, g_ref, b_ref, wrh_ref, wrl_ref, br_ref, o_ref, grp_ref):
    x = _layer_norm(z, g_ref[...], b_ref[...])
    o_ref[:, :D_MODEL] = x
    o_ref[:, D_MODEL:], grp_ref[...] = _route(x, wrh_ref[...], wrl_ref[...], br_ref[...])


def _outproj_ln_kernel(h_ref, x_ref, w_ref, g_ref, b_ref, wrh_ref, wrl_ref, br_ref, o_ref, grp_ref):
    y = _bdot(h_ref[...].astype(BF16), w_ref[...])
    _norm_route_store(DN_ALPHA * x_ref[...] + y, g_ref, b_ref, wrh_ref, wrl_ref, br_ref, o_ref, grp_ref)


def _outproj_heads_ln_kernel(h_ref, x_ref, w_ref, g_ref, b_ref, wrh_ref, wrl_ref, br_ref, o_ref, grp_ref):
    h = jnp.concatenate([h_ref[0, i].astype(BF16) for i in range(h_ref.shape[1])], axis=1)
    _norm_route_store(DN_ALPHA * x_ref[...] + _bdot(h, w_ref[...]), g_ref, b_ref, wrh_ref, wrl_ref, br_ref,
                      o_ref, grp_ref)


def _outproj_call(kernel, name, h, h_spec, x2, w, g, b, router):
    n = x2.shape[0]
    tm = OUT_TM
    row = lambda i: (i, 0)
    full = lambda i: (0, 0)
    return pl.pallas_call(
        kernel,
        grid=(n // tm,),
        in_specs=[h_spec, pl.BlockSpec((tm, D_MODEL), row),
                  pl.BlockSpec(w.shape, full), pl.BlockSpec((1, D_MODEL), full), pl.BlockSpec((1, D_MODEL), full)]
                 + [pl.BlockSpec(r.shape, full) for r in router],
        out_specs=[pl.BlockSpec((tm, XA_WIDTH), row), pl.BlockSpec((1, tm), lambda i: (0, i))],
        out_shape=[jax.ShapeDtypeStruct((n, XA_WIDTH), F32), jax.ShapeDtypeStruct((1, n), jnp.int32)],
        compiler_params=_cparams(("arbitrary",)),
        name=name,
    )(h, x2, w, g, b, *router)


def _outproj_heads_ln(h4, x2, w, g, b, router):
    _, heads, seq, d = h4.shape
    per = seq // OUT_TM
    spec = pl.BlockSpec((1, heads, OUT_TM, d), lambda i: (i // per, 0, i % per, 0))
    return _outproj_call(_outproj_heads_ln_kernel, "outproj_heads_ln", h4, spec, x2, w, g, b, router)


def _outproj_ln(h2, x2, w, g, b, router):
    spec = pl.BlockSpec((OUT_TM, h2.shape[1]), lambda i: (i, 0))
    return _outproj_call(_outproj_ln_kernel, "outproj_ln", h2, spec, x2, w, g, b, router)


def _moe_sorted_kernel(dest_ref, glo_ref, ghi_ref, xa_hbm, wg_ref, wu_ref, wd_ref, lg_ref, lb_ref, out_hbm,
                       src_ref, xbuf, obuf, in_sem, out_sem):
    t = pl.program_id(0)
    nt = pl.num_programs(0)
    ts = MOE_TS
    slot = t & 1

    def row_in(tile, sl, r):
        return pltpu.make_async_copy(xa_hbm.at[src_ref[tile * ts + r]], xbuf.at[sl, r], in_sem.at[sl])

    def row_out(tile, sl, r):
        return pltpu.make_async_copy(obuf.at[sl, r], out_hbm.at[src_ref[tile * ts + r]], out_sem.at[sl])

    def wait_in(sl):
        pltpu.make_async_copy(xa_hbm.at[pl.ds(0, ts)], xbuf.at[sl], in_sem.at[sl]).wait()

    def wait_out(sl):
        pltpu.make_async_copy(obuf.at[sl], out_hbm.at[pl.ds(0, ts)], out_sem.at[sl]).wait()

    @pl.when(t == 0)
    def _():
        def invert(i, carry):
            src_ref[dest_ref[i]] = i
            return carry

        lax.fori_loop(0, dest_ref.shape[0], invert, 0, unroll=8)
        for r in range(ts):
            row_in(0, 0, r).start()

    @pl.when(t >= 2)
    def _():
        wait_out(slot)

    wait_in(slot)
    nxt = jnp.minimum(t + 1, nt - 1)
    for r in range(ts):
        row_in(nxt, 1 - slot, r).start()

    xa = xbuf[slot]
    x = xa[:, :D_MODEL]
    gate = xa[:, D_MODEL:]
    xb = x.astype(BF16)
    f = D_FF_EXPERT
    epg = EXPERTS_PER_GROUP

    def group_ffn(g):
        hs = []
        for e in range(epg):
            col = gate[:, e:e + 1]
            for other in range(1, N_EXPERT_GROUPS):
                col = jnp.where(g == other, gate[:, other * epg + e:other * epg + e + 1], col)
            expert = g * epg + e
            hs.append((_silu(_bdot(xb, wg_ref[expert])) * _bdot(xb, wu_ref[expert]) * col).astype(BF16))
        return _bdot(jnp.concatenate(hs, axis=1), wd_ref[g])

    y = group_ffn(glo_ref[t])
    y = lax.fori_loop(glo_ref[t] + 1, ghi_ref[t] + 1, lambda g, acc: acc + group_ffn(g), y)
    obuf[slot] = _layer_norm(DN_ALPHA * x + y, lg_ref[...], lb_ref[...])
    for r in range(ts):
        row_out(t, slot, r).start()

    @pl.when(t == nt - 1)
    def _():
        wait_out(slot)
        wait_in(1 - slot)

    @pl.when((t == nt - 1) & (nt >= 2))
    def _():
        wait_out(1 - slot)


def _moe_sorted(xa, dest, g_lo, g_hi, wg, wu, wd, lg, lb):
    n = xa.shape[0]
    ts = MOE_TS
    resident = pl.Buffered(1)
    whole = lambda i, *_: (0, 0, 0)
    grid_spec = pltpu.PrefetchScalarGridSpec(
        num_scalar_prefetch=3,
        grid=(n // ts,),
        in_specs=[pl.BlockSpec(memory_space=pl.ANY),
                  pl.BlockSpec(wg.shape, whole, pipeline_mode=resident),
                  pl.BlockSpec(wu.shape, whole, pipeline_mode=resident),
                  pl.BlockSpec(wd.shape, whole, pipeline_mode=resident),
                  pl.BlockSpec((1, D_MODEL), lambda i, *_: (0, 0)),
                  pl.BlockSpec((1, D_MODEL), lambda i, *_: (0, 0))],
        out_specs=pl.BlockSpec(memory_space=pl.ANY),
        scratch_shapes=[pltpu.SMEM((n,), jnp.int32),
                        pltpu.VMEM((2, ts, XA_WIDTH), F32), pltpu.VMEM((2, ts, D_MODEL), F32),
                        pltpu.SemaphoreType.DMA((2,)), pltpu.SemaphoreType.DMA((2,))])
    return pl.pallas_call(
        _moe_sorted_kernel,
        grid_spec=grid_spec,
        out_shape=jax.ShapeDtypeStruct((n, D_MODEL), F32),
        compiler_params=pltpu.CompilerParams(dimension_semantics=("arbitrary",), vmem_limit_bytes=MOE_VMEM_LIMIT),
        name="moe_sorted",
    )(dest, g_lo, g_hi, xa, wg, wu, wd, lg, lb)


def _group_sort_plan(grp):
    n = grp.shape[1]
    onehot = (grp == jnp.arange(N_EXPERT_GROUPS, dtype=jnp.int32)[:, None]).astype(jnp.int32)
    csum = jnp.cumsum(onehot, axis=1)
    ends = jnp.cumsum(csum[:, -1])
    dest = jnp.sum(onehot * (csum - 1 + (ends - csum[:, -1])[:, None]), axis=0)
    first_row = jnp.arange(n // MOE_TS, dtype=jnp.int32) * MOE_TS
    group_of_row = lambda rows: jnp.sum(ends[None, :] <= rows[:, None], axis=1).astype(jnp.int32)
    return dest, group_of_row(first_row), group_of_row(first_row + (MOE_TS - 1))


def _nsa_proj_kernel(x_ref, wr_ref, wp_ref, wgt_ref, c_ref, s1_ref, s2_ref,
                     q_ref, kst_ref, kw_ref, kc_ref, vc_ref, vs_ref, vw_ref, gt_ref):
    xb = x_ref[...].astype(BF16)
    c, s1, s2 = c_ref[...], s1_ref[...], s2_ref[...]
    d = NSA_HEAD_DIM
    nq = NSA_HEADS * d // LANES
    ng = NSA_GROUPS * d // LANES
    first_head = lax.broadcasted_iota(jnp.int32, (x_ref.shape[0], LANES), 1) < d

    def put_rows(ref, j, t):
        ref[0, 2 * j] = t[:, :d].astype(ref.dtype)
        ref[0, 2 * j + 1] = t[:, d:].astype(ref.dtype)

    def put_with_ones(ref, j, t):
        ref[0, 2 * j] = jnp.where(first_head, t, 1.0).astype(ref.dtype)
        ref[0, 2 * j + 1] = jnp.where(first_head, pltpu.roll(t, d, axis=1), 1.0).astype(ref.dtype)

    for j in range(nq + 2 * ng):
        t = _rope128(_bdot(xb, wr_ref[:, j * LANES:(j + 1) * LANES]), c, s1, s2)
        if j < nq:
            put_rows(q_ref, j, t * (d ** -0.5 * LOG2E))
        elif j < nq + ng:
            tt = t.T.astype(BF16)
            kst_ref[0, 2 * (j - nq), 0] = tt[:d]
            kst_ref[0, 2 * (j - nq) + 1, 0] = tt[d:]
        else:
            put_rows(kw_ref, j - nq - ng, t)
    for idx, (ref, put) in enumerate(((kc_ref, put_rows), (vc_ref, put_rows),
                                      (vs_ref, put_with_ones), (vw_ref, put_with_ones))):
        for j in range(ng):
            col = (idx * ng + j) * LANES
            put(ref, j, _bdot(xb, wp_ref[:, col:col + LANES]))
    gt_ref[...] = jax.nn.sigmoid(_bdot(xb, wgt_ref[...]))


def _nsa_proj(x2, w_rope, w_plain, w_gates, rc, rs1, rs2, seq):
    n = x2.shape[0]
    tm = PROJ_TM
    gd = NSA_GROUPS * NSA_HEAD_DIM
    row = lambda i: (i, 0)
    full = lambda i: (0, 0)
    per = seq // tm
    pos = lambda i: (i % per, 0)
    bsz, g, d = n // seq, NSA_GROUPS, NSA_HEAD_DIM
    sub = ATT_TK // tm
    hmap = lambda i: (i // per, 0, i % per, 0)

    def rows_out(heads, width, dtype):
        return pl.BlockSpec((1, heads, tm, width), hmap), jax.ShapeDtypeStruct((bsz, heads, seq, width), dtype)

    outs = [rows_out(NSA_HEADS, d, BF16),
            (pl.BlockSpec((1, g, 1, d, tm), lambda i: (i // per, 0, (i % per) // sub, 0, (i % per) % sub)),
             jax.ShapeDtypeStruct((bsz, g, seq // ATT_TK, d, ATT_TK), BF16)),
            rows_out(g, d, BF16),
            rows_out(g, d, F32), rows_out(g, d, F32),
            rows_out(g, 2 * d, BF16), rows_out(g, 2 * d, BF16),
            (pl.BlockSpec((tm, g * LANES), row), jax.ShapeDtypeStruct((n, g * LANES), F32))]
    return pl.pallas_call(
        _nsa_proj_kernel,
        grid=(n // tm,),
        in_specs=[pl.BlockSpec((tm, D_MODEL), row),
                  pl.BlockSpec(w_rope.shape, full), pl.BlockSpec(w_plain.shape, full),
                  pl.BlockSpec(w_gates.shape, full),
                  pl.BlockSpec((tm, LANES), pos), pl.BlockSpec((tm, LANES), pos), pl.BlockSpec((tm, LANES), pos)],
        out_specs=[o[0] for o in outs],
        out_shape=[o[1] for o in outs],
        compiler_params=_cparams(("arbitrary",)),
        name="nsa_proj",
    )(x2, w_rope, w_plain, w_gates, rc, rs1, rs2)


def _compress_kernel(ck_ref, cv_ref, pet_ref, peb_ref, wk1_ref, wk2_ref, wv1_ref, wv2_ref,
                     c_ref, s1_ref, s2_ref, ko_ref, vo_ref):
    nch = ck_ref.shape[2]
    half = CMP_STRIDE * NSA_HEAD_DIM
    pet, peb = pet_ref[...], peb_ref[...]

    def mlp(ch, w1_ref, w2_ref):
        a = _bdot((ch + pet).astype(BF16), w1_ref[0:half, :])
        bm = _bdot((ch + peb).astype(BF16), w1_ref[half:2 * half, :])
        h = _silu(a + pltpu.roll(bm, nch - 1, axis=0))
        return _bdot(h.astype(BF16), w2_ref[...])

    kc = _rope128(mlp(ck_ref[0, 0], wk1_ref, wk2_ref), c_ref[...], s1_ref[...], s2_ref[...])
    ko_ref[0, 0] = kc[:, :NSA_HEAD_DIM].astype(BF16)
    vo_ref[0, 0] = mlp(cv_ref[0, 0], wv1_ref, wv2_ref)[:, :NSA_HEAD_DIM].astype(BF16)


def _compress(ck, cv, pet, peb, wk1, wk2, wv1, wv2, cc, cs1, cs2):
    bsz, g, nch, width = ck.shape
    blk = lambda b, gi: (b, gi, 0, 0)
    full = lambda b, gi: (0, 0)
    return pl.pallas_call(
        _compress_kernel,
        grid=(bsz, g),
        in_specs=[pl.BlockSpec((1, 1, nch, width), blk), pl.BlockSpec((1, 1, nch, width), blk),
                  pl.BlockSpec(pet.shape, full), pl.BlockSpec(peb.shape, full),
                  pl.BlockSpec(wk1.shape, full), pl.BlockSpec(wk2.shape, full),
                  pl.BlockSpec(wv1.shape, full), pl.BlockSpec(wv2.shape, full),
                  pl.BlockSpec(cc.shape, full), pl.BlockSpec(cs1.shape, full), pl.BlockSpec(cs2.shape, full)],
        out_specs=[pl.BlockSpec((1, 1, nch, NSA_HEAD_DIM), blk), pl.BlockSpec((1, 1, nch, NSA_HEAD_DIM), blk)],
        out_shape=[jax.ShapeDtypeStruct((bsz, g, nch, NSA_HEAD_DIM), BF16)] * 2,
        compiler_params=_cparams(("arbitrary", "arbitrary")),
        name="nsa_compress",
    )(ck, cv, pet, peb, wk1, wk2, wv1, wv2, cc, cs1, cs2)


def _cmp_select_kernel(q_ref, kc_ref, vc_ref, wimp_ref, o_ref, sel_ref):
    tq = q_ref.shape[2]
    nch = kc_ref.shape[2]
    ns = wimp_ref.shape[0]
    s0 = pl.program_id(2) * tq
    ratio = SLC_BLOCK // CMP_STRIDE
    n_vis = (s0 + tq - CMP_BLOCK) // CMP_STRIDE + 1
    n_chunks = (n_vis + CMP_CW - 1) // CMP_CW

    def body(w):
        nb = w // ratio
        qpos_c = s0 + lax.broadcasted_iota(jnp.int32, (tq, 1), 0)
        cend = lax.broadcasted_iota(jnp.int32, (1, w), 1) * CMP_STRIDE + (CMP_BLOCK - 1)
        cmask = cend <= qpos_c
        anyvis = (qpos_c >= CMP_BLOCK - 1).astype(F32)
        hpg, d = q_ref.shape[1], q_ref.shape[3]
        s = _dot_nt(q_ref[0].reshape(hpg * tq, d), kc_ref[0, 0, :w, :]).reshape(hpg, tq, w)
        s = jnp.where(cmask[None], s, NEG)
        e = jnp.exp2(s - jnp.max(s, axis=-1, keepdims=True))
        p = e * (anyvis[None] / jnp.sum(e, axis=-1, keepdims=True))
        o_ref[0] = _bdot(p.reshape(hpg * tq, w).astype(BF16), vc_ref[0, 0, :w, :]).reshape(hpg, tq, d)
        pg = jnp.sum(p, axis=0)

        pg_hi = pg.astype(BF16)
        pg_lo = (pg - pg_hi.astype(F32)).astype(BF16)
        wimp = wimp_ref[:nb, :w]
        imp = _dot_nt(wimp, pg_hi) + _dot_nt(wimp, pg_lo)
        blk = lax.broadcasted_iota(jnp.int32, (nb, tq), 0)
        cur = (s0 + lax.broadcasted_iota(jnp.int32, (nb, tq), 1)) >> SLC_SHIFT
        forced = (blk == 0) | (blk == cur) | (blk == cur - 1)
        imp = jnp.where(blk <= cur, jnp.where(forced, BIG, imp), NEG)
        parts = [slice(i, i + LANES) for i in range(0, tq, LANES)]
        blk_f = blk[:, :LANES].astype(F32)
        imps = [imp[:, pt] for pt in parts]
        sels = [jnp.zeros((nb, LANES), F32) for _ in parts]
        for _ in range(min(SLC_TOP_N, nb)):
            for i in range(len(parts)):
                m = jnp.max(imps[i], axis=0, keepdims=True)
                first = jnp.min(jnp.where(imps[i] == m, blk_f, float(nb)), axis=0, keepdims=True)
                hit = blk_f == first
                sels[i] = jnp.where(hit, 1.0, sels[i])
                imps[i] = jnp.where(hit, -jnp.inf, imps[i])
        sel = jnp.concatenate(sels, axis=1)
        if nb < ns:
            sel = jnp.concatenate([sel, jnp.zeros((ns - nb, tq), F32)], axis=0)
        sel_ref[0, 0] = jnp.where(sel.T > 0.0, 0.0, NEG).astype(BF16)

    for nv in range(1, nch // CMP_CW + 1):
        pl.when(n_chunks == nv)(functools.partial(body, nv * CMP_CW))


def _cmp_select(q, k_cmp, v_cmp, wimp_t):
    bsz, _, s, d = q.shape
    g = NSA_GROUPS
    nch = k_cmp.shape[2]
    ns = wimp_t.shape[0]
    tq = min(CMP_TQ, s)
    hmap = lambda b, gi, qi: (b, gi, qi, 0)
    return pl.pallas_call(
        _cmp_select_kernel,
        grid=(bsz, g, s // tq),
        in_specs=[pl.BlockSpec((1, NSA_HPG, tq, d), hmap),
                  pl.BlockSpec((1, 1, nch, d), lambda b, gi, qi: (b, gi, 0, 0)),
                  pl.BlockSpec((1, 1, nch, d), lambda b, gi, qi: (b, gi, 0, 0)),
                  pl.BlockSpec(wimp_t.shape, lambda b, gi, qi: (0, 0))],
        out_specs=[pl.BlockSpec((1, NSA_HPG, tq, d), hmap),
                   pl.BlockSpec((1, 1, tq, ns), hmap)],
        out_shape=[jax.ShapeDtypeStruct((bsz, NSA_HEADS, s, d), F32),
                   jax.ShapeDtypeStruct((bsz, g, s, ns), BF16)],
        compiler_params=_cparams(("arbitrary", "arbitrary", "arbitrary")),
        name="nsa_cmp_select",
    )(q, k_cmp, v_cmp, wimp_t)


def _nsa_attn_kernel(q_ref, kst_ref, vs_ref, kw_ref, vw_ref, selb_ref, eexp_ref, ocmp_ref, gt_ref,
                     o_ref, m_ref, acc_ref, s_ref, p_ref, a_ref):
    ng = kst_ref.shape[1]
    hpg = q_ref.shape[1] // ng
    tq, d = q_ref.shape[2:]
    rows = hpg * tq
    s0 = pl.program_id(2) * tq
    qpos = s0 + lax.broadcasted_iota(jnp.int32, (tq, 1), 0)
    q4 = [q_ref[0, gi * hpg:(gi + 1) * hpg].reshape(rows, d) for gi in range(ng)]
    q4a = [jnp.concatenate([jnp.concatenate([selb_ref[0, gi]] * hpg, axis=0), q4[gi]], axis=1)
           for gi in range(ng)]

    def scores(gi, t):
        return _bdot(q4a[gi], jnp.concatenate([eexp_ref[t], kst_ref[0, gi, t]], axis=0))

    def flush(gi, t):
        acc_ref[gi] = a_ref[gi] * acc_ref[gi] + _bdot(p_ref[gi], vs_ref[0, gi, t])

    def softmax_step(gi, s):
        m_prev = m_ref[gi]
        m_new = jnp.maximum(m_prev, jnp.max(s, axis=-1, keepdims=True))
        a_ref[gi] = jnp.exp2(m_prev - m_new)
        p_ref[gi] = jnp.exp2(s - m_new).astype(BF16)
        m_ref[gi] = m_new

    m_ref[...] = jnp.full_like(m_ref, NEG)
    acc_ref[...] = jnp.zeros_like(acc_ref)
    p_ref[...] = jnp.zeros_like(p_ref)
    a_ref[...] = jnp.ones_like(a_ref)
    t_diag = s0 // ATT_TK
    for gi in range(ng):
        s_ref[gi] = scores(gi, 0)

    def step(t):
        for gi in range(ng):
            s = s_ref[gi]
            s_next = scores(gi, t + 1)
            flush(gi, jnp.maximum(t - 1, 0))
            softmax_step(gi, s)
            s_ref[gi] = s_next

    def pair_body(i, carry):
        step(2 * i)
        step(2 * i + 1)
        return carry

    lax.fori_loop(0, t_diag // 2, pair_body, 0)
    pl.when(t_diag % 2 == 1)(lambda: step(t_diag - 1))

    span = WINDOW + tq
    w0 = pl.multiple_of(jnp.maximum(s0 - WINDOW, 0), tq)
    rel = qpos - (w0 + lax.broadcasted_iota(jnp.int32, (1, span), 1))
    wbias = jnp.where((rel >= 0) & (rel < WINDOW), 0.0, NEG)
    kpos = t_diag * ATT_TK + lax.broadcasted_iota(jnp.int32, (1, ATT_TK), 1)
    causal = (kpos <= qpos)[None]
    for gi in range(ng):
        sw = _dot_nt(q4[gi], kw_ref[0, gi, pl.ds(w0, span), :])
        flush(gi, jnp.maximum(t_diag - 1, 0))
        s = s_ref[gi].reshape(hpg, tq, ATT_TK)
        softmax_step(gi, jnp.where(causal, s, NEG).reshape(rows, ATT_TK))
        sw = (sw.reshape(hpg, tq, span) + wbias[None]).reshape(rows, span)
        pw = jnp.exp2(sw - jnp.max(sw, axis=-1, keepdims=True)).astype(BF16)
        flush(gi, t_diag)
        accw = _bdot(pw, vw_ref[0, gi, pl.ds(w0, span), :])
        o_win = (accw[:, :d] / accw[:, d:d + 1]).reshape(hpg, tq, d)
        acc = acc_ref[gi]
        o_slc = (acc[:, :d] / acc[:, d:d + 1]).reshape(hpg, tq, d)
        gts = gt_ref[0, :, gi * LANES:(gi + 1) * LANES]
        for h in range(hpg):
            g0, g1, g2 = (gts[:, 3 * h + j:3 * h + j + 1] for j in range(3))
            o_ref[0, gi * hpg + h] = g0 * ocmp_ref[0, gi * hpg + h] + g1 * o_slc[h] + g2 * o_win[h]


def _nsa_attn(q, kst, vst, kw, vw, selb, eexp, o_cmp, gates):
    bsz, _, s, d = q.shape
    ng = ATT_GROUPS
    tq = ATT_TQ
    heads = ng * NSA_HPG
    rows = NSA_HPG * tq
    ns = selb.shape[-1]
    hmap = lambda b, gp, qi: (b, gp, qi, 0)
    once = pl.Buffered(1)
    return pl.pallas_call(
        _nsa_attn_kernel,
        grid=(bsz, NSA_GROUPS // ng, s // tq),
        in_specs=[pl.BlockSpec((1, heads, tq, d), hmap),
                  pl.BlockSpec((1, ng) + kst.shape[2:], lambda b, gp, qi: (b, gp, 0, 0, 0), pipeline_mode=once),
                  pl.BlockSpec((1, ng) + vst.shape[2:], lambda b, gp, qi: (b, gp, 0, 0, 0), pipeline_mode=once),
                  pl.BlockSpec((1, ng) + kw.shape[2:], lambda b, gp, qi: (b, gp, 0, 0), pipeline_mode=once),
                  pl.BlockSpec((1, ng) + vw.shape[2:], lambda b, gp, qi: (b, gp, 0, 0), pipeline_mode=once),
                  pl.BlockSpec((1, ng, tq, ns), hmap),
                  pl.BlockSpec(eexp.shape, lambda b, gp, qi: (0, 0, 0), pipeline_mode=once),
                  pl.BlockSpec((1, heads, tq, d), hmap),
                  pl.BlockSpec((1, tq, ng * LANES), lambda b, gp, qi: (b, qi, gp))],
        out_specs=pl.BlockSpec((1, heads, tq, d), hmap),
        out_shape=jax.ShapeDtypeStruct((bsz, NSA_HEADS, s, d), F32),
        scratch_shapes=[pltpu.VMEM((ng, rows, 1), F32),
                        pltpu.VMEM((ng, rows, 2 * d), F32),
                        pltpu.VMEM((ng, rows, ATT_TK), F32),
                        pltpu.VMEM((ng, rows, ATT_TK), BF16),
                        pltpu.VMEM((ng, rows, 1), F32)],
        compiler_params=_cparams(("arbitrary", "arbitrary", "arbitrary")),
        name="nsa_attn",
    )(q, kst, vst, kw, vw, selb, eexp, o_cmp, gates)


def _rope_tables(pos):
    inv_freq = jnp.power(ROPE_THETA, -jnp.arange(ROT_HALF, dtype=F32) * (2.0 / ROT_DIM))
    ang = pos.astype(F32)[:, None] * inv_freq[None, :]
    cos, sin = jnp.cos(ang), jnp.sin(ang)
    n = pos.shape[0]
    one = jnp.ones((n, NSA_HEAD_DIM - ROT_DIM), F32)
    zero8 = jnp.zeros((n, ROT_HALF), F32)
    zero = jnp.zeros((n, NSA_HEAD_DIM - ROT_DIM), F32)
    c = jnp.concatenate([cos, cos, one], axis=1)
    s1 = jnp.concatenate([-sin, zero8, zero], axis=1)
    s2 = jnp.concatenate([zero8, sin, zero], axis=1)
    rep = LANES // NSA_HEAD_DIM
    return tuple(jnp.tile(t, (1, rep)) for t in (c, s1, s2))


def _importance_weights(ns, nch):
    ratio = SLC_BLOCK // CMP_STRIDE
    span = CMP_BLOCK // CMP_STRIDE
    w = np.zeros((ns, nch), np.float32)
    for j in range(ns):
        for m in range(ratio):
            for n in range(span):
                c = ratio * j + m + n
                if c < nch - 1:
                    w[j, c] += 1.0
    return jnp.asarray(w, dtype=BF16)


def _block_expansion(nt, ns):
    key_blk = (np.arange(nt)[:, None] * ATT_TK + np.arange(ATT_TK)[None, :]) // SLC_BLOCK
    e = (np.arange(ns)[None, :, None] == key_blk[:, None, :]).astype(np.float32)
    return jnp.asarray(e, dtype=BF16)


def _gla_layer(x2, bsz, seq, w_in, w_gate_up, b_gate, norm_g, w_out, ln_g, ln_b, router):
    hk = GLA_HEADS * GLA_DK
    hv = GLA_HEADS * GLA_DV
    cuts = np.cumsum([hk, hk, hv, GLA_GATE_RANK]).tolist()
    wq, wk, wv, wg, wr = jnp.split(w_in, cuts, axis=1)
    w_main = jnp.concatenate([wq, wk, wv, wr], axis=1).astype(BF16)
    w_glow = jnp.pad(wg, ((0, 0), (0, LANES - GLA_GATE_RANK))).astype(BF16)
    w_gu = jnp.pad(w_gate_up, ((0, LANES - GLA_GATE_RANK), (0, 0))).astype(BF16)
    q, k, v, r, la = _gla_proj(x2, w_main, w_glow, w_gu, b_gate.reshape(1, hk))
    sh = lambda t: t.reshape(bsz, seq, t.shape[-1])
    o = _gla_core(sh(q), sh(k), sh(v), sh(la), sh(r), norm_g.reshape(1, GLA_DV))
    return _outproj_ln(o.reshape(bsz * seq, hv), x2, w_out.astype(BF16),
                       ln_g.reshape(1, D_MODEL), ln_b.reshape(1, D_MODEL), router)


def _nsa_layer(x2, bsz, seq, w_in, w_ck1, w_ck2, w_cv1, w_cv2, cmp_pe, w_out, ln_g, ln_b, router):
    h, g, hpg, d = NSA_HEADS, NSA_GROUPS, NSA_HPG, NSA_HEAD_DIM
    gd = g * d
    cuts = np.cumsum([h * d] + [gd] * 6).tolist()
    wq, wkc, wvc, wks, wvs, wkw, wvw, wgt = jnp.split(w_in, cuts, axis=1)
    w_rope = jnp.concatenate([wq, wks, wkw], axis=1).astype(BF16)
    w_plain = jnp.concatenate([wkc, wvc, wvs, wvw], axis=1).astype(BF16)
    w_gates = jnp.pad(wgt.reshape(D_MODEL, g, hpg * 3), ((0, 0), (0, 0), (0, LANES - hpg * 3)))
    w_gates = w_gates.reshape(D_MODEL, g * LANES).astype(BF16)
    rc, rs1, rs2 = _rope_tables(jnp.arange(seq, dtype=jnp.int32))
    q, kst, kw, kc, vc, vs1, vw1, gates = _nsa_proj(x2, w_rope, w_plain, w_gates, rc, rs1, rs2, seq)

    nch = seq // CMP_STRIDE
    chunks = lambda t: t.reshape(bsz, g, nch, CMP_STRIDE * d)
    pe = cmp_pe.reshape(1, CMP_BLOCK * d)
    half = CMP_STRIDE * d
    pad2 = lambda w: jnp.pad(w, ((0, 0), (0, LANES - d))).astype(BF16)
    cc, cs1, cs2 = _rope_tables(jnp.arange(nch, dtype=jnp.int32) * CMP_STRIDE + (CMP_BLOCK - 1))
    k_cmp, v_cmp = _compress(chunks(kc), chunks(vc), pe[:, :half], pe[:, half:],
                             w_ck1.astype(BF16), pad2(w_ck2), w_cv1.astype(BF16), pad2(w_cv2), cc, cs1, cs2)

    ns = seq // SLC_BLOCK
    sh = lambda t: t.reshape(bsz, seq, t.shape[-1])
    o_cmp, selb = _cmp_select(q, k_cmp, v_cmp, _importance_weights(ns, nch))

    nt = seq // ATT_TK
    o = _nsa_attn(q, kst, vs1.reshape(bsz, g, nt, ATT_TK, 2 * d), kw, vw1,
                  selb, _block_expansion(nt, ns), o_cmp, sh(gates))
    return _outproj_heads_ln(o, x2, w_out.astype(BF16), ln_g.reshape(1, D_MODEL), ln_b.reshape(1, D_MODEL), router)


def _moe_layer(xa, grp, w_gate, w_up, w_down, ln_g, ln_b):
    wd = w_down.astype(BF16).reshape(N_EXPERT_GROUPS, EXPERTS_PER_GROUP * D_FF_EXPERT, D_MODEL)
    dest, g_lo, g_hi = _group_sort_plan(grp)
    return _moe_sorted(xa, dest, g_lo, g_hi, w_gate.astype(BF16), w_up.astype(BF16), wd,
                       ln_g.reshape(1, D_MODEL), ln_b.reshape(1, D_MODEL))


def kernel(x, gla_w_in, gla_w_gate_up, gla_b_gate, gla_norm_g, gla_w_out, nsa_w_in, nsa_w_cmp_k1, nsa_w_cmp_k2,
           nsa_w_cmp_v1, nsa_w_cmp_v2, nsa_cmp_pe, nsa_w_out, moe_w_router, moe_b_router, moe_w_gate, moe_w_up,
           moe_w_down, ln_g, ln_b):
    bsz, seq, _ = x.shape
    x2 = x.reshape(bsz * seq, D_MODEL)
    w_router_t = moe_w_router.T
    wr_hi = w_router_t.astype(BF16)
    wr_lo = (w_router_t - wr_hi.astype(F32)).astype(BF16)
    router = (wr_hi, wr_lo, moe_b_router.reshape(N_EXPERTS, 1))
    for i in range(DEPTH):
        j = i // 2
        if i % 2 == 0:
            xa, grp = _gla_layer(x2, bsz, seq, gla_w_in[j], gla_w_gate_up[j], gla_b_gate[j], gla_norm_g[j],
                                 gla_w_out[j], ln_g[i, 0], ln_b[i, 0], router)
        else:
            xa, grp = _nsa_layer(x2, bsz, seq, nsa_w_in[j], nsa_w_cmp_k1[j], nsa_w_cmp_k2[j], nsa_w_cmp_v1[j],
                                 nsa_w_cmp_v2[j], nsa_cmp_pe[j], nsa_w_out[j], ln_g[i, 0], ln_b[i, 0], router)
        x2 = _moe_layer(xa, grp, moe_w_gate[i], moe_w_up[i], moe_w_down[i], ln_g[i, 1], ln_b[i, 1])
    return x2.reshape(bsz, seq, D_MODEL)
```

```python
import functools

import numpy as np
import jax
import jax.numpy as jnp
from jax import lax
from jax.experimental import pallas as pl
from jax.experimental.pallas import tpu as pltpu

F32 = jnp.float32
BF16 = jnp.bfloat16
HIGHEST = lax.Precision.HIGHEST

D_MODEL = 1024
DEPTH = 2

GLA_HEADS = 4
GLA_DK = D_MODEL // 2 // GLA_HEADS
GLA_DV = D_MODEL // GLA_HEADS
GLA_GATE_RANK = 16
GLA_TAU = 16.0
GLA_CHUNK = 64

NSA_HEADS = 16
NSA_GROUPS = 4
NSA_HPG = NSA_HEADS // NSA_GROUPS
NSA_HEAD_DIM = D_MODEL // NSA_HEADS
CMP_BLOCK = 32
CMP_STRIDE = 16
CMP_HIDDEN = 256
SLC_BLOCK = 64
SLC_SHIFT = SLC_BLOCK.bit_length() - 1
SLC_TOP_N = 16
WINDOW = 512

ROPE_THETA = 500000.0
ROT_DIM = NSA_HEAD_DIM // 4
ROT_HALF = ROT_DIM // 2

N_EXPERTS = 16
N_EXPERT_GROUPS = 4
EXPERTS_PER_GROUP = N_EXPERTS // N_EXPERT_GROUPS
D_FF_EXPERT = D_MODEL // 4

DN_ALPHA = (2 * DEPTH) ** 0.25
LN_EPS = 1e-5
NEG = -1e30
BIG = 1e30
LOG2E = 1.4426950408889634

LANES = 128
VMEM_LIMIT = 48 * 1024 * 1024

PROJ_TM = 256
OUT_TM = 512
GLA_T = 512
GLA_HEADS_PER_STEP = 4
MOE_TS = 512
MOE_VMEM_LIMIT = 56 * 1024 * 1024
XA_WIDTH = D_MODEL + LANES
CMP_TQ = 512
CMP_CW = 128
CMP_GROUPS = 2
ATT_TQ = 128
ATT_TK = 512
ATT_GROUPS = 2
WIN_TK = 128


def _cparams(sem):
    return pltpu.CompilerParams(dimension_semantics=sem, vmem_limit_bytes=VMEM_LIMIT)


def _bdot(a, b):
    return jnp.dot(a, b, preferred_element_type=F32)


def _dot_nt(a, b, precision=None):
    return lax.dot_general(a, b, (((1,), (1,)), ((), ())), preferred_element_type=F32, precision=precision)


def _dot_tn(a, b):
    return lax.dot_general(a, b, (((0,), (0,)), ((), ())), preferred_element_type=F32)


def _layer_norm(z, g, b):
    mu = jnp.mean(z, axis=-1, keepdims=True)
    zc = z - mu
    var = jnp.mean(zc * zc, axis=-1, keepdims=True)
    return zc * lax.rsqrt(var + LN_EPS) * g + b


def _silu(t):
    return t * (0.5 * jnp.tanh(0.5 * t) + 0.5)


def _rope128(t, c, s1, s2):
    up = pltpu.roll(t, LANES - ROT_HALF, axis=1)
    dn = pltpu.roll(t, ROT_HALF, axis=1)
    return t * c + up * s1 + dn * s2


def _gla_proj_kernel(x_ref, wm_ref, wg_ref, wgu_ref, bg_ref, q_ref, k_ref, v_ref, r_ref, la_ref):
    xb = x_ref[...].astype(BF16)
    hk = GLA_HEADS * GLA_DK
    hv = GLA_HEADS * GLA_DV
    q_ref[...] = _bdot(xb, wm_ref[:, 0:hk])
    k_ref[...] = _bdot(xb, wm_ref[:, hk:2 * hk])
    v_ref[...] = _bdot(xb, wm_ref[:, 2 * hk:2 * hk + hv])
    r_ref[...] = _bdot(xb, wm_ref[:, 2 * hk + hv:2 * hk + 2 * hv])
    g_low = _bdot(xb, wg_ref[...])
    z = _bdot(g_low.astype(BF16), wgu_ref[...]) + bg_ref[...]
    log_sig = jnp.minimum(z, 0.0) - jnp.log1p(jnp.exp(-jnp.abs(z)))
    la_ref[...] = log_sig * (1.0 / GLA_TAU)


def _gla_proj(x2, w_main, w_glow, w_gu, b_gate):
    n = x2.shape[0]
    hk = GLA_HEADS * GLA_DK
    hv = GLA_HEADS * GLA_DV
    tm = PROJ_TM
    row = lambda i: (i, 0)
    full = lambda i: (0, 0)
    return pl.pallas_call(
        _gla_proj_kernel,
        grid=(n // tm,),
        in_specs=[pl.BlockSpec((tm, D_MODEL), row),
                  pl.BlockSpec(w_main.shape, full),
                  pl.BlockSpec(w_glow.shape, full),
                  pl.BlockSpec(w_gu.shape, full),
                  pl.BlockSpec(b_gate.shape, full)],
        out_specs=[pl.BlockSpec((tm, hk), row), pl.BlockSpec((tm, hk), row),
                   pl.BlockSpec((tm, hv), row), pl.BlockSpec((tm, hv), row),
                   pl.BlockSpec((tm, hk), row)],
        out_shape=[jax.ShapeDtypeStruct((n, hk), F32), jax.ShapeDtypeStruct((n, hk), F32),
                   jax.ShapeDtypeStruct((n, hv), F32), jax.ShapeDtypeStruct((n, hv), F32),
                   jax.ShapeDtypeStruct((n, hk), F32)],
        compiler_params=_cparams(("arbitrary",)),
        name="gla_proj",
    )(x2, w_main, w_glow, w_gu, b_gate)


def _dot_01(m, x):
    x1 = x.astype(BF16)
    r1 = x - x1.astype(F32)
    x2 = r1.astype(BF16)
    x3 = (r1 - x2.astype(F32)).astype(BF16)
    return _bdot(m, x1) + _bdot(m, x2) + _bdot(m, x3)


def _gla_core_kernel(q_ref, k_ref, v_ref, la_ref, r_ref, g_ref, o_ref, st_ref):
    @pl.when(pl.program_id(2) == 0)
    def _():
        st_ref[...] = jnp.zeros_like(st_ref)

    c, t = GLA_CHUNK, GLA_T
    shift = c.bit_length() - 1
    row = lax.broadcasted_iota(jnp.int32, (t, t), 0)
    col = lax.broadcasted_iota(jnp.int32, (t, t), 1)
    causal = ((row >> shift) == (col >> shift)) & (row >= col)
    in_chunk = lax.broadcasted_iota(jnp.int32, (t, GLA_DK), 0) & (c - 1)
    chunks = [slice(ci * c, (ci + 1) * c) for ci in range(t // c)]
    for hh in range(st_ref.shape[0]):
        kcols = slice(hh * GLA_DK, (hh + 1) * GLA_DK)
        vcols = slice(hh * GLA_DV, (hh + 1) * GLA_DV)
        b = la_ref[0, :, kcols]
        step = 1
        while step < c:
            b = b + jnp.where(in_chunk >= step, pltpu.roll(b, step, axis=0), 0.0)
            step *= 2
        b_last = jnp.broadcast_to(b.reshape(t // c, c, GLA_DK)[:, c - 1:c, :],
                                  (t // c, c, GLA_DK)).reshape(t, GLA_DK)
        q = q_ref[0, :, kcols] * (GLA_DK ** -0.5)
        k = k_ref[0, :, kcols]
        vb = v_ref[0, :, vcols].astype(BF16)
        q_dec = (q * jnp.exp(b)).astype(BF16)
        k_neg = (k * jnp.exp(-b)).astype(BF16)
        k_dec = (k * jnp.exp(b_last - b)).astype(BF16)
        decay = jnp.exp(b_last)
        s = jnp.where(causal, _dot_nt(q_dec, k_neg), 0.0)
        o = _bdot(s.astype(BF16), vb)

        kv = [_dot_tn(vb[rows], k_dec[rows]) for rows in chunks]
        st = st_ref[hh]
        entering = []
        for ci, rows in enumerate(chunks):
            entering.append(st.astype(BF16))
            st = decay[ci * c:ci * c + 1] * st + kv[ci]
        st_ref[hh] = st
        o = o + jnp.concatenate([_dot_nt(q_dec[rows], s_in) for rows, s_in in zip(chunks, entering)], axis=0)
        ms = jnp.mean(o * o, axis=-1, keepdims=True)
        o = o * lax.rsqrt(ms + LN_EPS) * g_ref[...]
        o_ref[0, :, vcols] = o * _silu(r_ref[0, :, vcols])


def _gla_core(q, k, v, la, r, norm_g):
    bsz, s, _ = q.shape
    t = GLA_T
    hs = GLA_HEADS_PER_STEP
    kmap = lambda b, h, n: (b, n, h)
    kspec = pl.BlockSpec((1, t, hs * GLA_DK), kmap)
    vspec = pl.BlockSpec((1, t, hs * GLA_DV), kmap)
    return pl.pallas_call(
        _gla_core_kernel,
        grid=(bsz, GLA_HEADS // hs, s // t),
        in_specs=[kspec, kspec, vspec, kspec, vspec,
                  pl.BlockSpec((1, GLA_DV), lambda b, h, n: (0, 0))],
        out_specs=vspec,
        out_shape=jax.ShapeDtypeStruct((bsz, s, GLA_HEADS * GLA_DV), F32),
        scratch_shapes=[pltpu.VMEM((hs, GLA_DV, GLA_DK), F32)],
        compiler_params=_cparams(("arbitrary", "arbitrary", "arbitrary")),
        name="gla_core",
    )(q, k, v, la, r, norm_g)


def _row_xor(v, k, row):
    n = v.shape[0]
    up = pltpu.roll(v, n - k, axis=0)
    dn = pltpu.roll(v, k, axis=0)
    return jnp.where((row & k) == 0, up, dn)


def _route(x, wt_hi, wt_lo, bias):
    x_hi = x.astype(BF16)
    x_lo = (x - x_hi.astype(F32)).astype(BF16)
    logits = _dot_nt(wt_hi, x_hi) + (_dot_nt(wt_lo, x_hi) + _dot_nt(wt_hi, x_lo))
    s = jax.nn.sigmoid(logits)
    a = s + bias
    row = lax.broadcasted_iota(jnp.int32, a.shape, 0)
    a1 = _row_xor(a, 1, row)
    a2 = _row_xor(a, 2, row)
    a3 = _row_xor(a1, 2, row)
    p, q = jnp.maximum(a, a1), jnp.minimum(a, a1)
    r, t = jnp.maximum(a2, a3), jnp.minimum(a2, a3)
    gs = jnp.maximum(p, r) + jnp.maximum(jnp.minimum(p, r), jnp.maximum(q, t))
    g1 = _row_xor(gs, 4, row)
    g2 = _row_xor(gs, 8, row)
    g3 = _row_xor(g1, 8, row)

    def beats(other, other_first):
        return (other > gs) | ((other == gs) & other_first)

    lose = (beats(g1, (row & 4) != 0) | beats(g2, (row & 8) != 0) | beats(g3, (row & 8) != 0))

    def ahead(other, other_first):
        return jnp.where((other > a) | ((other == a) & other_first), 1.0, 0.0)

    rank = ahead(a1, (row & 1) != 0) + ahead(a2, (row & 2) != 0) + ahead(a3, (row & 2) != 0)
    sel = jnp.logical_not(lose) & (rank < 2.0)
    ssel = jnp.where(sel, s, 0.0)
    s1 = _row_xor(ssel, 1, row)
    tot = (ssel + s1) + (_row_xor(ssel, 2, row) + _row_xor(s1, 2, row))
    gate = jnp.where(sel, s / tot, 0.0)
    group = jnp.max(jnp.where(sel, (row >> 2).astype(F32), 0.0), axis=0, keepdims=True)
    tokens = x.shape[0]
    record = jnp.concatenate([gate, jnp.zeros((LANES - N_EXPERTS, tokens), F32)], axis=0)
    return record.T, group.astype(jnp.int32)


def _norm_route_store(z, g_ref, b_ref, wrh_ref, wrl_ref, br_ref, o_ref, grp_ref):
    x = _layer_norm(z, g_ref[...], b_ref[...])
    o_ref[:, :D_MODEL] = x
    o_ref[:, D_MODEL:], grp_ref[...] = _route(x, wrh_ref[...], wrl_ref[...], br_ref[...])


def _outproj_ln_kernel(h_ref, x_ref, w_ref, g_ref, b_ref, wrh_ref, wrl_ref, br_ref, o_ref, grp_ref):
    y = _bdot(h_ref[...].astype(BF16), w_ref[...])
    _norm_route_store(DN_ALPHA * x_ref[...] + y, g_ref, b_ref, wrh_ref, wrl_ref, br_ref, o_ref, grp_ref)


def _outproj_heads_ln_kernel(h_ref, x_ref, w_ref, g_ref, b_ref, wrh_ref, wrl_ref, br_ref, o_ref, grp_ref):
    h = jnp.concatenate([h_ref[0, i].astype(BF16) for i in range(h_ref.shape[1])], axis=1)
    _norm_route_store(DN_ALPHA * x_ref[...] + _bdot(h, w_ref[...]), g_ref, b_ref, wrh_ref, wrl_ref, br_ref,
                      o_ref, grp_ref)


def _outproj_call(kernel, name, h, h_spec, x2, w, g, b, router):
    n = x2.shape[0]
    tm = OUT_TM
    row = lambda i: (i, 0)
    full = lambda i: (0, 0)
    return pl.pallas_call(
        kernel,
        grid=(n // tm,),
        in_specs=[h_spec, pl.BlockSpec((tm, D_MODEL), row),
                  pl.BlockSpec(w.shape, full), pl.BlockSpec((1, D_MODEL), full), pl.BlockSpec((1, D_MODEL), full)]
                 + [pl.BlockSpec(r.shape, full) for r in router],
        out_specs=[pl.BlockSpec((tm, XA_WIDTH), row), pl.BlockSpec((1, tm), lambda i: (0, i))],
        out_shape=[jax.ShapeDtypeStruct((n, XA_WIDTH), F32), jax.ShapeDtypeStruct((1, n), jnp.int32)],
        compiler_params=_cparams(("arbitrary",)),
        name=name,
    )(h, x2, w, g, b, *router)


def _outproj_heads_ln(h4, x2, w, g, b, router):
    _, heads, seq, d = h4.shape
    per = seq // OUT_TM
    spec = pl.BlockSpec((1, heads, OUT_TM, d), lambda i: (i // per, 0, i % per, 0))
    return _outproj_call(_outproj_heads_ln_kernel, "outproj_heads_ln", h4, spec, x2, w, g, b, router)


def _outproj_ln(h2, x2, w, g, b, router):
    spec = pl.BlockSpec((OUT_TM, h2.shape[1]), lambda i: (i, 0))
    return _outproj_call(_outproj_ln_kernel, "outproj_ln", h2, spec, x2, w, g, b, router)


def _moe_sorted_kernel(dest_ref, glo_ref, ghi_ref, xa_hbm, wg_ref, wu_ref, wd_ref, lg_ref, lb_ref, out_hbm,
                       src_ref, xbuf, obuf, in_sem, out_sem):
    i = pl.program_id(0)
    last = pl.num_programs(0) - 1
    nt = 2 * pl.num_programs(0)
    ts = MOE_TS
    epg = EXPERTS_PER_GROUP

    def row_in(tile, sl, r):
        return pltpu.make_async_copy(xa_hbm.at[src_ref[tile * ts + r]], xbuf.at[sl, r], in_sem.at[sl])

    def row_out(tile, sl, r):
        return pltpu.make_async_copy(obuf.at[sl, r], out_hbm.at[src_ref[tile * ts + r]], out_sem.at[sl])

    def wait_in(sl):
        pltpu.make_async_copy(xa_hbm.at[pl.ds(0, ts)], xbuf.at[sl], in_sem.at[sl]).wait()

    def wait_out(sl):
        pltpu.make_async_copy(obuf.at[sl], out_hbm.at[pl.ds(0, ts)], out_sem.at[sl]).wait()

    @pl.when(i == 0)
    def _():
        def invert(n, carry):
            src_ref[dest_ref[n]] = n
            return carry

        lax.fori_loop(0, dest_ref.shape[0], invert, 0, unroll=8)
        for r in range(ts):
            row_in(0, 0, r).start()

    @pl.when(i >= 1)
    def _():
        wait_out(0)
        wait_out(1)

    for sl in range(2):
        tile = 2 * i + sl
        wait_in(sl)
        nxt = jnp.minimum(tile + 1, nt - 1)
        for r in range(ts):
            row_in(nxt, 1 - sl, r).start()

        xa = xbuf[sl]
        x = xa[:, :D_MODEL]
        gate = xa[:, D_MODEL:]
        xb = x.astype(BF16)

        def group_ffn(g, xb=xb, gate=gate):
            hs = []
            for e in range(epg):
                col = gate[:, e:e + 1]
                for other in range(1, N_EXPERT_GROUPS):
                    col = jnp.where(g == other, gate[:, other * epg + e:other * epg + e + 1], col)
                expert = g * epg + e
                hs.append((_silu(_bdot(xb, wg_ref[expert])) * _bdot(xb, wu_ref[expert]) * col).astype(BF16))
            return _bdot(jnp.concatenate(hs, axis=1), wd_ref[g])

        y = group_ffn(glo_ref[tile])
        y = lax.fori_loop(glo_ref[tile] + 1, ghi_ref[tile] + 1, lambda g, acc: acc + group_ffn(g), y)
        obuf[sl] = _layer_norm(DN_ALPHA * x + y, lg_ref[...], lb_ref[...])
        for r in range(ts):
            row_out(tile, sl, r).start()

    @pl.when(i == last)
    def _():
        wait_out(0)
        wait_out(1)
        wait_in(0)


def _moe_sorted(xa, dest, g_lo, g_hi, wg, wu, wd, lg, lb):
    n = xa.shape[0]
    ts = MOE_TS
    resident = pl.Buffered(1)
    whole = lambda i, *_: (0, 0, 0)
    grid_spec = pltpu.PrefetchScalarGridSpec(
        num_scalar_prefetch=3,
        grid=(n // (2 * ts),),
        in_specs=[pl.BlockSpec(memory_space=pl.ANY),
                  pl.BlockSpec(wg.shape, whole, pipeline_mode=resident),
                  pl.BlockSpec(wu.shape, whole, pipeline_mode=resident),
                  pl.BlockSpec(wd.shape, whole, pipeline_mode=resident),
                  pl.BlockSpec((1, D_MODEL), lambda i, *_: (0, 0)),
                  pl.BlockSpec((1, D_MODEL), lambda i, *_: (0, 0))],
        out_specs=pl.BlockSpec(memory_space=pl.ANY),
        scratch_shapes=[pltpu.SMEM((n,), jnp.int32),
                        pltpu.VMEM((2, ts, XA_WIDTH), F32), pltpu.VMEM((2, ts, D_MODEL), F32),
                        pltpu.SemaphoreType.DMA((2,)), pltpu.SemaphoreType.DMA((2,))])
    return pl.pallas_call(
        _moe_sorted_kernel,
        grid_spec=grid_spec,
        out_shape=jax.ShapeDtypeStruct((n, D_MODEL), F32),
        compiler_params=pltpu.CompilerParams(dimension_semantics=("arbitrary",), vmem_limit_bytes=MOE_VMEM_LIMIT),
        name="moe_sorted",
    )(dest, g_lo, g_hi, xa, wg, wu, wd, lg, lb)


def _cast_experts_kernel(wg_ref, wu_ref, wd_ref, og_ref, ou_ref, od_ref):
    og_ref[...] = wg_ref[...].astype(BF16)
    ou_ref[...] = wu_ref[...].astype(BF16)
    od_ref[...] = wd_ref[...].astype(BF16)


def _cast_experts(w_gate, w_up, w_down):
    spec = lambda w: pl.BlockSpec((1,) + w.shape[1:], lambda e: (e, 0, 0))
    ws = (w_gate, w_up, w_down)
    return pl.pallas_call(
        _cast_experts_kernel,
        grid=(w_gate.shape[0],),
        in_specs=[spec(w) for w in ws],
        out_specs=[spec(w) for w in ws],
        out_shape=[jax.ShapeDtypeStruct(w.shape, BF16) for w in ws],
        compiler_params=_cparams(("arbitrary",)),
        name="cast_experts",
    )(*ws)


def _group_sort_plan(grp):
    n = grp.shape[1]
    onehot = (grp == jnp.arange(N_EXPERT_GROUPS, dtype=jnp.int32)[:, None]).astype(jnp.int32)
    csum = jnp.cumsum(onehot, axis=1)
    ends = jnp.cumsum(csum[:, -1])
    dest = jnp.sum(onehot * (csum - 1 + (ends - csum[:, -1])[:, None]), axis=0)
    first_row = jnp.arange(n // MOE_TS, dtype=jnp.int32) * MOE_TS
    group_of_row = lambda rows: jnp.sum(ends[None, :] <= rows[:, None], axis=1).astype(jnp.int32)
    return dest, group_of_row(first_row), group_of_row(first_row + (MOE_TS - 1))


def _nsa_proj_kernel(x_ref, wr_ref, wp_ref, wgt_ref, c_ref, s1_ref, s2_ref,
                     q_ref, kst_ref, kw_ref, kc_ref, vc_ref, vs_ref, vw_ref, gt_ref):
    xb = x_ref[...].astype(BF16)
    c, s1, s2 = c_ref[...], s1_ref[...], s2_ref[...]
    d = NSA_HEAD_DIM
    nq = NSA_HEADS * d // LANES
    ng = NSA_GROUPS * d // LANES
    first_head = lax.broadcasted_iota(jnp.int32, (x_ref.shape[0], LANES), 1) < d

    def put_rows(ref, j, t):
        ref[0, 2 * j] = t[:, :d].astype(ref.dtype)
        ref[0, 2 * j + 1] = t[:, d:].astype(ref.dtype)

    def put_with_ones(ref, j, t):
        ref[0, 2 * j] = jnp.where(first_head, t, 1.0).astype(ref.dtype)
        ref[0, 2 * j + 1] = jnp.where(first_head, pltpu.roll(t, d, axis=1), 1.0).astype(ref.dtype)

    for j in range(nq + 2 * ng):
        t = _rope128(_bdot(xb, wr_ref[:, j * LANES:(j + 1) * LANES]), c, s1, s2)
        if j < nq:
            put_rows(q_ref, j, t * (d ** -0.5 * LOG2E))
        elif j < nq + ng:
            tt = t.T.astype(BF16)
            kst_ref[0, 2 * (j - nq), 0] = tt[:d]
            kst_ref[0, 2 * (j - nq) + 1, 0] = tt[d:]
        else:
            put_rows(kw_ref, j - nq - ng, t)
    for idx, (ref, put) in enumerate(((kc_ref, put_rows), (vc_ref, put_rows),
                                      (vs_ref, put_with_ones), (vw_ref, put_with_ones))):
        for j in range(ng):
            col = (idx * ng + j) * LANES
            put(ref, j, _bdot(xb, wp_ref[:, col:col + LANES]))
    gt_ref[...] = jax.nn.sigmoid(_bdot(xb, wgt_ref[...]))


def _nsa_proj(x2, w_rope, w_plain, w_gates, rc, rs1, rs2, seq):
    n = x2.shape[0]
    tm = PROJ_TM
    gd = NSA_GROUPS * NSA_HEAD_DIM
    row = lambda i: (i, 0)
    full = lambda i: (0, 0)
    per = seq // tm
    pos = lambda i: (i % per, 0)
    bsz, g, d = n // seq, NSA_GROUPS, NSA_HEAD_DIM
    sub = ATT_TK // tm
    hmap = lambda i: (i // per, 0, i % per, 0)

    def rows_out(heads, width, dtype):
        return pl.BlockSpec((1, heads, tm, width), hmap), jax.ShapeDtypeStruct((bsz, heads, seq, width), dtype)

    outs = [rows_out(NSA_HEADS, d, BF16),
            (pl.BlockSpec((1, g, 1, d, tm), lambda i: (i // per, 0, (i % per) // sub, 0, (i % per) % sub)),
             jax.ShapeDtypeStruct((bsz, g, seq // ATT_TK, d, ATT_TK), BF16)),
            rows_out(g, d, BF16),
            rows_out(g, d, F32), rows_out(g, d, F32),
            rows_out(g, 2 * d, BF16), rows_out(g, 2 * d, BF16),
            (pl.BlockSpec((tm, g * LANES), row), jax.ShapeDtypeStruct((n, g * LANES), F32))]
    return pl.pallas_call(
        _nsa_proj_kernel,
        grid=(n // tm,),
        in_specs=[pl.BlockSpec((tm, D_MODEL), row),
                  pl.BlockSpec(w_rope.shape, full), pl.BlockSpec(w_plain.shape, full),
                  pl.BlockSpec(w_gates.shape, full),
                  pl.BlockSpec((tm, LANES), pos), pl.BlockSpec((tm, LANES), pos), pl.BlockSpec((tm, LANES), pos)],
        out_specs=[o[0] for o in outs],
        out_shape=[o[1] for o in outs],
        compiler_params=_cparams(("arbitrary",)),
        name="nsa_proj",
    )(x2, w_rope, w_plain, w_gates, rc, rs1, rs2)


def _compress_kernel(ck_ref, cv_ref, pet_ref, peb_ref, wk1_ref, wk2_ref, wv1_ref, wv2_ref,
                     c_ref, s1_ref, s2_ref, ko_ref, vo_ref):
    nch = ck_ref.shape[2]
    half = CMP_STRIDE * NSA_HEAD_DIM
    pet, peb = pet_ref[...], peb_ref[...]

    def mlp(ch, w1_ref, w2_ref):
        a = _bdot((ch + pet).astype(BF16), w1_ref[0:half, :])
        bm = _bdot((ch + peb).astype(BF16), w1_ref[half:2 * half, :])
        h = _silu(a + pltpu.roll(bm, nch - 1, axis=0))
        return _bdot(h.astype(BF16), w2_ref[...])

    kc = _rope128(mlp(ck_ref[0, 0], wk1_ref, wk2_ref), c_ref[...], s1_ref[...], s2_ref[...])
    ko_ref[0, 0] = kc[:, :NSA_HEAD_DIM].astype(BF16)
    vo_ref[0, 0] = mlp(cv_ref[0, 0], wv1_ref, wv2_ref)[:, :NSA_HEAD_DIM].astype(BF16)


def _compress(ck, cv, pet, peb, wk1, wk2, wv1, wv2, cc, cs1, cs2):
    bsz, g, nch, width = ck.shape
    blk = lambda b, gi: (b, gi, 0, 0)
    full = lambda b, gi: (0, 0)
    return pl.pallas_call(
        _compress_kernel,
        grid=(bsz, g),
        in_specs=[pl.BlockSpec((1, 1, nch, width), blk), pl.BlockSpec((1, 1, nch, width), blk),
                  pl.BlockSpec(pet.shape, full), pl.BlockSpec(peb.shape, full),
                  pl.BlockSpec(wk1.shape, full), pl.BlockSpec(wk2.shape, full),
                  pl.BlockSpec(wv1.shape, full), pl.BlockSpec(wv2.shape, full),
                  pl.BlockSpec(cc.shape, full), pl.BlockSpec(cs1.shape, full), pl.BlockSpec(cs2.shape, full)],
        out_specs=[pl.BlockSpec((1, 1, nch, NSA_HEAD_DIM), blk), pl.BlockSpec((1, 1, nch, NSA_HEAD_DIM), blk)],
        out_shape=[jax.ShapeDtypeStruct((bsz, g, nch, NSA_HEAD_DIM), BF16)] * 2,
        compiler_params=_cparams(("arbitrary", "arbitrary")),
        name="nsa_compress",
    )(ck, cv, pet, peb, wk1, wk2, wv1, wv2, cc, cs1, cs2)


def _cmp_select_kernel(q_ref, kc_ref, vc_ref, wimp_ref, o_ref, sel_ref):
    tq = q_ref.shape[2]
    nch = kc_ref.shape[2]
    ns = wimp_ref.shape[0]
    s0 = pl.program_id(2) * tq
    ratio = SLC_BLOCK // CMP_STRIDE
    n_vis = (s0 + tq - CMP_BLOCK) // CMP_STRIDE + 1
    n_chunks = (n_vis + CMP_CW - 1) // CMP_CW

    def body(w):
        nb = w // ratio
        qpos_c = s0 + lax.broadcasted_iota(jnp.int32, (tq, 1), 0)
        cend = lax.broadcasted_iota(jnp.int32, (1, w), 1) * CMP_STRIDE + (CMP_BLOCK - 1)
        cmask = cend <= qpos_c
        anyvis = (qpos_c >= CMP_BLOCK - 1).astype(F32)
        blk = lax.broadcasted_iota(jnp.int32, (nb, tq), 0)
        cur = (s0 + lax.broadcasted_iota(jnp.int32, (nb, tq), 1)) >> SLC_SHIFT
        forced = (blk == 0) | (blk == cur) | (blk == cur - 1)
        visible = blk <= cur
        for gi in range(kc_ref.shape[1]):
            group_body(gi, w, nb, cmask, anyvis, forced, visible)

    def group_body(gi, w, nb, cmask, anyvis, forced, visible):
        d = q_ref.shape[3]
        hpg = q_ref.shape[1] // kc_ref.shape[1]
        heads = slice(gi * hpg, (gi + 1) * hpg)
        s = _dot_nt(q_ref[0, heads].reshape(hpg * tq, d), kc_ref[0, gi, :w, :]).reshape(hpg, tq, w)
        s = jnp.where(cmask[None], s, NEG)
        e = jnp.exp2(s - jnp.max(s, axis=-1, keepdims=True))
        p = e * (anyvis[None] / jnp.sum(e, axis=-1, keepdims=True))
        o_ref[0, heads] = _bdot(p.reshape(hpg * tq, w).astype(BF16), vc_ref[0, gi, :w, :]).reshape(hpg, tq, d)
        pg = jnp.sum(p, axis=0)

        pg_hi = pg.astype(BF16)
        pg_lo = (pg - pg_hi.astype(F32)).astype(BF16)
        wimp = wimp_ref[:nb, :w]
        imp = _dot_nt(wimp, pg_hi) + _dot_nt(wimp, pg_lo)
        imp = jnp.where(visible, jnp.where(forced, BIG, imp), NEG)
        parts = [slice(i, i + LANES) for i in range(0, tq, LANES)]
        blk_f = lax.broadcasted_iota(jnp.int32, (nb, LANES), 0).astype(F32)
        imps = [imp[:, pt] for pt in parts]
        sels = [jnp.zeros((nb, LANES), F32) for _ in parts]
        for _ in range(min(SLC_TOP_N, nb)):
            for i in range(len(parts)):
                m = jnp.max(imps[i], axis=0, keepdims=True)
                first = jnp.min(jnp.where(imps[i] == m, blk_f, float(nb)), axis=0, keepdims=True)
                hit = blk_f == first
                sels[i] = jnp.where(hit, 1.0, sels[i])
                imps[i] = jnp.where(hit, -jnp.inf, imps[i])
        sel = jnp.concatenate(sels, axis=1)
        if nb < ns:
            sel = jnp.concatenate([sel, jnp.zeros((ns - nb, tq), F32)], axis=0)
        sel_ref[0, gi] = jnp.where(sel.T > 0.0, 0.0, NEG).astype(BF16)

    for nv in range(1, nch // CMP_CW + 1):
        pl.when(n_chunks == nv)(functools.partial(body, nv * CMP_CW))


def _cmp_select(q, k_cmp, v_cmp, wimp_t):
    bsz, _, s, d = q.shape
    g = NSA_GROUPS
    ng = CMP_GROUPS
    nch = k_cmp.shape[2]
    ns = wimp_t.shape[0]
    tq = min(CMP_TQ, s)
    hmap = lambda b, gp, qi: (b, gp, qi, 0)
    return pl.pallas_call(
        _cmp_select_kernel,
        grid=(bsz, g // ng, s // tq),
        in_specs=[pl.BlockSpec((1, ng * NSA_HPG, tq, d), hmap),
                  pl.BlockSpec((1, ng, nch, d), lambda b, gp, qi: (b, gp, 0, 0)),
                  pl.BlockSpec((1, ng, nch, d), lambda b, gp, qi: (b, gp, 0, 0)),
                  pl.BlockSpec(wimp_t.shape, lambda b, gp, qi: (0, 0))],
        out_specs=[pl.BlockSpec((1, ng * NSA_HPG, tq, d), hmap),
                   pl.BlockSpec((1, ng, tq, ns), hmap)],
        out_shape=[jax.ShapeDtypeStruct((bsz, NSA_HEADS, s, d), F32),
                   jax.ShapeDtypeStruct((bsz, g, s, ns), BF16)],
        compiler_params=_cparams(("arbitrary", "arbitrary", "arbitrary")),
        name="nsa_cmp_select",
    )(q, k_cmp, v_cmp, wimp_t)


def _nsa_attn_kernel(q_ref, kst_ref, vs_ref, kw_ref, vw_ref, selb_ref, eexp_ref, ocmp_ref, gt_ref,
                     o_ref, m_ref, acc_ref, s_ref, p_ref, a_ref):
    ng = kst_ref.shape[1]
    hpg = q_ref.shape[1] // ng
    tq, d = q_ref.shape[2:]
    rows = hpg * tq
    s0 = pl.program_id(2) * tq
    qpos = s0 + lax.broadcasted_iota(jnp.int32, (tq, 1), 0)
    q4 = [q_ref[0, gi * hpg:(gi + 1) * hpg].reshape(rows, d) for gi in range(ng)]
    q4a = [jnp.concatenate([jnp.concatenate([selb_ref[0, gi]] * hpg, axis=0), q4[gi]], axis=1)
           for gi in range(ng)]

    def scores(gi, t):
        return _bdot(q4a[gi], jnp.concatenate([eexp_ref[t], kst_ref[0, gi, t]], axis=0))

    def flush(gi, t):
        acc_ref[gi] = a_ref[gi] * acc_ref[gi] + _bdot(p_ref[gi], vs_ref[0, gi, t])

    def softmax_step(gi, s):
        m_prev = m_ref[gi]
        m_new = jnp.maximum(m_prev, jnp.max(s, axis=-1, keepdims=True))
        a_ref[gi] = jnp.exp2(m_prev - m_new)
        p_ref[gi] = jnp.exp2(s - m_new).astype(BF16)
        m_ref[gi] = m_new

    m_ref[...] = jnp.full_like(m_ref, NEG)
    acc_ref[...] = jnp.zeros_like(acc_ref)
    p_ref[...] = jnp.zeros_like(p_ref)
    a_ref[...] = jnp.ones_like(a_ref)
    t_diag = s0 // ATT_TK
    for gi in range(ng):
        s_ref[gi] = scores(gi, 0)

    def step(t):
        for gi in range(ng):
            s = s_ref[gi]
            s_next = scores(gi, t + 1)
            flush(gi, jnp.maximum(t - 1, 0))
            softmax_step(gi, s)
            s_ref[gi] = s_next

    def pair_body(i, carry):
        step(2 * i)
        step(2 * i + 1)
        return carry

    lax.fori_loop(0, t_diag // 2, pair_body, 0)
    pl.when(t_diag % 2 == 1)(lambda: step(t_diag - 1))

    span = WINDOW + tq
    w0 = pl.multiple_of(jnp.maximum(s0 - WINDOW, 0), tq)
    rel = qpos - (w0 + lax.broadcasted_iota(jnp.int32, (1, span), 1))
    wbias = jnp.where((rel >= 0) & (rel < WINDOW), 0.0, NEG)
    kpos = t_diag * ATT_TK + lax.broadcasted_iota(jnp.int32, (1, ATT_TK), 1)
    causal = (kpos <= qpos)[None]
    for gi in range(ng):
        sw = _dot_nt(q4[gi], kw_ref[0, gi, pl.ds(w0, span), :])
        flush(gi, jnp.maximum(t_diag - 1, 0))
        s = s_ref[gi].reshape(hpg, tq, ATT_TK)
        softmax_step(gi, jnp.where(causal, s, NEG).reshape(rows, ATT_TK))
        sw = (sw.reshape(hpg, tq, span) + wbias[None]).reshape(rows, span)
        pw = jnp.exp2(sw - jnp.max(sw, axis=-1, keepdims=True)).astype(BF16)
        flush(gi, t_diag)
        accw = _bdot(pw, vw_ref[0, gi, pl.ds(w0, span), :])
        o_win = (accw[:, :d] / accw[:, d:d + 1]).reshape(hpg, tq, d)
        acc = acc_ref[gi]
        o_slc = (acc[:, :d] / acc[:, d:d + 1]).reshape(hpg, tq, d)
        gts = gt_ref[0, :, gi * LANES:(gi + 1) * LANES]
        for h in range(hpg):
            g0, g1, g2 = (gts[:, 3 * h + j:3 * h + j + 1] for j in range(3))
            o_ref[0, gi * hpg + h] = g0 * ocmp_ref[0, gi * hpg + h] + g1 * o_slc[h] + g2 * o_win[h]


def _nsa_attn(q, kst, vst, kw, vw, selb, eexp, o_cmp, gates):
    bsz, _, s, d = q.shape
    ng = ATT_GROUPS
    tq = ATT_TQ
    heads = ng * NSA_HPG
    rows = NSA_HPG * tq
    ns = selb.shape[-1]
    hmap = lambda b, gp, qi: (b, gp, qi, 0)
    once = pl.Buffered(1)
    return pl.pallas_call(
        _nsa_attn_kernel,
        grid=(bsz, NSA_GROUPS // ng, s // tq),
        in_specs=[pl.BlockSpec((1, heads, tq, d), hmap),
                  pl.BlockSpec((1, ng) + kst.shape[2:], lambda b, gp, qi: (b, gp, 0, 0, 0), pipeline_mode=once),
                  pl.BlockSpec((1, ng) + vst.shape[2:], lambda b, gp, qi: (b, gp, 0, 0, 0), pipeline_mode=once),
                  pl.BlockSpec((1, ng) + kw.shape[2:], lambda b, gp, qi: (b, gp, 0, 0), pipeline_mode=once),
                  pl.BlockSpec((1, ng) + vw.shape[2:], lambda b, gp, qi: (b, gp, 0, 0), pipeline_mode=once),
                  pl.BlockSpec((1, ng, tq, ns), hmap),
                  pl.BlockSpec(eexp.shape, lambda b, gp, qi: (0, 0, 0), pipeline_mode=once),
                  pl.BlockSpec((1, heads, tq, d), hmap),
                  pl.BlockSpec((1, tq, ng * LANES), lambda b, gp, qi: (b, qi, gp))],
        out_specs=pl.BlockSpec((1, heads, tq, d), hmap),
        out_shape=jax.ShapeDtypeStruct((bsz, NSA_HEADS, s, d), F32),
        scratch_shapes=[pltpu.VMEM((ng, rows, 1), F32),
                        pltpu.VMEM((ng, rows, 2 * d), F32),
                        pltpu.VMEM((ng, rows, ATT_TK), F32),
                        pltpu.VMEM((ng, rows, ATT_TK), BF16),
                        pltpu.VMEM((ng, rows, 1), F32)],
        compiler_params=_cparams(("arbitrary", "arbitrary", "arbitrary")),
        name="nsa_attn",
    )(q, kst, vst, kw, vw, selb, eexp, o_cmp, gates)


def _rope_tables(pos):
    inv_freq = jnp.power(ROPE_THETA, -jnp.arange(ROT_HALF, dtype=F32) * (2.0 / ROT_DIM))
    ang = pos.astype(F32)[:, None] * inv_freq[None, :]
    cos, sin = jnp.cos(ang), jnp.sin(ang)
    n = pos.shape[0]
    one = jnp.ones((n, NSA_HEAD_DIM - ROT_DIM), F32)
    zero8 = jnp.zeros((n, ROT_HALF), F32)
    zero = jnp.zeros((n, NSA_HEAD_DIM - ROT_DIM), F32)
    c = jnp.concatenate([cos, cos, one], axis=1)
    s1 = jnp.concatenate([-sin, zero8, zero], axis=1)
    s2 = jnp.concatenate([zero8, sin, zero], axis=1)
    rep = LANES // NSA_HEAD_DIM
    return tuple(jnp.tile(t, (1, rep)) for t in (c, s1, s2))


def _importance_weights(ns, nch):
    ratio = SLC_BLOCK // CMP_STRIDE
    span = CMP_BLOCK // CMP_STRIDE
    w = np.zeros((ns, nch), np.float32)
    for j in range(ns):
        for m in range(ratio):
            for n in range(span):
                c = ratio * j + m + n
                if c < nch - 1:
                    w[j, c] += 1.0
    return jnp.asarray(w, dtype=BF16)


def _block_expansion(nt, ns):
    key_blk = (np.arange(nt)[:, None] * ATT_TK + np.arange(ATT_TK)[None, :]) // SLC_BLOCK
    e = (np.arange(ns)[None, :, None] == key_blk[:, None, :]).astype(np.float32)
    return jnp.asarray(e, dtype=BF16)


def _gla_layer(x2, bsz, seq, w_in, w_gate_up, b_gate, norm_g, w_out, ln_g, ln_b, router):
    hk = GLA_HEADS * GLA_DK
    hv = GLA_HEADS * GLA_DV
    cuts = np.cumsum([hk, hk, hv, GLA_GATE_RANK]).tolist()
    wq, wk, wv, wg, wr = jnp.split(w_in, cuts, axis=1)
    w_main = jnp.concatenate([wq, wk, wv, wr], axis=1).astype(BF16)
    w_glow = jnp.pad(wg, ((0, 0), (0, LANES - GLA_GATE_RANK))).astype(BF16)
    w_gu = jnp.pad(w_gate_up, ((0, LANES - GLA_GATE_RANK), (0, 0))).astype(BF16)
    q, k, v, r, la = _gla_proj(x2, w_main, w_glow, w_gu, b_gate.reshape(1, hk))
    sh = lambda t: t.reshape(bsz, seq, t.shape[-1])
    o = _gla_core(sh(q), sh(k), sh(v), sh(la), sh(r), norm_g.reshape(1, GLA_DV))
    return _outproj_ln(o.reshape(bsz * seq, hv), x2, w_out.astype(BF16),
                       ln_g.reshape(1, D_MODEL), ln_b.reshape(1, D_MODEL), router)


def _nsa_layer(x2, bsz, seq, w_in, w_ck1, w_ck2, w_cv1, w_cv2, cmp_pe, w_out, ln_g, ln_b, router):
    h, g, hpg, d = NSA_HEADS, NSA_GROUPS, NSA_HPG, NSA_HEAD_DIM
    gd = g * d
    cuts = np.cumsum([h * d] + [gd] * 6).tolist()
    wq, wkc, wvc, wks, wvs, wkw, wvw, wgt = jnp.split(w_in, cuts, axis=1)
    w_rope = jnp.concatenate([wq, wks, wkw], axis=1).astype(BF16)
    w_plain = jnp.concatenate([wkc, wvc, wvs, wvw], axis=1).astype(BF16)
    w_gates = jnp.pad(wgt.reshape(D_MODEL, g, hpg * 3), ((0, 0), (0, 0), (0, LANES - hpg * 3)))
    w_gates = w_gates.reshape(D_MODEL, g * LANES).astype(BF16)
    rc, rs1, rs2 = _rope_tables(jnp.arange(seq, dtype=jnp.int32))
    q, kst, kw, kc, vc, vs1, vw1, gates = _nsa_proj(x2, w_rope, w_plain, w_gates, rc, rs1, rs2, seq)

    nch = seq // CMP_STRIDE
    chunks = lambda t: t.reshape(bsz, g, nch, CMP_STRIDE * d)
    pe = cmp_pe.reshape(1, CMP_BLOCK * d)
    half = CMP_STRIDE * d
    pad2 = lambda w: jnp.pad(w, ((0, 0), (0, LANES - d))).astype(BF16)
    cc, cs1, cs2 = _rope_tables(jnp.arange(nch, dtype=jnp.int32) * CMP_STRIDE + (CMP_BLOCK - 1))
    k_cmp, v_cmp = _compress(chunks(kc), chunks(vc), pe[:, :half], pe[:, half:],
                             w_ck1.astype(BF16), pad2(w_ck2), w_cv1.astype(BF16), pad2(w_cv2), cc, cs1, cs2)

    ns = seq // SLC_BLOCK
    sh = lambda t: t.reshape(bsz, seq, t.shape[-1])
    o_cmp, selb = _cmp_select(q, k_cmp, v_cmp, _importance_weights(ns, nch))

    nt = seq // ATT_TK
    o = _nsa_attn(q, kst, vs1.reshape(bsz, g, nt, ATT_TK, 2 * d), kw, vw1,
                  selb, _block_expansion(nt, ns), o_cmp, sh(gates))
    return _outproj_heads_ln(o, x2, w_out.astype(BF16), ln_g.reshape(1, D_MODEL), ln_b.reshape(1, D_MODEL), router)


def _moe_layer(xa, grp, w_gate, w_up, w_down, ln_g, ln_b):
    wg, wu, wd = _cast_experts(w_gate, w_up, w_down)
    wd = wd.reshape(N_EXPERT_GROUPS, EXPERTS_PER_GROUP * D_FF_EXPERT, D_MODEL)
    dest, g_lo, g_hi = _group_sort_plan(grp)
    return _moe_sorted(xa, dest, g_lo, g_hi, wg, wu, wd, ln_g.reshape(1, D_MODEL), ln_b.reshape(1, D_MODEL))


def kernel(x, gla_w_in, gla_w_gate_up, gla_b_gate, gla_norm_g, gla_w_out, nsa_w_in, nsa_w_cmp_k1, nsa_w_cmp_k2,
           nsa_w_cmp_v1, nsa_w_cmp_v2, nsa_cmp_pe, nsa_w_out, moe_w_router, moe_b_router, moe_w_gate, moe_w_up,
           moe_w_down, ln_g, ln_b):
    bsz, seq, _ = x.shape
    x2 = x.reshape(bsz * seq, D_MODEL)
    w_router_t = moe_w_router.T
    wr_hi = w_router_t.astype(BF16)
    wr_lo = (w_router_t - wr_hi.astype(F32)).astype(BF16)
    router = (wr_hi, wr_lo, moe_b_router.reshape(N_EXPERTS, 1))
    for i in range(DEPTH):
        j = i // 2
        if i % 2 == 0:
            xa, grp = _gla_layer(x2, bsz, seq, gla_w_in[j], gla_w_gate_up[j], gla_b_gate[j], gla_norm_g[j],
                                 gla_w_out[j], ln_g[i, 0], ln_b[i, 0], router)
        else:
            xa, grp = _nsa_layer(x2, bsz, seq, nsa_w_in[j], nsa_w_cmp_k1[j], nsa_w_cmp_k2[j], nsa_w_cmp_v1[j],
                                 nsa_w_cmp_v2[j], nsa_cmp_pe[j], nsa_w_out[j], ln_g[i, 0], ln_b[i, 0], router)
        x2 = _moe_layer(xa, grp, moe_w_gate[i], moe_w_up[i], moe_w_down[i], ln_g[i, 1], ln_b[i, 1])
    return x2.reshape(bsz, seq, D_MODEL)
```

```python
import functools

import numpy as np
import jax
import jax.numpy as jnp
from jax import lax
from jax.experimental import pallas as pl
from jax.experimental.pallas import tpu as pltpu

F32 = jnp.float32
BF16 = jnp.bfloat16
HIGHEST = lax.Precision.HIGHEST

D_MODEL = 1024
DEPTH = 2

GLA_HEADS = 4
GLA_DK = D_MODEL // 2 // GLA_HEADS
GLA_DV = D_MODEL // GLA_HEADS
GLA_GATE_RANK = 16
GLA_TAU = 16.0
GLA_CHUNK = 64

NSA_HEADS = 16
NSA_GROUPS = 4
NSA_HPG = NSA_HEADS // NSA_GROUPS
NSA_HEAD_DIM = D_MODEL // NSA_HEADS
CMP_BLOCK = 32
CMP_STRIDE = 16
CMP_HIDDEN = 256
SLC_BLOCK = 64
SLC_SHIFT = SLC_BLOCK.bit_length() - 1
SLC_TOP_N = 16
WINDOW = 512

ROPE_THETA = 500000.0
ROT_DIM = NSA_HEAD_DIM // 4
ROT_HALF = ROT_DIM // 2

N_EXPERTS = 16
N_EXPERT_GROUPS = 4
EXPERTS_PER_GROUP = N_EXPERTS // N_EXPERT_GROUPS
D_FF_EXPERT = D_MODEL // 4

DN_ALPHA = (2 * DEPTH) ** 0.25
LN_EPS = 1e-5
NEG = -1e30
BIG = 1e30
LOG2E = 1.4426950408889634

LANES = 128
VMEM_LIMIT = 48 * 1024 * 1024

PROJ_TM = 256
OUT_TM = 512
GLA_T = 512
GLA_HEADS_PER_STEP = 4
MOE_TS = 512
MOE_VMEM_LIMIT = 56 * 1024 * 1024
XA_WIDTH = D_MODEL + LANES
CMP_TQ = 512
CMP_CW = 128
CMP_GROUPS = 2
ATT_TQ = 128
ATT_TK = 512
ATT_GROUPS = 2
WIN_TK = 128


def _cparams(sem):
    return pltpu.CompilerParams(dimension_semantics=sem, vmem_limit_bytes=VMEM_LIMIT)


def _bdot(a, b):
    return jnp.dot(a, b, preferred_element_type=F32)


def _dot_nt(a, b, precision=None):
    return lax.dot_general(a, b, (((1,), (1,)), ((), ())), preferred_element_type=F32, precision=precision)


def _dot_tn(a, b):
    return lax.dot_general(a, b, (((0,), (0,)), ((), ())), preferred_element_type=F32)


def _layer_norm(z, g, b):
    mu = jnp.mean(z, axis=-1, keepdims=True)
    zc = z - mu
    var = jnp.mean(zc * zc, axis=-1, keepdims=True)
    return zc * lax.rsqrt(var + LN_EPS) * g + b


def _silu(t):
    return t * (0.5 * jnp.tanh(0.5 * t) + 0.5)


def _rope128(t, c, s1, s2):
    up = pltpu.roll(t, LANES - ROT_HALF, axis=1)
    dn = pltpu.roll(t, ROT_HALF, axis=1)
    return t * c + up * s1 + dn * s2


def _gla_proj_kernel(x_ref, wm_ref, wg_ref, wgu_ref, bg_ref, q_ref, k_ref, v_ref, r_ref, la_ref):
    xb = x_ref[...].astype(BF16)
    hk = GLA_HEADS * GLA_DK
    hv = GLA_HEADS * GLA_DV
    q_ref[...] = _bdot(xb, wm_ref[:, 0:hk])
    k_ref[...] = _bdot(xb, wm_ref[:, hk:2 * hk])
    v_ref[...] = _bdot(xb, wm_ref[:, 2 * hk:2 * hk + hv])
    r_ref[...] = _bdot(xb, wm_ref[:, 2 * hk + hv:2 * hk + 2 * hv])
    g_low = _bdot(xb, wg_ref[...])
    z = _bdot(g_low.astype(BF16), wgu_ref[...]) + bg_ref[...]
    log_sig = jnp.minimum(z, 0.0) - jnp.log1p(jnp.exp(-jnp.abs(z)))
    la_ref[...] = log_sig * (1.0 / GLA_TAU)


def _gla_proj(x2, w_main, w_glow, w_gu, b_gate):
    n = x2.shape[0]
    hk = GLA_HEADS * GLA_DK
    hv = GLA_HEADS * GLA_DV
    tm = PROJ_TM
    row = lambda i: (i, 0)
    full = lambda i: (0, 0)
    return pl.pallas_call(
        _gla_proj_kernel,
        grid=(n // tm,),
        in_specs=[pl.BlockSpec((tm, D_MODEL), row),
                  pl.BlockSpec(w_main.shape, full),
                  pl.BlockSpec(w_glow.shape, full),
                  pl.BlockSpec(w_gu.shape, full),
                  pl.BlockSpec(b_gate.shape, full)],
        out_specs=[pl.BlockSpec((tm, hk), row), pl.BlockSpec((tm, hk), row),
                   pl.BlockSpec((tm, hv), row), pl.BlockSpec((tm, hv), row),
                   pl.BlockSpec((tm, hk), row)],
        out_shape=[jax.ShapeDtypeStruct((n, hk), F32), jax.ShapeDtypeStruct((n, hk), F32),
                   jax.ShapeDtypeStruct((n, hv), F32), jax.ShapeDtypeStruct((n, hv), F32),
                   jax.ShapeDtypeStruct((n, hk), F32)],
        compiler_params=_cparams(("arbitrary",)),
        name="gla_proj",
    )(x2, w_main, w_glow, w_gu, b_gate)


def _dot_01(m, x):
    x1 = x.astype(BF16)
    r1 = x - x1.astype(F32)
    x2 = r1.astype(BF16)
    x3 = (r1 - x2.astype(F32)).astype(BF16)
    return _bdot(m, x1) + _bdot(m, x2) + _bdot(m, x3)


def _gla_core_kernel(q_ref, k_ref, v_ref, la_ref, r_ref, g_ref, o_ref, st_ref):
    @pl.when(pl.program_id(2) == 0)
    def _():
        st_ref[...] = jnp.zeros_like(st_ref)

    c, t = GLA_CHUNK, GLA_T
    shift = c.bit_length() - 1
    row = lax.broadcasted_iota(jnp.int32, (t, t), 0)
    col = lax.broadcasted_iota(jnp.int32, (t, t), 1)
    causal = ((row >> shift) == (col >> shift)) & (row >= col)
    in_chunk = lax.broadcasted_iota(jnp.int32, (t, GLA_DK), 0) & (c - 1)
    chunks = [slice(ci * c, (ci + 1) * c) for ci in range(t // c)]
    for hh in range(st_ref.shape[0]):
        kcols = slice(hh * GLA_DK, (hh + 1) * GLA_DK)
        vcols = slice(hh * GLA_DV, (hh + 1) * GLA_DV)
        b = la_ref[0, :, kcols]
        step = 1
        while step < c:
            b = b + jnp.where(in_chunk >= step, pltpu.roll(b, step, axis=0), 0.0)
            step *= 2
        b_last = jnp.broadcast_to(b.reshape(t // c, c, GLA_DK)[:, c - 1:c, :],
                                  (t // c, c, GLA_DK)).reshape(t, GLA_DK)
        q = q_ref[0, :, kcols] * (GLA_DK ** -0.5)
        k = k_ref[0, :, kcols]
        vb = v_ref[0, :, vcols].astype(BF16)
        q_dec = (q * jnp.exp(b)).astype(BF16)
        k_neg = (k * jnp.exp(-b)).astype(BF16)
        k_dec = (k * jnp.exp(b_last - b)).astype(BF16)
        decay = jnp.exp(b_last)
        s = jnp.where(causal, _dot_nt(q_dec, k_neg), 0.0)
        o = _bdot(s.astype(BF16), vb)

        kv = [_dot_tn(vb[rows], k_dec[rows]) for rows in chunks]
        st = st_ref[hh]
        entering = []
        for ci, rows in enumerate(chunks):
            entering.append(st.astype(BF16))
            st = decay[ci * c:ci * c + 1] * st + kv[ci]
        st_ref[hh] = st
        o = o + jnp.concatenate([_dot_nt(q_dec[rows], s_in) for rows, s_in in zip(chunks, entering)], axis=0)
        ms = jnp.mean(o * o, axis=-1, keepdims=True)
        o = o * lax.rsqrt(ms + LN_EPS) * g_ref[...]
        o_ref[0, :, vcols] = o * _silu(r_ref[0, :, vcols])


def _gla_core(q, k, v, la, r, norm_g):
    bsz, s, _ = q.shape
    t = GLA_T
    hs = GLA_HEADS_PER_STEP
    kmap = lambda b, h, n: (b, n, h)
    kspec = pl.BlockSpec((1, t, hs * GLA_DK), kmap)
    vspec = pl.BlockSpec((1, t, hs * GLA_DV), kmap)
    return pl.pallas_call(
        _gla_core_kernel,
        grid=(bsz, GLA_HEADS // hs, s // t),
        in_specs=[kspec, kspec, vspec, kspec, vspec,
                  pl.BlockSpec((1, GLA_DV), lambda b, h, n: (0, 0))],
        out_specs=vspec,
        out_shape=jax.ShapeDtypeStruct((bsz, s, GLA_HEADS * GLA_DV), F32),
        scratch_shapes=[pltpu.VMEM((hs, GLA_DV, GLA_DK), F32)],
        compiler_params=_cparams(("arbitrary", "arbitrary", "arbitrary")),
        name="gla_core",
    )(q, k, v, la, r, norm_g)


def _row_xor(v, k, row):
    n = v.shape[0]
    up = pltpu.roll(v, n - k, axis=0)
    dn = pltpu.roll(v, k, axis=0)
    return jnp.where((row & k) == 0, up, dn)


def _route(x, wt_hi, wt_lo, bias):
    x_hi = x.astype(BF16)
    x_lo = (x - x_hi.astype(F32)).astype(BF16)
    logits = _dot_nt(wt_hi, x_hi) + (_dot_nt(wt_lo, x_hi) + _dot_nt(wt_hi, x_lo))
    s = jax.nn.sigmoid(logits)
    a = s + bias
    row = lax.broadcasted_iota(jnp.int32, a.shape, 0)
    a1 = _row_xor(a, 1, row)
    a2 = _row_xor(a, 2, row)
    a3 = _row_xor(a1, 2, row)
    p, q = jnp.maximum(a, a1), jnp.minimum(a, a1)
    r, t = jnp.maximum(a2, a3), jnp.minimum(a2, a3)
    gs = jnp.maximum(p, r) + jnp.maximum(jnp.minimum(p, r), jnp.maximum(q, t))
    g1 = _row_xor(gs, 4, row)
    g2 = _row_xor(gs, 8, row)
    g3 = _row_xor(g1, 8, row)

    def beats(other, other_first):
        return (other > gs) | ((other == gs) & other_first)

    lose = (beats(g1, (row & 4) != 0) | beats(g2, (row & 8) != 0) | beats(g3, (row & 8) != 0))

    def ahead(other, other_first):
        return jnp.where((other > a) | ((other == a) & other_first), 1.0, 0.0)

    rank = ahead(a1, (row & 1) != 0) + ahead(a2, (row & 2) != 0) + ahead(a3, (row & 2) != 0)
    sel = jnp.logical_not(lose) & (rank < 2.0)
    ssel = jnp.where(sel, s, 0.0)
    s1 = _row_xor(ssel, 1, row)
    tot = (ssel + s1) + (_row_xor(ssel, 2, row) + _row_xor(s1, 2, row))
    gate = jnp.where(sel, s / tot, 0.0)
    group = jnp.max(jnp.where(sel, (row >> 2).astype(F32), 0.0), axis=0, keepdims=True)
    tokens = x.shape[0]
    record = jnp.concatenate([gate, jnp.zeros((LANES - N_EXPERTS, tokens), F32)], axis=0)
    return record.T, group.astype(jnp.int32)


def _norm_route_store(z, g_ref, b_ref, wrh_ref, wrl_ref, br_ref, o_ref, grp_ref):
    x = _layer_norm(z, g_ref[...], b_ref[...])
    o_ref[:, :D_MODEL] = x
    o_ref[:, D_MODEL:], grp_ref[...] = _route(x, wrh_ref[...], wrl_ref[...], br_ref[...])


def _outproj_ln_kernel(h_ref, x_ref, w_ref, g_ref, b_ref, wrh_ref, wrl_ref, br_ref, o_ref, grp_ref):
    y = _bdot(h_ref[...].astype(BF16), w_ref[...])
    _norm_route_store(DN_ALPHA * x_ref[...] + y, g_ref, b_ref, wrh_ref, wrl_ref, br_ref, o_ref, grp_ref)


def _outproj_heads_ln_kernel(h_ref, x_ref, w_ref, g_ref, b_ref, wrh_ref, wrl_ref, br_ref, o_ref, grp_ref):
    h = jnp.concatenate([h_ref[0, i].astype(BF16) for i in range(h_ref.shape[1])], axis=1)
    _norm_route_store(DN_ALPHA * x_ref[...] + _bdot(h, w_ref[...]), g_ref, b_ref, wrh_ref, wrl_ref, br_ref,
                      o_ref, grp_ref)


def _outproj_call(kernel, name, h, h_spec, x2, w, g, b, router):
    n = x2.shape[0]
    tm = OUT_TM
    row = lambda i: (i, 0)
    full = lambda i: (0, 0)
    return pl.pallas_call(
        kernel,
        grid=(n // tm,),
        in_specs=[h_spec, pl.BlockSpec((tm, D_MODEL), row),
                  pl.BlockSpec(w.shape, full), pl.BlockSpec((1, D_MODEL), full), pl.BlockSpec((1, D_MODEL), full)]
                 + [pl.BlockSpec(r.shape, full) for r in router],
        out_specs=[pl.BlockSpec((tm, XA_WIDTH), row), pl.BlockSpec((1, tm), lambda i: (0, i))],
        out_shape=[jax.ShapeDtypeStruct((n, XA_WIDTH), F32), jax.ShapeDtypeStruct((1, n), jnp.int32)],
        compiler_params=_cparams(("arbitrary",)),
        name=name,
    )(h, x2, w, g, b, *router)


def _outproj_heads_ln(h4, x2, w, g, b, router):
    _, heads, seq, d = h4.shape
    per = seq // OUT_TM
    spec = pl.BlockSpec((1, heads, OUT_TM, d), lambda i: (i // per, 0, i % per, 0))
    return _outproj_call(_outproj_heads_ln_kernel, "outproj_heads_ln", h4, spec, x2, w, g, b, router)


def _outproj_ln(h2, x2, w, g, b, router):
    spec = pl.BlockSpec((OUT_TM, h2.shape[1]), lambda i: (i, 0))
    return _outproj_call(_outproj_ln_kernel, "outproj_ln", h2, spec, x2, w, g, b, router)


def _moe_sorted_kernel(dest_ref, glo_ref, ghi_ref, xa_hbm, wg_ref, wu_ref, wd_ref, lg_ref, lb_ref, out_hbm,
                       src_ref, xbuf, obuf, in_sem, out_sem):
    i = pl.program_id(0)
    last = pl.num_programs(0) - 1
    nt = 2 * pl.num_programs(0)
    ts = MOE_TS
    epg = EXPERTS_PER_GROUP

    def row_in(tile, sl, r):
        return pltpu.make_async_copy(xa_hbm.at[src_ref[tile * ts + r]], xbuf.at[sl, r], in_sem.at[sl])

    def row_out(tile, sl, r):
        return pltpu.make_async_copy(obuf.at[sl, r], out_hbm.at[src_ref[tile * ts + r]], out_sem.at[sl])

    def wait_in(sl):
        pltpu.make_async_copy(xa_hbm.at[pl.ds(0, ts)], xbuf.at[sl], in_sem.at[sl]).wait()

    def wait_out(sl):
        pltpu.make_async_copy(obuf.at[sl], out_hbm.at[pl.ds(0, ts)], out_sem.at[sl]).wait()

    @pl.when(i == 0)
    def _():
        def invert(n, carry):
            src_ref[dest_ref[n]] = n
            return carry

        lax.fori_loop(0, dest_ref.shape[0], invert, 0, unroll=8)
        for r in range(ts):
            row_in(0, 0, r).start()

    @pl.when(i >= 1)
    def _():
        wait_out(0)
        wait_out(1)

    for sl in range(2):
        tile = 2 * i + sl
        wait_in(sl)
        nxt = jnp.minimum(tile + 1, nt - 1)
        for r in range(ts):
            row_in(nxt, 1 - sl, r).start()

        xa = xbuf[sl]
        x = xa[:, :D_MODEL]
        gate = xa[:, D_MODEL:]
        xb = x.astype(BF16)

        def group_ffn(g, xb=xb, gate=gate):
            hs = []
            for e in range(epg):
                col = gate[:, e:e + 1]
                for other in range(1, N_EXPERT_GROUPS):
                    col = jnp.where(g == other, gate[:, other * epg + e:other * epg + e + 1], col)
                expert = g * epg + e
                hs.append((_silu(_bdot(xb, wg_ref[expert])) * _bdot(xb, wu_ref[expert]) * col).astype(BF16))
            return _bdot(jnp.concatenate(hs, axis=1), wd_ref[g])

        y = group_ffn(glo_ref[tile])
        y = lax.fori_loop(glo_ref[tile] + 1, ghi_ref[tile] + 1, lambda g, acc: acc + group_ffn(g), y)
        obuf[sl] = _layer_norm(DN_ALPHA * x + y, lg_ref[...], lb_ref[...])
        for r in range(ts):
            row_out(tile, sl, r).start()

    @pl.when(i == last)
    def _():
        wait_out(0)
        wait_out(1)
        wait_in(0)


def _moe_sorted(xa, dest, g_lo, g_hi, wg, wu, wd, lg, lb):
    n = xa.shape[0]
    ts = MOE_TS
    resident = pl.Buffered(1)
    whole = lambda i, *_: (0, 0, 0)
    grid_spec = pltpu.PrefetchScalarGridSpec(
        num_scalar_prefetch=3,
        grid=(n // (2 * ts),),
        in_specs=[pl.BlockSpec(memory_space=pl.ANY),
                  pl.BlockSpec(wg.shape, whole, pipeline_mode=resident),
                  pl.BlockSpec(wu.shape, whole, pipeline_mode=resident),
                  pl.BlockSpec(wd.shape, whole, pipeline_mode=resident),
                  pl.BlockSpec((1, D_MODEL), lambda i, *_: (0, 0)),
                  pl.BlockSpec((1, D_MODEL), lambda i, *_: (0, 0))],
        out_specs=pl.BlockSpec(memory_space=pl.ANY),
        scratch_shapes=[pltpu.SMEM((n,), jnp.int32),
                        pltpu.VMEM((2, ts, XA_WIDTH), F32), pltpu.VMEM((2, ts, D_MODEL), F32),
                        pltpu.SemaphoreType.DMA((2,)), pltpu.SemaphoreType.DMA((2,))])
    return pl.pallas_call(
        _moe_sorted_kernel,
        grid_spec=grid_spec,
        out_shape=jax.ShapeDtypeStruct((n, D_MODEL), F32),
        compiler_params=pltpu.CompilerParams(dimension_semantics=("arbitrary",), vmem_limit_bytes=MOE_VMEM_LIMIT),
        name="moe_sorted",
    )(dest, g_lo, g_hi, xa, wg, wu, wd, lg, lb)


def _cast_experts_kernel(wg_ref, wu_ref, wd_ref, og_ref, ou_ref, od_ref):
    og_ref[...] = wg_ref[0].astype(BF16)
    ou_ref[...] = wu_ref[0].astype(BF16)
    od_ref[...] = wd_ref[0].astype(BF16)


def _cast_experts(layer, w_gate, w_up, w_down):
    ws = (w_gate, w_up, w_down)
    return pl.pallas_call(
        _cast_experts_kernel,
        grid=(w_gate.shape[1],),
        in_specs=[pl.BlockSpec((1, 1) + w.shape[2:], lambda e: (layer, e, 0, 0)) for w in ws],
        out_specs=[pl.BlockSpec((1,) + w.shape[2:], lambda e: (e, 0, 0)) for w in ws],
        out_shape=[jax.ShapeDtypeStruct(w.shape[1:], BF16) for w in ws],
        compiler_params=_cparams(("arbitrary",)),
        name="cast_experts",
    )(*ws)


def _group_sort_plan(grp):
    n = grp.shape[1]
    onehot = (grp == jnp.arange(N_EXPERT_GROUPS, dtype=jnp.int32)[:, None]).astype(jnp.int32)
    csum = jnp.cumsum(onehot, axis=1)
    ends = jnp.cumsum(csum[:, -1])
    dest = jnp.sum(onehot * (csum - 1 + (ends - csum[:, -1])[:, None]), axis=0)
    first_row = jnp.arange(n // MOE_TS, dtype=jnp.int32) * MOE_TS
    group_of_row = lambda rows: jnp.sum(ends[None, :] <= rows[:, None], axis=1).astype(jnp.int32)
    return dest, group_of_row(first_row), group_of_row(first_row + (MOE_TS - 1))


def _nsa_proj_kernel(x_ref, wr_ref, wp_ref, wgt_ref, c_ref, s1_ref, s2_ref,
                     q_ref, kst_ref, kw_ref, kc_ref, vc_ref, vs_ref, vw_ref, gt_ref):
    xb = x_ref[...].astype(BF16)
    c, s1, s2 = c_ref[...], s1_ref[...], s2_ref[...]
    d = NSA_HEAD_DIM
    nq = NSA_HEADS * d // LANES
    ng = NSA_GROUPS * d // LANES
    first_head = lax.broadcasted_iota(jnp.int32, (x_ref.shape[0], LANES), 1) < d

    def put_rows(ref, j, t):
        ref[0, 2 * j] = t[:, :d].astype(ref.dtype)
        ref[0, 2 * j + 1] = t[:, d:].astype(ref.dtype)

    def put_with_ones(ref, j, t):
        ref[0, 2 * j] = jnp.where(first_head, t, 1.0).astype(ref.dtype)
        ref[0, 2 * j + 1] = jnp.where(first_head, pltpu.roll(t, d, axis=1), 1.0).astype(ref.dtype)

    for j in range(nq + 2 * ng):
        t = _rope128(_bdot(xb, wr_ref[:, j * LANES:(j + 1) * LANES]), c, s1, s2)
        if j < nq:
            put_rows(q_ref, j, t * (d ** -0.5 * LOG2E))
        elif j < nq + ng:
            tt = t.T.astype(BF16)
            kst_ref[0, 2 * (j - nq), 0] = tt[:d]
            kst_ref[0, 2 * (j - nq) + 1, 0] = tt[d:]
        else:
            put_rows(kw_ref, j - nq - ng, t)
    for idx, (ref, put) in enumerate(((kc_ref, put_rows), (vc_ref, put_rows),
                                      (vs_ref, put_with_ones), (vw_ref, put_with_ones))):
        for j in range(ng):
            col = (idx * ng + j) * LANES
            put(ref, j, _bdot(xb, wp_ref[:, col:col + LANES]))
    gt_ref[...] = jax.nn.sigmoid(_bdot(xb, wgt_ref[...]))


def _nsa_proj(x2, w_rope, w_plain, w_gates, rc, rs1, rs2, seq):
    n = x2.shape[0]
    tm = PROJ_TM
    gd = NSA_GROUPS * NSA_HEAD_DIM
    row = lambda i: (i, 0)
    full = lambda i: (0, 0)
    per = seq // tm
    pos = lambda i: (i % per, 0)
    bsz, g, d = n // seq, NSA_GROUPS, NSA_HEAD_DIM
    sub = ATT_TK // tm
    hmap = lambda i: (i // per, 0, i % per, 0)

    def rows_out(heads, width, dtype):
        return pl.BlockSpec((1, heads, tm, width), hmap), jax.ShapeDtypeStruct((bsz, heads, seq, width), dtype)

    outs = [rows_out(NSA_HEADS, d, BF16),
            (pl.BlockSpec((1, g, 1, d, tm), lambda i: (i // per, 0, (i % per) // sub, 0, (i % per) % sub)),
             jax.ShapeDtypeStruct((bsz, g, seq // ATT_TK, d, ATT_TK), BF16)),
            rows_out(g, d, BF16),
            rows_out(g, d, F32), rows_out(g, d, F32),
            rows_out(g, 2 * d, BF16), rows_out(g, 2 * d, BF16),
            (pl.BlockSpec((tm, g * LANES), row), jax.ShapeDtypeStruct((n, g * LANES), F32))]
    return pl.pallas_call(
        _nsa_proj_kernel,
        grid=(n // tm,),
        in_specs=[pl.BlockSpec((tm, D_MODEL), row),
                  pl.BlockSpec(w_rope.shape, full), pl.BlockSpec(w_plain.shape, full),
                  pl.BlockSpec(w_gates.shape, full),
                  pl.BlockSpec((tm, LANES), pos), pl.BlockSpec((tm, LANES), pos), pl.BlockSpec((tm, LANES), pos)],
        out_specs=[o[0] for o in outs],
        out_shape=[o[1] for o in outs],
        compiler_params=_cparams(("arbitrary",)),
        name="nsa_proj",
    )(x2, w_rope, w_plain, w_gates, rc, rs1, rs2)


def _compress_kernel(ck_ref, cv_ref, pet_ref, peb_ref, wk1_ref, wk2_ref, wv1_ref, wv2_ref,
                     c_ref, s1_ref, s2_ref, ko_ref, vo_ref):
    nch = ck_ref.shape[2] // CMP_STRIDE
    half = CMP_STRIDE * NSA_HEAD_DIM
    pet, peb = pet_ref[...], peb_ref[...]

    def chunk_rows(ref):
        return jnp.concatenate([ref[0, 0, pl.ds(t, nch, stride=CMP_STRIDE), :] for t in range(CMP_STRIDE)], axis=1)

    def mlp(ch, w1_ref, w2_ref):
        a = _bdot((ch + pet).astype(BF16), w1_ref[0:half, :])
        bm = _bdot((ch + peb).astype(BF16), w1_ref[half:2 * half, :])
        h = _silu(a + pltpu.roll(bm, nch - 1, axis=0))
        return _bdot(h.astype(BF16), w2_ref[...])

    kc = _rope128(mlp(chunk_rows(ck_ref), wk1_ref, wk2_ref), c_ref[...], s1_ref[...], s2_ref[...])
    ko_ref[0, 0] = kc[:, :NSA_HEAD_DIM].astype(BF16)
    vo_ref[0, 0] = mlp(chunk_rows(cv_ref), wv1_ref, wv2_ref)[:, :NSA_HEAD_DIM].astype(BF16)


def _compress(ck, cv, pet, peb, wk1, wk2, wv1, wv2, cc, cs1, cs2):
    bsz, g, seq, width = ck.shape
    nch = seq // CMP_STRIDE
    blk = lambda b, gi: (b, gi, 0, 0)
    full = lambda b, gi: (0, 0)
    return pl.pallas_call(
        _compress_kernel,
        grid=(bsz, g),
        in_specs=[pl.BlockSpec((1, 1, seq, width), blk), pl.BlockSpec((1, 1, seq, width), blk),
                  pl.BlockSpec(pet.shape, full), pl.BlockSpec(peb.shape, full),
                  pl.BlockSpec(wk1.shape, full), pl.BlockSpec(wk2.shape, full),
                  pl.BlockSpec(wv1.shape, full), pl.BlockSpec(wv2.shape, full),
                  pl.BlockSpec(cc.shape, full), pl.BlockSpec(cs1.shape, full), pl.BlockSpec(cs2.shape, full)],
        out_specs=[pl.BlockSpec((1, 1, nch, NSA_HEAD_DIM), blk), pl.BlockSpec((1, 1, nch, NSA_HEAD_DIM), blk)],
        out_shape=[jax.ShapeDtypeStruct((bsz, g, nch, NSA_HEAD_DIM), BF16)] * 2,
        compiler_params=_cparams(("arbitrary", "arbitrary")),
        name="nsa_compress",
    )(ck, cv, pet, peb, wk1, wk2, wv1, wv2, cc, cs1, cs2)


def _cmp_select_kernel(q_ref, kc_ref, vc_ref, wimp_ref, o_ref, sel_ref):
    tq = q_ref.shape[2]
    nch = kc_ref.shape[2]
    ns = wimp_ref.shape[0]
    s0 = pl.program_id(2) * tq
    ratio = SLC_BLOCK // CMP_STRIDE
    n_vis = (s0 + tq - CMP_BLOCK) // CMP_STRIDE + 1
    n_chunks = (n_vis + CMP_CW - 1) // CMP_CW

    def body(w):
        nb = w // ratio
        qpos_c = s0 + lax.broadcasted_iota(jnp.int32, (tq, 1), 0)
        cend = lax.broadcasted_iota(jnp.int32, (1, w), 1) * CMP_STRIDE + (CMP_BLOCK - 1)
        cmask = cend <= qpos_c
        anyvis = (qpos_c >= CMP_BLOCK - 1).astype(F32)
        blk = lax.broadcasted_iota(jnp.int32, (nb, tq), 0)
        cur = (s0 + lax.broadcasted_iota(jnp.int32, (nb, tq), 1)) >> SLC_SHIFT
        forced = (blk == 0) | (blk == cur) | (blk == cur - 1)
        visible = blk <= cur
        for gi in range(kc_ref.shape[1]):
            group_body(gi, w, nb, cmask, anyvis, forced, visible)

    def group_body(gi, w, nb, cmask, anyvis, forced, visible):
        d = q_ref.shape[3]
        hpg = q_ref.shape[1] // kc_ref.shape[1]
        heads = slice(gi * hpg, (gi + 1) * hpg)
        s = _dot_nt(q_ref[0, heads].reshape(hpg * tq, d), kc_ref[0, gi, :w, :]).reshape(hpg, tq, w)
        s = jnp.where(cmask[None], s, NEG)
        e = jnp.exp2(s - jnp.max(s, axis=-1, keepdims=True))
        p = e * (anyvis[None] / jnp.sum(e, axis=-1, keepdims=True))
        o_ref[0, heads] = _bdot(p.reshape(hpg * tq, w).astype(BF16), vc_ref[0, gi, :w, :]).reshape(hpg, tq, d)
        pg = jnp.sum(p, axis=0)

        pg_hi = pg.astype(BF16)
        pg_lo = (pg - pg_hi.astype(F32)).astype(BF16)
        wimp = wimp_ref[:nb, :w]
        imp = _dot_nt(wimp, pg_hi) + _dot_nt(wimp, pg_lo)
        imp = jnp.where(visible, jnp.where(forced, BIG, imp), NEG)
        parts = [slice(i, i + LANES) for i in range(0, tq, LANES)]
        blk_f = lax.broadcasted_iota(jnp.int32, (nb, LANES), 0).astype(F32)
        imps = [imp[:, pt] for pt in parts]
        sels = [jnp.zeros((nb, LANES), F32) for _ in parts]
        for _ in range(min(SLC_TOP_N, nb)):
            for i in range(len(parts)):
                m = jnp.max(imps[i], axis=0, keepdims=True)
                first = jnp.min(jnp.where(imps[i] == m, blk_f, float(nb)), axis=0, keepdims=True)
                hit = blk_f == first
                sels[i] = jnp.where(hit, 1.0, sels[i])
                imps[i] = jnp.where(hit, -jnp.inf, imps[i])
        sel = jnp.concatenate(sels, axis=1)
        if nb < ns:
            sel = jnp.concatenate([sel, jnp.zeros((ns - nb, tq), F32)], axis=0)
        sel_ref[0, gi] = jnp.where(sel.T > 0.0, 0.0, NEG).astype(BF16)

    for nv in range(1, nch // CMP_CW + 1):
        pl.when(n_chunks == nv)(functools.partial(body, nv * CMP_CW))


def _cmp_select(q, k_cmp, v_cmp, wimp_t):
    bsz, _, s, d = q.shape
    g = NSA_GROUPS
    ng = CMP_GROUPS
    nch = k_cmp.shape[2]
    ns = wimp_t.shape[0]
    tq = min(CMP_TQ, s)
    hmap = lambda b, gp, qi: (b, gp, qi, 0)
    return pl.pallas_call(
        _cmp_select_kernel,
        grid=(bsz, g // ng, s // tq),
        in_specs=[pl.BlockSpec((1, ng * NSA_HPG, tq, d), hmap),
                  pl.BlockSpec((1, ng, nch, d), lambda b, gp, qi: (b, gp, 0, 0)),
                  pl.BlockSpec((1, ng, nch, d), lambda b, gp, qi: (b, gp, 0, 0)),
                  pl.BlockSpec(wimp_t.shape, lambda b, gp, qi: (0, 0))],
        out_specs=[pl.BlockSpec((1, ng * NSA_HPG, tq, d), hmap),
                   pl.BlockSpec((1, ng, tq, ns), hmap)],
        out_shape=[jax.ShapeDtypeStruct((bsz, NSA_HEADS, s, d), F32),
                   jax.ShapeDtypeStruct((bsz, g, s, ns), BF16)],
        compiler_params=_cparams(("arbitrary", "arbitrary", "arbitrary")),
        name="nsa_cmp_select",
    )(q, k_cmp, v_cmp, wimp_t)


def _nsa_attn_kernel(q_ref, kst_ref, vs_ref, kw_ref, vw_ref, selb_ref, eexp_ref, ocmp_ref, gt_ref,
                     o_ref, m_ref, acc_ref, s_ref, p_ref, a_ref):
    ng = kst_ref.shape[1]
    hpg = q_ref.shape[1] // ng
    tq, d = q_ref.shape[2:]
    rows = hpg * tq
    s0 = pl.program_id(2) * tq
    qpos = s0 + lax.broadcasted_iota(jnp.int32, (tq, 1), 0)
    q4 = [q_ref[0, gi * hpg:(gi + 1) * hpg].reshape(rows, d) for gi in range(ng)]
    q4a = [jnp.concatenate([jnp.concatenate([selb_ref[0, gi]] * hpg, axis=0), q4[gi]], axis=1)
           for gi in range(ng)]

    def scores(gi, t):
        return _bdot(q4a[gi], jnp.concatenate([eexp_ref[t], kst_ref[0, gi, t]], axis=0))

    def flush(gi, t):
        acc_ref[gi] = a_ref[gi] * acc_ref[gi] + _bdot(p_ref[gi], vs_ref[0, gi, t])

    def softmax_step(gi, s):
        m_prev = m_ref[gi]
        m_new = jnp.maximum(m_prev, jnp.max(s, axis=-1, keepdims=True))
        a_ref[gi] = jnp.exp2(m_prev - m_new)
        p_ref[gi] = jnp.exp2(s - m_new).astype(BF16)
        m_ref[gi] = m_new

    m_ref[...] = jnp.full_like(m_ref, NEG)
    acc_ref[...] = jnp.zeros_like(acc_ref)
    p_ref[...] = jnp.zeros_like(p_ref)
    a_ref[...] = jnp.ones_like(a_ref)
    t_diag = s0 // ATT_TK
    for gi in range(ng):
        s_ref[gi] = scores(gi, 0)

    def step(t):
        for gi in range(ng):
            s = s_ref[gi]
            s_next = scores(gi, t + 1)
            flush(gi, jnp.maximum(t - 1, 0))
            softmax_step(gi, s)
            s_ref[gi] = s_next

    def pair_body(i, carry):
        step(2 * i)
        step(2 * i + 1)
        return carry

    lax.fori_loop(0, t_diag // 2, pair_body, 0)
    pl.when(t_diag % 2 == 1)(lambda: step(t_diag - 1))

    span = WINDOW + tq
    w0 = pl.multiple_of(jnp.maximum(s0 - WINDOW, 0), tq)
    rel = qpos - (w0 + lax.broadcasted_iota(jnp.int32, (1, span), 1))
    wbias = jnp.where((rel >= 0) & (rel < WINDOW), 0.0, NEG)
    kpos = t_diag * ATT_TK + lax.broadcasted_iota(jnp.int32, (1, ATT_TK), 1)
    causal = (kpos <= qpos)[None]
    for gi in range(ng):
        sw = _dot_nt(q4[gi], kw_ref[0, gi, pl.ds(w0, span), :])
        flush(gi, jnp.maximum(t_diag - 1, 0))
        s = s_ref[gi].reshape(hpg, tq, ATT_TK)
        softmax_step(gi, jnp.where(causal, s, NEG).reshape(rows, ATT_TK))
        sw = (sw.reshape(hpg, tq, span) + wbias[None]).reshape(rows, span)
        pw = jnp.exp2(sw - jnp.max(sw, axis=-1, keepdims=True)).astype(BF16)
        flush(gi, t_diag)
        accw = _bdot(pw, vw_ref[0, gi, pl.ds(w0, span), :])
        o_win = (accw[:, :d] / accw[:, d:d + 1]).reshape(hpg, tq, d)
        acc = acc_ref[gi]
        o_slc = (acc[:, :d] / acc[:, d:d + 1]).reshape(hpg, tq, d)
        gts = gt_ref[0, :, gi * LANES:(gi + 1) * LANES]
        for h in range(hpg):
            g0, g1, g2 = (gts[:, 3 * h + j:3 * h + j + 1] for j in range(3))
            o_ref[0, gi * hpg + h] = g0 * ocmp_ref[0, gi * hpg + h] + g1 * o_slc[h] + g2 * o_win[h]


def _nsa_attn(q, kst, vst, kw, vw, selb, eexp, o_cmp, gates):
    bsz, _, s, d = q.shape
    ng = ATT_GROUPS
    tq = ATT_TQ
    heads = ng * NSA_HPG
    rows = NSA_HPG * tq
    ns = selb.shape[-1]
    hmap = lambda b, gp, qi: (b, gp, qi, 0)
    once = pl.Buffered(1)
    return pl.pallas_call(
        _nsa_attn_kernel,
        grid=(bsz, NSA_GROUPS // ng, s // tq),
        in_specs=[pl.BlockSpec((1, heads, tq, d), hmap),
                  pl.BlockSpec((1, ng) + kst.shape[2:], lambda b, gp, qi: (b, gp, 0, 0, 0), pipeline_mode=once),
                  pl.BlockSpec((1, ng) + vst.shape[2:], lambda b, gp, qi: (b, gp, 0, 0, 0), pipeline_mode=once),
                  pl.BlockSpec((1, ng) + kw.shape[2:], lambda b, gp, qi: (b, gp, 0, 0), pipeline_mode=once),
                  pl.BlockSpec((1, ng) + vw.shape[2:], lambda b, gp, qi: (b, gp, 0, 0), pipeline_mode=once),
                  pl.BlockSpec((1, ng, tq, ns), hmap),
                  pl.BlockSpec(eexp.shape, lambda b, gp, qi: (0, 0, 0), pipeline_mode=once),
                  pl.BlockSpec((1, heads, tq, d), hmap),
                  pl.BlockSpec((1, tq, ng * LANES), lambda b, gp, qi: (b, qi, gp))],
        out_specs=pl.BlockSpec((1, heads, tq, d), hmap),
        out_shape=jax.ShapeDtypeStruct((bsz, NSA_HEADS, s, d), F32),
        scratch_shapes=[pltpu.VMEM((ng, rows, 1), F32),
                        pltpu.VMEM((ng, rows, 2 * d), F32),
                        pltpu.VMEM((ng, rows, ATT_TK), F32),
                        pltpu.VMEM((ng, rows, ATT_TK), BF16),
                        pltpu.VMEM((ng, rows, 1), F32)],
        compiler_params=_cparams(("arbitrary", "arbitrary", "arbitrary")),
        name="nsa_attn",
    )(q, kst, vst, kw, vw, selb, eexp, o_cmp, gates)


def _rope_tables(pos):
    inv_freq = jnp.power(ROPE_THETA, -jnp.arange(ROT_HALF, dtype=F32) * (2.0 / ROT_DIM))
    ang = pos.astype(F32)[:, None] * inv_freq[None, :]
    cos, sin = jnp.cos(ang), jnp.sin(ang)
    n = pos.shape[0]
    one = jnp.ones((n, NSA_HEAD_DIM - ROT_DIM), F32)
    zero8 = jnp.zeros((n, ROT_HALF), F32)
    zero = jnp.zeros((n, NSA_HEAD_DIM - ROT_DIM), F32)
    c = jnp.concatenate([cos, cos, one], axis=1)
    s1 = jnp.concatenate([-sin, zero8, zero], axis=1)
    s2 = jnp.concatenate([zero8, sin, zero], axis=1)
    rep = LANES // NSA_HEAD_DIM
    return tuple(jnp.tile(t, (1, rep)) for t in (c, s1, s2))


def _importance_weights(ns, nch):
    ratio = SLC_BLOCK // CMP_STRIDE
    span = CMP_BLOCK // CMP_STRIDE
    w = np.zeros((ns, nch), np.float32)
    for j in range(ns):
        for m in range(ratio):
            for n in range(span):
                c = ratio * j + m + n
                if c < nch - 1:
                    w[j, c] += 1.0
    return jnp.asarray(w, dtype=BF16)


def _block_expansion(nt, ns):
    key_blk = (np.arange(nt)[:, None] * ATT_TK + np.arange(ATT_TK)[None, :]) // SLC_BLOCK
    e = (np.arange(ns)[None, :, None] == key_blk[:, None, :]).astype(np.float32)
    return jnp.asarray(e, dtype=BF16)


def _gla_layer(x2, bsz, seq, w_in, w_gate_up, b_gate, norm_g, w_out, ln_g, ln_b, router):
    hk = GLA_HEADS * GLA_DK
    hv = GLA_HEADS * GLA_DV
    cuts = np.cumsum([hk, hk, hv, GLA_GATE_RANK]).tolist()
    wq, wk, wv, wg, wr = jnp.split(w_in, cuts, axis=1)
    w_main = jnp.concatenate([wq, wk, wv, wr], axis=1).astype(BF16)
    w_glow = jnp.pad(wg, ((0, 0), (0, LANES - GLA_GATE_RANK))).astype(BF16)
    w_gu = jnp.pad(w_gate_up, ((0, LANES - GLA_GATE_RANK), (0, 0))).astype(BF16)
    q, k, v, r, la = _gla_proj(x2, w_main, w_glow, w_gu, b_gate.reshape(1, hk))
    sh = lambda t: t.reshape(bsz, seq, t.shape[-1])
    o = _gla_core(sh(q), sh(k), sh(v), sh(la), sh(r), norm_g.reshape(1, GLA_DV))
    return _outproj_ln(o.reshape(bsz * seq, hv), x2, w_out.astype(BF16),
                       ln_g.reshape(1, D_MODEL), ln_b.reshape(1, D_MODEL), router)


def _nsa_layer(x2, bsz, seq, w_in, w_ck1, w_ck2, w_cv1, w_cv2, cmp_pe, w_out, ln_g, ln_b, router):
    h, g, hpg, d = NSA_HEADS, NSA_GROUPS, NSA_HPG, NSA_HEAD_DIM
    gd = g * d
    cuts = np.cumsum([h * d] + [gd] * 6).tolist()
    wq, wkc, wvc, wks, wvs, wkw, wvw, wgt = jnp.split(w_in, cuts, axis=1)
    w_rope = jnp.concatenate([wq, wks, wkw], axis=1).astype(BF16)
    w_plain = jnp.concatenate([wkc, wvc, wvs, wvw], axis=1).astype(BF16)
    w_gates = jnp.pad(wgt.reshape(D_MODEL, g, hpg * 3), ((0, 0), (0, 0), (0, LANES - hpg * 3)))
    w_gates = w_gates.reshape(D_MODEL, g * LANES).astype(BF16)
    rc, rs1, rs2 = _rope_tables(jnp.arange(seq, dtype=jnp.int32))
    q, kst, kw, kc, vc, vs1, vw1, gates = _nsa_proj(x2, w_rope, w_plain, w_gates, rc, rs1, rs2, seq)

    nch = seq // CMP_STRIDE
    pe = cmp_pe.reshape(1, CMP_BLOCK * d)
    half = CMP_STRIDE * d
    pad2 = lambda w: jnp.pad(w, ((0, 0), (0, LANES - d))).astype(BF16)
    cc, cs1, cs2 = _rope_tables(jnp.arange(nch, dtype=jnp.int32) * CMP_STRIDE + (CMP_BLOCK - 1))
    k_cmp, v_cmp = _compress(kc, vc, pe[:, :half], pe[:, half:],
                             w_ck1.astype(BF16), pad2(w_ck2), w_cv1.astype(BF16), pad2(w_cv2), cc, cs1, cs2)

    ns = seq // SLC_BLOCK
    sh = lambda t: t.reshape(bsz, seq, t.shape[-1])
    o_cmp, selb = _cmp_select(q, k_cmp, v_cmp, _importance_weights(ns, nch))

    nt = seq // ATT_TK
    o = _nsa_attn(q, kst, vs1.reshape(bsz, g, nt, ATT_TK, 2 * d), kw, vw1,
                  selb, _block_expansion(nt, ns), o_cmp, sh(gates))
    return _outproj_heads_ln(o, x2, w_out.astype(BF16), ln_g.reshape(1, D_MODEL), ln_b.reshape(1, D_MODEL), router)


def _moe_layer(xa, grp, layer, w_gate, w_up, w_down, ln_g, ln_b):
    wg, wu, wd = _cast_experts(layer, w_gate, w_up, w_down)
    wd = wd.reshape(N_EXPERT_GROUPS, EXPERTS_PER_GROUP * D_FF_EXPERT, D_MODEL)
    dest, g_lo, g_hi = _group_sort_plan(grp)
    return _moe_sorted(xa, dest, g_lo, g_hi, wg, wu, wd, ln_g.reshape(1, D_MODEL), ln_b.reshape(1, D_MODEL))


def kernel(x, gla_w_in, gla_w_gate_up, gla_b_gate, gla_norm_g, gla_w_out, nsa_w_in, nsa_w_cmp_k1, nsa_w_cmp_k2,
           nsa_w_cmp_v1, nsa_w_cmp_v2, nsa_cmp_pe, nsa_w_out, moe_w_router, moe_b_router, moe_w_gate, moe_w_up,
           moe_w_down, ln_g, ln_b):
    bsz, seq, _ = x.shape
    x2 = x.reshape(bsz * seq, D_MODEL)
    w_router_t = moe_w_router.T
    wr_hi = w_router_t.astype(BF16)
    wr_lo = (w_router_t - wr_hi.astype(F32)).astype(BF16)
    router = (wr_hi, wr_lo, moe_b_router.reshape(N_EXPERTS, 1))
    for i in range(DEPTH):
        j = i // 2
        if i % 2 == 0:
            xa, grp = _gla_layer(x2, bsz, seq, gla_w_in[j], gla_w_gate_up[j], gla_b_gate[j], gla_norm_g[j],
                                 gla_w_out[j], ln_g[i, 0], ln_b[i, 0], router)
        else:
            xa, grp = _nsa_layer(x2, bsz, seq, nsa_w_in[j], nsa_w_cmp_k1[j], nsa_w_cmp_k2[j], nsa_w_cmp_v1[j],
                                 nsa_w_cmp_v2[j], nsa_cmp_pe[j], nsa_w_out[j], ln_g[i, 0], ln_b[i, 0], router)
        x2 = _moe_layer(xa, grp, i, moe_w_gate, moe_w_up, moe_w_down, ln_g[i, 1], ln_b[i, 1])
    return x2.reshape(bsz, seq, D_MODEL)
```

```python
import functools

import numpy as np
import jax
import jax.numpy as jnp
from jax import lax
from jax.experimental import pallas as pl
from jax.experimental.pallas import tpu as pltpu

F32 = jnp.float32
BF16 = jnp.bfloat16
HIGHEST = lax.Precision.HIGHEST

D_MODEL = 1024
DEPTH = 2

GLA_HEADS = 4
GLA_DK = D_MODEL // 2 // GLA_HEADS
GLA_DV = D_MODEL // GLA_HEADS
GLA_GATE_RANK = 16
GLA_TAU = 16.0
GLA_CHUNK = 64

NSA_HEADS = 16
NSA_GROUPS = 4
NSA_HPG = NSA_HEADS // NSA_GROUPS
NSA_HEAD_DIM = D_MODEL // NSA_HEADS
CMP_BLOCK = 32
CMP_STRIDE = 16
CMP_HIDDEN = 256
SLC_BLOCK = 64
SLC_SHIFT = SLC_BLOCK.bit_length() - 1
SLC_TOP_N = 16
WINDOW = 512

ROPE_THETA = 500000.0
ROT_DIM = NSA_HEAD_DIM // 4
ROT_HALF = ROT_DIM // 2

N_EXPERTS = 16
N_EXPERT_GROUPS = 4
EXPERTS_PER_GROUP = N_EXPERTS // N_EXPERT_GROUPS
D_FF_EXPERT = D_MODEL // 4

DN_ALPHA = (2 * DEPTH) ** 0.25
LN_EPS = 1e-5
NEG = -1e30
BIG = 1e30
LOG2E = 1.4426950408889634

LANES = 128
MXU_COLS = 256
VMEM_LIMIT = 48 * 1024 * 1024

PROJ_TM = 256
OUT_TM = 512
GLA_T = 512
GLA_HEADS_PER_STEP = 4
MOE_TS = 512
MOE_VMEM_LIMIT = 56 * 1024 * 1024
XA_WIDTH = D_MODEL + LANES
CMP_TQ = 512
CMP_CW = 128
CMP_GROUPS = 2
ATT_TQ = 128
ATT_TK = 512
ATT_GROUPS = 2
WIN_TK = 128


def _cparams(sem):
    return pltpu.CompilerParams(dimension_semantics=sem, vmem_limit_bytes=VMEM_LIMIT)


def _bdot(a, b):
    return jnp.dot(a, b, preferred_element_type=F32)


def _dot_nt(a, b, precision=None):
    return lax.dot_general(a, b, (((1,), (1,)), ((), ())), preferred_element_type=F32, precision=precision)


def _dot_tn(a, b):
    return lax.dot_general(a, b, (((0,), (0,)), ((), ())), preferred_element_type=F32)


def _layer_norm(z, g, b):
    mu = jnp.mean(z, axis=-1, keepdims=True)
    zc = z - mu
    var = jnp.mean(zc * zc, axis=-1, keepdims=True)
    return zc * lax.rsqrt(var + LN_EPS) * g + b


def _silu(t):
    return t * (0.5 * jnp.tanh(0.5 * t) + 0.5)


def _rope128(t, c, s1, s2):
    up = pltpu.roll(t, LANES - ROT_HALF, axis=1)
    dn = pltpu.roll(t, ROT_HALF, axis=1)
    return t * c + up * s1 + dn * s2


def _gla_proj_kernel(x_ref, wm_ref, wg_ref, wgu_ref, bg_ref, q_ref, k_ref, v_ref, r_ref, la_ref):
    xb = x_ref[...].astype(BF16)
    hk = GLA_HEADS * GLA_DK
    hv = GLA_HEADS * GLA_DV
    q_ref[...] = _bdot(xb, wm_ref[:, 0:hk])
    k_ref[...] = _bdot(xb, wm_ref[:, hk:2 * hk])
    v_ref[...] = _bdot(xb, wm_ref[:, 2 * hk:2 * hk + hv])
    r_ref[...] = _bdot(xb, wm_ref[:, 2 * hk + hv:2 * hk + 2 * hv])
    g_low = _bdot(xb, wg_ref[...])
    z = _bdot(g_low.astype(BF16), wgu_ref[...]) + bg_ref[...]
    log_sig = jnp.minimum(z, 0.0) - jnp.log1p(jnp.exp(-jnp.abs(z)))
    la_ref[...] = log_sig * (1.0 / GLA_TAU)


def _gla_proj(x2, w_main, w_glow, w_gu, b_gate):
    n = x2.shape[0]
    hk = GLA_HEADS * GLA_DK
    hv = GLA_HEADS * GLA_DV
    tm = PROJ_TM
    row = lambda i: (i, 0)
    full = lambda i: (0, 0)
    return pl.pallas_call(
        _gla_proj_kernel,
        grid=(n // tm,),
        in_specs=[pl.BlockSpec((tm, D_MODEL), row),
                  pl.BlockSpec(w_main.shape, full),
                  pl.BlockSpec(w_glow.shape, full),
                  pl.BlockSpec(w_gu.shape, full),
                  pl.BlockSpec(b_gate.shape, full)],
        out_specs=[pl.BlockSpec((tm, hk), row), pl.BlockSpec((tm, hk), row),
                   pl.BlockSpec((tm, hv), row), pl.BlockSpec((tm, hv), row),
                   pl.BlockSpec((tm, hk), row)],
        out_shape=[jax.ShapeDtypeStruct((n, hk), F32), jax.ShapeDtypeStruct((n, hk), F32),
                   jax.ShapeDtypeStruct((n, hv), F32), jax.ShapeDtypeStruct((n, hv), F32),
                   jax.ShapeDtypeStruct((n, hk), F32)],
        compiler_params=_cparams(("arbitrary",)),
        name="gla_proj",
    )(x2, w_main, w_glow, w_gu, b_gate)


def _dot_01(m, x):
    x1 = x.astype(BF16)
    r1 = x - x1.astype(F32)
    x2 = r1.astype(BF16)
    x3 = (r1 - x2.astype(F32)).astype(BF16)
    return _bdot(m, x1) + _bdot(m, x2) + _bdot(m, x3)


def _gla_core_kernel(q_ref, k_ref, v_ref, la_ref, r_ref, g_ref, o_ref, st_ref):
    @pl.when(pl.program_id(2) == 0)
    def _():
        st_ref[...] = jnp.zeros_like(st_ref)

    c, t = GLA_CHUNK, GLA_T
    shift = c.bit_length() - 1
    row = lax.broadcasted_iota(jnp.int32, (t, t), 0)
    col = lax.broadcasted_iota(jnp.int32, (t, t), 1)
    causal = ((row >> shift) == (col >> shift)) & (row >= col)
    in_chunk = lax.broadcasted_iota(jnp.int32, (t, GLA_DK), 0) & (c - 1)
    chunks = [slice(ci * c, (ci + 1) * c) for ci in range(t // c)]
    for hh in range(st_ref.shape[0]):
        kcols = slice(hh * GLA_DK, (hh + 1) * GLA_DK)
        vcols = slice(hh * GLA_DV, (hh + 1) * GLA_DV)
        b = la_ref[0, :, kcols]
        step = 1
        while step < c:
            b = b + jnp.where(in_chunk >= step, pltpu.roll(b, step, axis=0), 0.0)
            step *= 2
        b_last = jnp.broadcast_to(b.reshape(t // c, c, GLA_DK)[:, c - 1:c, :],
                                  (t // c, c, GLA_DK)).reshape(t, GLA_DK)
        q = q_ref[0, :, kcols] * (GLA_DK ** -0.5)
        k = k_ref[0, :, kcols]
        vb = v_ref[0, :, vcols].astype(BF16)
        q_dec = (q * jnp.exp(b)).astype(BF16)
        k_neg = (k * jnp.exp(-b)).astype(BF16)
        k_dec = (k * jnp.exp(b_last - b)).astype(BF16)
        decay = jnp.exp(b_last)
        s = jnp.where(causal, _dot_nt(q_dec, k_neg), 0.0)
        o = _bdot(s.astype(BF16), vb)

        kv = [_dot_tn(vb[rows], k_dec[rows]) for rows in chunks]
        st = st_ref[hh]
        entering = []
        for ci, rows in enumerate(chunks):
            entering.append(st.astype(BF16))
            st = decay[ci * c:ci * c + 1] * st + kv[ci]
        st_ref[hh] = st
        o = o + jnp.concatenate([_dot_nt(q_dec[rows], s_in) for rows, s_in in zip(chunks, entering)], axis=0)
        ms = jnp.mean(o * o, axis=-1, keepdims=True)
        o = o * lax.rsqrt(ms + LN_EPS) * g_ref[...]
        o_ref[0, :, vcols] = o * _silu(r_ref[0, :, vcols])


def _gla_core(q, k, v, la, r, norm_g):
    bsz, s, _ = q.shape
    t = GLA_T
    hs = GLA_HEADS_PER_STEP
    kmap = lambda b, h, n: (b, n, h)
    kspec = pl.BlockSpec((1, t, hs * GLA_DK), kmap)
    vspec = pl.BlockSpec((1, t, hs * GLA_DV), kmap)
    return pl.pallas_call(
        _gla_core_kernel,
        grid=(bsz, GLA_HEADS // hs, s // t),
        in_specs=[kspec, kspec, vspec, kspec, vspec,
                  pl.BlockSpec((1, GLA_DV), lambda b, h, n: (0, 0))],
        out_specs=vspec,
        out_shape=jax.ShapeDtypeStruct((bsz, s, GLA_HEADS * GLA_DV), F32),
        scratch_shapes=[pltpu.VMEM((hs, GLA_DV, GLA_DK), F32)],
        compiler_params=_cparams(("arbitrary", "arbitrary", "arbitrary")),
        name="gla_core",
    )(q, k, v, la, r, norm_g)


def _row_xor(v, k, row):
    n = v.shape[0]
    up = pltpu.roll(v, n - k, axis=0)
    dn = pltpu.roll(v, k, axis=0)
    return jnp.where((row & k) == 0, up, dn)


def _route(x, wt_hi, wt_lo, bias):
    x_hi = x.astype(BF16)
    x_lo = (x - x_hi.astype(F32)).astype(BF16)
    logits = _dot_nt(wt_hi, x_hi) + (_dot_nt(wt_lo, x_hi) + _dot_nt(wt_hi, x_lo))
    s = jax.nn.sigmoid(logits)
    a = s + bias
    row = lax.broadcasted_iota(jnp.int32, a.shape, 0)
    a1 = _row_xor(a, 1, row)
    a2 = _row_xor(a, 2, row)
    a3 = _row_xor(a1, 2, row)
    p, q = jnp.maximum(a, a1), jnp.minimum(a, a1)
    r, t = jnp.maximum(a2, a3), jnp.minimum(a2, a3)
    gs = jnp.maximum(p, r) + jnp.maximum(jnp.minimum(p, r), jnp.maximum(q, t))
    g1 = _row_xor(gs, 4, row)
    g2 = _row_xor(gs, 8, row)
    g3 = _row_xor(g1, 8, row)

    def beats(other, other_first):
        return (other > gs) | ((other == gs) & other_first)

    lose = (beats(g1, (row & 4) != 0) | beats(g2, (row & 8) != 0) | beats(g3, (row & 8) != 0))

    def ahead(other, other_first):
        return jnp.where((other > a) | ((other == a) & other_first), 1.0, 0.0)

    rank = ahead(a1, (row & 1) != 0) + ahead(a2, (row & 2) != 0) + ahead(a3, (row & 2) != 0)
    sel = jnp.logical_not(lose) & (rank < 2.0)
    ssel = jnp.where(sel, s, 0.0)
    s1 = _row_xor(ssel, 1, row)
    tot = (ssel + s1) + (_row_xor(ssel, 2, row) + _row_xor(s1, 2, row))
    gate = jnp.where(sel, s / tot, 0.0)
    group = jnp.max(jnp.where(sel, (row >> 2).astype(F32), 0.0), axis=0, keepdims=True)
    tokens = x.shape[0]
    record = jnp.concatenate([gate, jnp.zeros((LANES - N_EXPERTS, tokens), F32)], axis=0)
    return record.T, group.astype(jnp.int32)


def _norm_route_store(z, g_ref, b_ref, wrh_ref, wrl_ref, br_ref, o_ref, grp_ref):
    x = _layer_norm(z, g_ref[...], b_ref[...])
    o_ref[:, :D_MODEL] = x
    o_ref[:, D_MODEL:], grp_ref[...] = _route(x, wrh_ref[...], wrl_ref[...], br_ref[...])


def _outproj_ln_kernel(h_ref, x_ref, w_ref, g_ref, b_ref, wrh_ref, wrl_ref, br_ref, o_ref, grp_ref):
    y = _bdot(h_ref[...].astype(BF16), w_ref[...])
    _norm_route_store(DN_ALPHA * x_ref[...] + y, g_ref, b_ref, wrh_ref, wrl_ref, br_ref, o_ref, grp_ref)


def _outproj_heads_ln_kernel(h_ref, x_ref, w_ref, g_ref, b_ref, wrh_ref, wrl_ref, br_ref, o_ref, grp_ref):
    h = jnp.concatenate([h_ref[0, i].astype(BF16) for i in range(h_ref.shape[1])], axis=1)
    _norm_route_store(DN_ALPHA * x_ref[...] + _bdot(h, w_ref[...]), g_ref, b_ref, wrh_ref, wrl_ref, br_ref,
                      o_ref, grp_ref)


def _outproj_call(kernel, name, h, h_spec, x2, w, g, b, router):
    n = x2.shape[0]
    tm = OUT_TM
    row = lambda i: (i, 0)
    full = lambda i: (0, 0)
    return pl.pallas_call(
        kernel,
        grid=(n // tm,),
        in_specs=[h_spec, pl.BlockSpec((tm, D_MODEL), row),
                  pl.BlockSpec(w.shape, full), pl.BlockSpec((1, D_MODEL), full), pl.BlockSpec((1, D_MODEL), full)]
                 + [pl.BlockSpec(r.shape, full) for r in router],
        out_specs=[pl.BlockSpec((tm, XA_WIDTH), row), pl.BlockSpec((1, tm), lambda i: (0, i))],
        out_shape=[jax.ShapeDtypeStruct((n, XA_WIDTH), F32), jax.ShapeDtypeStruct((1, n), jnp.int32)],
        compiler_params=_cparams(("arbitrary",)),
        name=name,
    )(h, x2, w, g, b, *router)


def _outproj_heads_ln(h4, x2, w, g, b, router):
    _, heads, seq, d = h4.shape
    per = seq // OUT_TM
    spec = pl.BlockSpec((1, heads, OUT_TM, d), lambda i: (i // per, 0, i % per, 0))
    return _outproj_call(_outproj_heads_ln_kernel, "outproj_heads_ln", h4, spec, x2, w, g, b, router)


def _outproj_ln(h2, x2, w, g, b, router):
    spec = pl.BlockSpec((OUT_TM, h2.shape[1]), lambda i: (i, 0))
    return _outproj_call(_outproj_ln_kernel, "outproj_ln", h2, spec, x2, w, g, b, router)


def _moe_sorted_kernel(dest_ref, glo_ref, ghi_ref, xa_hbm, wg_ref, wu_ref, wd_ref, lg_ref, lb_ref, out_hbm,
                       src_ref, xbuf, obuf, in_sem, out_sem):
    i = pl.program_id(0)
    last = pl.num_programs(0) - 1
    nt = 2 * pl.num_programs(0)
    ts = MOE_TS
    epg = EXPERTS_PER_GROUP

    def row_in(tile, sl, r):
        return pltpu.make_async_copy(xa_hbm.at[src_ref[tile * ts + r]], xbuf.at[sl, r], in_sem.at[sl])

    def row_out(tile, sl, r):
        return pltpu.make_async_copy(obuf.at[sl, r], out_hbm.at[src_ref[tile * ts + r]], out_sem.at[sl])

    def wait_in(sl):
        pltpu.make_async_copy(xa_hbm.at[pl.ds(0, ts)], xbuf.at[sl], in_sem.at[sl]).wait()

    def wait_out(sl):
        pltpu.make_async_copy(obuf.at[sl], out_hbm.at[pl.ds(0, ts)], out_sem.at[sl]).wait()

    @pl.when(i == 0)
    def _():
        def invert(n, carry):
            src_ref[dest_ref[n]] = n
            return carry

        lax.fori_loop(0, dest_ref.shape[0], invert, 0, unroll=8)
        for r in range(ts):
            row_in(0, 0, r).start()

    @pl.when(i >= 1)
    def _():
        wait_out(0)
        wait_out(1)

    for sl in range(2):
        tile = 2 * i + sl
        wait_in(sl)
        nxt = jnp.minimum(tile + 1, nt - 1)
        for r in range(ts):
            row_in(nxt, 1 - sl, r).start()

        xa = xbuf[sl]
        x = xa[:, :D_MODEL]
        gate = xa[:, D_MODEL:]
        xb = x.astype(BF16)

        def group_ffn(g, xb=xb, gate=gate):
            hs = []
            for e in range(epg):
                col = gate[:, e:e + 1]
                for other in range(1, N_EXPERT_GROUPS):
                    col = jnp.where(g == other, gate[:, other * epg + e:other * epg + e + 1], col)
                expert = g * epg + e
                hs.append((_silu(_bdot(xb, wg_ref[expert])) * _bdot(xb, wu_ref[expert]) * col).astype(BF16))
            return _bdot(jnp.concatenate(hs, axis=1), wd_ref[g])

        y = group_ffn(glo_ref[tile])
        y = lax.fori_loop(glo_ref[tile] + 1, ghi_ref[tile] + 1, lambda g, acc: acc + group_ffn(g), y)
        obuf[sl] = _layer_norm(DN_ALPHA * x + y, lg_ref[...], lb_ref[...])
        for r in range(ts):
            row_out(tile, sl, r).start()

    @pl.when(i == last)
    def _():
        wait_out(0)
        wait_out(1)
        wait_in(0)


def _moe_sorted(xa, dest, g_lo, g_hi, wg, wu, wd, lg, lb):
    n = xa.shape[0]
    ts = MOE_TS
    resident = pl.Buffered(1)
    whole = lambda i, *_: (0, 0, 0)
    grid_spec = pltpu.PrefetchScalarGridSpec(
        num_scalar_prefetch=3,
        grid=(n // (2 * ts),),
        in_specs=[pl.BlockSpec(memory_space=pl.ANY),
                  pl.BlockSpec(wg.shape, whole, pipeline_mode=resident),
                  pl.BlockSpec(wu.shape, whole, pipeline_mode=resident),
                  pl.BlockSpec(wd.shape, whole, pipeline_mode=resident),
                  pl.BlockSpec((1, D_MODEL), lambda i, *_: (0, 0)),
                  pl.BlockSpec((1, D_MODEL), lambda i, *_: (0, 0))],
        out_specs=pl.BlockSpec(memory_space=pl.ANY),
        scratch_shapes=[pltpu.SMEM((n,), jnp.int32),
                        pltpu.VMEM((2, ts, XA_WIDTH), F32), pltpu.VMEM((2, ts, D_MODEL), F32),
                        pltpu.SemaphoreType.DMA((2,)), pltpu.SemaphoreType.DMA((2,))])
    return pl.pallas_call(
        _moe_sorted_kernel,
        grid_spec=grid_spec,
        out_shape=jax.ShapeDtypeStruct((n, D_MODEL), F32),
        compiler_params=pltpu.CompilerParams(dimension_semantics=("arbitrary",), vmem_limit_bytes=MOE_VMEM_LIMIT),
        name="moe_sorted",
    )(dest, g_lo, g_hi, xa, wg, wu, wd, lg, lb)


def _cast_experts_kernel(wg_ref, wu_ref, wd_ref, og_ref, ou_ref, od_ref):
    og_ref[...] = wg_ref[0].astype(BF16)
    ou_ref[...] = wu_ref[0].astype(BF16)
    od_ref[...] = wd_ref[0].astype(BF16)


def _cast_experts(layer, w_gate, w_up, w_down):
    ws = (w_gate, w_up, w_down)
    return pl.pallas_call(
        _cast_experts_kernel,
        grid=(w_gate.shape[1],),
        in_specs=[pl.BlockSpec((1, 1) + w.shape[2:], lambda e: (layer, e, 0, 0)) for w in ws],
        out_specs=[pl.BlockSpec((1,) + w.shape[2:], lambda e: (e, 0, 0)) for w in ws],
        out_shape=[jax.ShapeDtypeStruct(w.shape[1:], BF16) for w in ws],
        compiler_params=_cparams(("arbitrary",)),
        name="cast_experts",
    )(*ws)


def _group_sort_plan(grp):
    n = grp.shape[1]
    onehot = (grp == jnp.arange(N_EXPERT_GROUPS, dtype=jnp.int32)[:, None]).astype(jnp.int32)
    csum = jnp.cumsum(onehot, axis=1)
    ends = jnp.cumsum(csum[:, -1])
    dest = jnp.sum(onehot * (csum - 1 + (ends - csum[:, -1])[:, None]), axis=0)
    first_row = jnp.arange(n // MOE_TS, dtype=jnp.int32) * MOE_TS
    group_of_row = lambda rows: jnp.sum(ends[None, :] <= rows[:, None], axis=1).astype(jnp.int32)
    return dest, group_of_row(first_row), group_of_row(first_row + (MOE_TS - 1))


def _nsa_proj_kernel(x_ref, wr_ref, wp_ref, wgt_ref, c_ref, s1_ref, s2_ref,
                     q_ref, kst_ref, kw_ref, kc_ref, vc_ref, vs_ref, vw_ref, gt_ref):
    xb = x_ref[...].astype(BF16)
    c, s1, s2 = c_ref[...], s1_ref[...], s2_ref[...]
    d = NSA_HEAD_DIM
    nq = NSA_HEADS * d // LANES
    ng = NSA_GROUPS * d // LANES
    first_head = lax.broadcasted_iota(jnp.int32, (x_ref.shape[0], LANES), 1) < d

    def put_rows(ref, j, t):
        ref[0, 2 * j] = t[:, :d].astype(ref.dtype)
        ref[0, 2 * j + 1] = t[:, d:].astype(ref.dtype)

    def put_with_ones(ref, j, t):
        ref[0, 2 * j] = jnp.where(first_head, t, 1.0).astype(ref.dtype)
        ref[0, 2 * j + 1] = jnp.where(first_head, pltpu.roll(t, d, axis=1), 1.0).astype(ref.dtype)

    def lane_blocks(w_ref, nblocks):
        per = MXU_COLS // LANES
        for j0 in range(0, nblocks, per):
            wide = _bdot(xb, w_ref[:, j0 * LANES:(j0 + per) * LANES])
            for k in range(per):
                yield j0 + k, wide[:, k * LANES:(k + 1) * LANES]

    for j, t in lane_blocks(wr_ref, nq + 2 * ng):
        t = _rope128(t, c, s1, s2)
        if j < nq:
            put_rows(q_ref, j, t * (d ** -0.5 * LOG2E))
        elif j < nq + ng:
            tt = t.T.astype(BF16)
            kst_ref[0, 2 * (j - nq), 0] = tt[:d]
            kst_ref[0, 2 * (j - nq) + 1, 0] = tt[d:]
        else:
            put_rows(kw_ref, j - nq - ng, t)
    plain = ((kc_ref, put_rows), (vc_ref, put_rows), (vs_ref, put_with_ones), (vw_ref, put_with_ones))
    for j, t in lane_blocks(wp_ref, len(plain) * ng):
        ref, put = plain[j // ng]
        put(ref, j % ng, t)
    gt_ref[...] = jax.nn.sigmoid(_bdot(xb, wgt_ref[...]))


def _nsa_proj(x2, w_rope, w_plain, w_gates, rc, rs1, rs2, seq):
    n = x2.shape[0]
    tm = PROJ_TM
    gd = NSA_GROUPS * NSA_HEAD_DIM
    row = lambda i: (i, 0)
    full = lambda i: (0, 0)
    per = seq // tm
    pos = lambda i: (i % per, 0)
    bsz, g, d = n // seq, NSA_GROUPS, NSA_HEAD_DIM
    sub = ATT_TK // tm
    hmap = lambda i: (i // per, 0, i % per, 0)

    def rows_out(heads, width, dtype):
        return pl.BlockSpec((1, heads, tm, width), hmap), jax.ShapeDtypeStruct((bsz, heads, seq, width), dtype)

    outs = [rows_out(NSA_HEADS, d, BF16),
            (pl.BlockSpec((1, g, 1, d, tm), lambda i: (i // per, 0, (i % per) // sub, 0, (i % per) % sub)),
             jax.ShapeDtypeStruct((bsz, g, seq // ATT_TK, d, ATT_TK), BF16)),
            rows_out(g, d, BF16),
            rows_out(g, d, F32), rows_out(g, d, F32),
            rows_out(g, 2 * d, BF16), rows_out(g, 2 * d, BF16),
            (pl.BlockSpec((tm, g * LANES), row), jax.ShapeDtypeStruct((n, g * LANES), F32))]
    return pl.pallas_call(
        _nsa_proj_kernel,
        grid=(n // tm,),
        in_specs=[pl.BlockSpec((tm, D_MODEL), row),
                  pl.BlockSpec(w_rope.shape, full), pl.BlockSpec(w_plain.shape, full),
                  pl.BlockSpec(w_gates.shape, full),
                  pl.BlockSpec((tm, LANES), pos), pl.BlockSpec((tm, LANES), pos), pl.BlockSpec((tm, LANES), pos)],
        out_specs=[o[0] for o in outs],
        out_shape=[o[1] for o in outs],
        compiler_params=_cparams(("arbitrary",)),
        name="nsa_proj",
    )(x2, w_rope, w_plain, w_gates, rc, rs1, rs2)


def _compress_kernel(ck_ref, cv_ref, pet_ref, peb_ref, wk1_ref, wk2_ref, wv1_ref, wv2_ref,
                     c_ref, s1_ref, s2_ref, ko_ref, vo_ref):
    nch = ck_ref.shape[2] // CMP_STRIDE
    half = CMP_STRIDE * NSA_HEAD_DIM
    pet, peb = pet_ref[...], peb_ref[...]

    def chunk_rows(ref):
        return jnp.concatenate([ref[0, 0, pl.ds(t, nch, stride=CMP_STRIDE), :] for t in range(CMP_STRIDE)], axis=1)

    def mlp(ch, w1_ref, w2_ref):
        a = _bdot((ch + pet).astype(BF16), w1_ref[0:half, :])
        bm = _bdot((ch + peb).astype(BF16), w1_ref[half:2 * half, :])
        h = _silu(a + pltpu.roll(bm, nch - 1, axis=0))
        return _bdot(h.astype(BF16), w2_ref[...])

    kc = _rope128(mlp(chunk_rows(ck_ref), wk1_ref, wk2_ref), c_ref[...], s1_ref[...], s2_ref[...])
    ko_ref[0, 0] = kc[:, :NSA_HEAD_DIM].astype(BF16)
    vo_ref[0, 0] = mlp(chunk_rows(cv_ref), wv1_ref, wv2_ref)[:, :NSA_HEAD_DIM].astype(BF16)


def _compress(ck, cv, pet, peb, wk1, wk2, wv1, wv2, cc, cs1, cs2):
    bsz, g, seq, width = ck.shape
    nch = seq // CMP_STRIDE
    blk = lambda b, gi: (b, gi, 0, 0)
    full = lambda b, gi: (0, 0)
    return pl.pallas_call(
        _compress_kernel,
        grid=(bsz, g),
        in_specs=[pl.BlockSpec((1, 1, seq, width), blk), pl.BlockSpec((1, 1, seq, width), blk),
                  pl.BlockSpec(pet.shape, full), pl.BlockSpec(peb.shape, full),
                  pl.BlockSpec(wk1.shape, full), pl.BlockSpec(wk2.shape, full),
                  pl.BlockSpec(wv1.shape, full), pl.BlockSpec(wv2.shape, full),
                  pl.BlockSpec(cc.shape, full), pl.BlockSpec(cs1.shape, full), pl.BlockSpec(cs2.shape, full)],
        out_specs=[pl.BlockSpec((1, 1, nch, NSA_HEAD_DIM), blk), pl.BlockSpec((1, 1, nch, NSA_HEAD_DIM), blk)],
        out_shape=[jax.ShapeDtypeStruct((bsz, g, nch, NSA_HEAD_DIM), BF16)] * 2,
        compiler_params=_cparams(("arbitrary", "arbitrary")),
        name="nsa_compress",
    )(ck, cv, pet, peb, wk1, wk2, wv1, wv2, cc, cs1, cs2)


def _cmp_select_kernel(q_ref, kc_ref, vc_ref, wimp_ref, o_ref, sel_ref):
    tq = q_ref.shape[2]
    nch = kc_ref.shape[2]
    ns = wimp_ref.shape[0]
    s0 = pl.program_id(2) * tq
    ratio = SLC_BLOCK // CMP_STRIDE
    n_vis = (s0 + tq - CMP_BLOCK) // CMP_STRIDE + 1
    n_chunks = (n_vis + CMP_CW - 1) // CMP_CW

    def body(w):
        nb = w // ratio
        qpos_c = s0 + lax.broadcasted_iota(jnp.int32, (tq, 1), 0)
        cend = lax.broadcasted_iota(jnp.int32, (1, w), 1) * CMP_STRIDE + (CMP_BLOCK - 1)
        cmask = cend <= qpos_c
        anyvis = (qpos_c >= CMP_BLOCK - 1).astype(F32)
        blk = lax.broadcasted_iota(jnp.int32, (nb, tq), 0)
        cur = (s0 + lax.broadcasted_iota(jnp.int32, (nb, tq), 1)) >> SLC_SHIFT
        forced = (blk == 0) | (blk == cur) | (blk == cur - 1)
        visible = blk <= cur
        for gi in range(kc_ref.shape[1]):
            group_body(gi, w, nb, cmask, anyvis, forced, visible)

    def group_body(gi, w, nb, cmask, anyvis, forced, visible):
        d = q_ref.shape[3]
        hpg = q_ref.shape[1] // kc_ref.shape[1]
        heads = slice(gi * hpg, (gi + 1) * hpg)
        s = _dot_nt(q_ref[0, heads].reshape(hpg * tq, d), kc_ref[0, gi, :w, :]).reshape(hpg, tq, w)
        s = jnp.where(cmask[None], s, NEG)
        e = jnp.exp2(s - jnp.max(s, axis=-1, keepdims=True))
        p = e * (anyvis[None] / jnp.sum(e, axis=-1, keepdims=True))
        o_ref[0, heads] = _bdot(p.reshape(hpg * tq, w).astype(BF16), vc_ref[0, gi, :w, :]).reshape(hpg, tq, d)
        pg = jnp.sum(p, axis=0)

        pg_hi = pg.astype(BF16)
        pg_lo = (pg - pg_hi.astype(F32)).astype(BF16)
        wimp = wimp_ref[:nb, :w]
        imp = _dot_nt(wimp, pg_hi) + _dot_nt(wimp, pg_lo)
        imp = jnp.where(visible, jnp.where(forced, BIG, imp), NEG)
        parts = [slice(i, i + LANES) for i in range(0, tq, LANES)]
        blk_f = lax.broadcasted_iota(jnp.int32, (nb, LANES), 0).astype(F32)
        imps = [imp[:, pt] for pt in parts]
        sels = [jnp.zeros((nb, LANES), F32) for _ in parts]
        for _ in range(min(SLC_TOP_N, nb)):
            for i in range(len(parts)):
                m = jnp.max(imps[i], axis=0, keepdims=True)
                first = jnp.min(jnp.where(imps[i] == m, blk_f, float(nb)), axis=0, keepdims=True)
                hit = blk_f == first
                sels[i] = jnp.where(hit, 1.0, sels[i])
                imps[i] = jnp.where(hit, -jnp.inf, imps[i])
        sel = jnp.concatenate(sels, axis=1)
        if nb < ns:
            sel = jnp.concatenate([sel, jnp.zeros((ns - nb, tq), F32)], axis=0)
        sel_ref[0, gi] = jnp.where(sel.T > 0.0, 0.0, NEG).astype(BF16)

    for nv in range(1, nch // CMP_CW + 1):
        pl.when(n_chunks == nv)(functools.partial(body, nv * CMP_CW))


def _cmp_select(q, k_cmp, v_cmp, wimp_t):
    bsz, _, s, d = q.shape
    g = NSA_GROUPS
    ng = CMP_GROUPS
    nch = k_cmp.shape[2]
    ns = wimp_t.shape[0]
    tq = min(CMP_TQ, s)
    hmap = lambda b, gp, qi: (b, gp, qi, 0)
    return pl.pallas_call(
        _cmp_select_kernel,
        grid=(bsz, g // ng, s // tq),
        in_specs=[pl.BlockSpec((1, ng * NSA_HPG, tq, d), hmap),
                  pl.BlockSpec((1, ng, nch, d), lambda b, gp, qi: (b, gp, 0, 0)),
                  pl.BlockSpec((1, ng, nch, d), lambda b, gp, qi: (b, gp, 0, 0)),
                  pl.BlockSpec(wimp_t.shape, lambda b, gp, qi: (0, 0))],
        out_specs=[pl.BlockSpec((1, ng * NSA_HPG, tq, d), hmap),
                   pl.BlockSpec((1, ng, tq, ns), hmap)],
        out_shape=[jax.ShapeDtypeStruct((bsz, NSA_HEADS, s, d), F32),
                   jax.ShapeDtypeStruct((bsz, g, s, ns), BF16)],
        compiler_params=_cparams(("arbitrary", "arbitrary", "arbitrary")),
        name="nsa_cmp_select",
    )(q, k_cmp, v_cmp, wimp_t)


def _nsa_attn_kernel(q_ref, kst_ref, vs_ref, kw_ref, vw_ref, selb_ref, eexp_ref, ocmp_ref, gt_ref,
                     o_ref, m_ref, acc_ref, s_ref, p_ref, a_ref):
    ng = kst_ref.shape[1]
    hpg = q_ref.shape[1] // ng
    tq, d = q_ref.shape[2:]
    rows = hpg * tq
    s0 = pl.program_id(2) * tq
    qpos = s0 + lax.broadcasted_iota(jnp.int32, (tq, 1), 0)
    q4 = [q_ref[0, gi * hpg:(gi + 1) * hpg].reshape(rows, d) for gi in range(ng)]
    q4a = [jnp.concatenate([jnp.concatenate([selb_ref[0, gi]] * hpg, axis=0), q4[gi]], axis=1)
           for gi in range(ng)]

    def scores(gi, t):
        return _bdot(q4a[gi], jnp.concatenate([eexp_ref[t], kst_ref[0, gi, t]], axis=0))

    def flush(gi, t):
        acc_ref[gi] = a_ref[gi] * acc_ref[gi] + _bdot(p_ref[gi], vs_ref[0, gi, t])

    def softmax_step(gi, s):
        m_prev = m_ref[gi]
        m_new = jnp.maximum(m_prev, jnp.max(s, axis=-1, keepdims=True))
        a_ref[gi] = jnp.exp2(m_prev - m_new)
        p_ref[gi] = jnp.exp2(s - m_new).astype(BF16)
        m_ref[gi] = m_new

    m_ref[...] = jnp.full_like(m_ref, NEG)
    acc_ref[...] = jnp.zeros_like(acc_ref)
    p_ref[...] = jnp.zeros_like(p_ref)
    a_ref[...] = jnp.ones_like(a_ref)
    t_diag = s0 // ATT_TK
    for gi in range(ng):
        s_ref[gi] = scores(gi, 0)

    def step(t):
        for gi in range(ng):
            s = s_ref[gi]
            s_next = scores(gi, t + 1)
            flush(gi, jnp.maximum(t - 1, 0))
            softmax_step(gi, s)
            s_ref[gi] = s_next

    def pair_body(i, carry):
        step(2 * i)
        step(2 * i + 1)
        return carry

    lax.fori_loop(0, t_diag // 2, pair_body, 0)
    pl.when(t_diag % 2 == 1)(lambda: step(t_diag - 1))

    span = WINDOW + tq
    w0 = pl.multiple_of(jnp.maximum(s0 - WINDOW, 0), tq)
    rel = qpos - (w0 + lax.broadcasted_iota(jnp.int32, (1, span), 1))
    wbias = jnp.where((rel >= 0) & (rel < WINDOW), 0.0, NEG)
    kpos = t_diag * ATT_TK + lax.broadcasted_iota(jnp.int32, (1, ATT_TK), 1)
    causal = (kpos <= qpos)[None]
    for gi in range(ng):
        sw = _dot_nt(q4[gi], kw_ref[0, gi, pl.ds(w0, span), :])
        flush(gi, jnp.maximum(t_diag - 1, 0))
        s = s_ref[gi].reshape(hpg, tq, ATT_TK)
        softmax_step(gi, jnp.where(causal, s, NEG).reshape(rows, ATT_TK))
        sw = (sw.reshape(hpg, tq, span) + wbias[None]).reshape(rows, span)
        pw = jnp.exp2(sw - jnp.max(sw, axis=-1, keepdims=True)).astype(BF16)
        flush(gi, t_diag)
        accw = _bdot(pw, vw_ref[0, gi, pl.ds(w0, span), :])
        o_win = (accw[:, :d] / accw[:, d:d + 1]).reshape(hpg, tq, d)
        acc = acc_ref[gi]
        o_slc = (acc[:, :d] / acc[:, d:d + 1]).reshape(hpg, tq, d)
        gts = gt_ref[0, :, gi * LANES:(gi + 1) * LANES]
        for h in range(hpg):
            g0, g1, g2 = (gts[:, 3 * h + j:3 * h + j + 1] for j in range(3))
            o_ref[0, gi * hpg + h] = g0 * ocmp_ref[0, gi * hpg + h] + g1 * o_slc[h] + g2 * o_win[h]


def _nsa_attn(q, kst, vst, kw, vw, selb, eexp, o_cmp, gates):
    bsz, _, s, d = q.shape
    ng = ATT_GROUPS
    tq = ATT_TQ
    heads = ng * NSA_HPG
    rows = NSA_HPG * tq
    ns = selb.shape[-1]
    hmap = lambda b, gp, qi: (b, gp, qi, 0)
    once = pl.Buffered(1)
    return pl.pallas_call(
        _nsa_attn_kernel,
        grid=(bsz, NSA_GROUPS // ng, s // tq),
        in_specs=[pl.BlockSpec((1, heads, tq, d), hmap),
                  pl.BlockSpec((1, ng) + kst.shape[2:], lambda b, gp, qi: (b, gp, 0, 0, 0), pipeline_mode=once),
                  pl.BlockSpec((1, ng) + vst.shape[2:], lambda b, gp, qi: (b, gp, 0, 0, 0), pipeline_mode=once),
                  pl.BlockSpec((1, ng) + kw.shape[2:], lambda b, gp, qi: (b, gp, 0, 0), pipeline_mode=once),
                  pl.BlockSpec((1, ng) + vw.shape[2:], lambda b, gp, qi: (b, gp, 0, 0), pipeline_mode=once),
                  pl.BlockSpec((1, ng, tq, ns), hmap),
                  pl.BlockSpec(eexp.shape, lambda b, gp, qi: (0, 0, 0), pipeline_mode=once),
                  pl.BlockSpec((1, heads, tq, d), hmap),
                  pl.BlockSpec((1, tq, ng * LANES), lambda b, gp, qi: (b, qi, gp))],
        out_specs=pl.BlockSpec((1, heads, tq, d), hmap),
        out_shape=jax.ShapeDtypeStruct((bsz, NSA_HEADS, s, d), F32),
        scratch_shapes=[pltpu.VMEM((ng, rows, 1), F32),
                        pltpu.VMEM((ng, rows, 2 * d), F32),
                        pltpu.VMEM((ng, rows, ATT_TK), F32),
                        pltpu.VMEM((ng, rows, ATT_TK), BF16),
                        pltpu.VMEM((ng, rows, 1), F32)],
        compiler_params=_cparams(("arbitrary", "arbitrary", "arbitrary")),
        name="nsa_attn",
    )(q, kst, vst, kw, vw, selb, eexp, o_cmp, gates)


def _rope_tables(pos):
    inv_freq = jnp.power(ROPE_THETA, -jnp.arange(ROT_HALF, dtype=F32) * (2.0 / ROT_DIM))
    ang = pos.astype(F32)[:, None] * inv_freq[None, :]
    cos, sin = jnp.cos(ang), jnp.sin(ang)
    n = pos.shape[0]
    one = jnp.ones((n, NSA_HEAD_DIM - ROT_DIM), F32)
    zero8 = jnp.zeros((n, ROT_HALF), F32)
    zero = jnp.zeros((n, NSA_HEAD_DIM - ROT_DIM), F32)
    c = jnp.concatenate([cos, cos, one], axis=1)
    s1 = jnp.concatenate([-sin, zero8, zero], axis=1)
    s2 = jnp.concatenate([zero8, sin, zero], axis=1)
    rep = LANES // NSA_HEAD_DIM
    return tuple(jnp.tile(t, (1, rep)) for t in (c, s1, s2))


def _importance_weights(ns, nch):
    ratio = SLC_BLOCK // CMP_STRIDE
    span = CMP_BLOCK // CMP_STRIDE
    w = np.zeros((ns, nch), np.float32)
    for j in range(ns):
        for m in range(ratio):
            for n in range(span):
                c = ratio * j + m + n
                if c < nch - 1:
                    w[j, c] += 1.0
    return jnp.asarray(w, dtype=BF16)


def _block_expansion(nt, ns):
    key_blk = (np.arange(nt)[:, None] * ATT_TK + np.arange(ATT_TK)[None, :]) // SLC_BLOCK
    e = (np.arange(ns)[None, :, None] == key_blk[:, None, :]).astype(np.float32)
    return jnp.asarray(e, dtype=BF16)


def _gla_layer(x2, bsz, seq, w_in, w_gate_up, b_gate, norm_g, w_out, ln_g, ln_b, router):
    hk = GLA_HEADS * GLA_DK
    hv = GLA_HEADS * GLA_DV
    cuts = np.cumsum([hk, hk, hv, GLA_GATE_RANK]).tolist()
    wq, wk, wv, wg, wr = jnp.split(w_in, cuts, axis=1)
    w_main = jnp.concatenate([wq, wk, wv, wr], axis=1).astype(BF16)
    w_glow = jnp.pad(wg, ((0, 0), (0, LANES - GLA_GATE_RANK))).astype(BF16)
    w_gu = jnp.pad(w_gate_up, ((0, LANES - GLA_GATE_RANK), (0, 0))).astype(BF16)
    q, k, v, r, la = _gla_proj(x2, w_main, w_glow, w_gu, b_gate.reshape(1, hk))
    sh = lambda t: t.reshape(bsz, seq, t.shape[-1])
    o = _gla_core(sh(q), sh(k), sh(v), sh(la), sh(r), norm_g.reshape(1, GLA_DV))
    return _outproj_ln(o.reshape(bsz * seq, hv), x2, w_out.astype(BF16),
                       ln_g.reshape(1, D_MODEL), ln_b.reshape(1, D_MODEL), router)


def _nsa_layer(x2, bsz, seq, w_in, w_ck1, w_ck2, w_cv1, w_cv2, cmp_pe, w_out, ln_g, ln_b, router):
    h, g, hpg, d = NSA_HEADS, NSA_GROUPS, NSA_HPG, NSA_HEAD_DIM
    gd = g * d
    cuts = np.cumsum([h * d] + [gd] * 6).tolist()
    wq, wkc, wvc, wks, wvs, wkw, wvw, wgt = jnp.split(w_in, cuts, axis=1)
    w_rope = jnp.concatenate([wq, wks, wkw], axis=1).astype(BF16)
    w_plain = jnp.concatenate([wkc, wvc, wvs, wvw], axis=1).astype(BF16)
    w_gates = jnp.pad(wgt.reshape(D_MODEL, g, hpg * 3), ((0, 0), (0, 0), (0, LANES - hpg * 3)))
    w_gates = w_gates.reshape(D_MODEL, g * LANES).astype(BF16)
    rc, rs1, rs2 = _rope_tables(jnp.arange(seq, dtype=jnp.int32))
    q, kst, kw, kc, vc, vs1, vw1, gates = _nsa_proj(x2, w_rope, w_plain, w_gates, rc, rs1, rs2, seq)

    nch = seq // CMP_STRIDE
    pe = cmp_pe.reshape(1, CMP_BLOCK * d)
    half = CMP_STRIDE * d
    pad2 = lambda w: jnp.pad(w, ((0, 0), (0, LANES - d))).astype(BF16)
    cc, cs1, cs2 = _rope_tables(jnp.arange(nch, dtype=jnp.int32) * CMP_STRIDE + (CMP_BLOCK - 1))
    k_cmp, v_cmp = _compress(kc, vc, pe[:, :half], pe[:, half:],
                             w_ck1.astype(BF16), pad2(w_ck2), w_cv1.astype(BF16), pad2(w_cv2), cc, cs1, cs2)

    ns = seq // SLC_BLOCK
    sh = lambda t: t.reshape(bsz, seq, t.shape[-1])
    o_cmp, selb = _cmp_select(q, k_cmp, v_cmp, _importance_weights(ns, nch))

    nt = seq // ATT_TK
    o = _nsa_attn(q, kst, vs1.reshape(bsz, g, nt, ATT_TK, 2 * d), kw, vw1,
                  selb, _block_expansion(nt, ns), o_cmp, sh(gates))
    return _outproj_heads_ln(o, x2, w_out.astype(BF16), ln_g.reshape(1, D_MODEL), ln_b.reshape(1, D_MODEL), router)


def _moe_layer(xa, grp, layer, w_gate, w_up, w_down, ln_g, ln_b):
    wg, wu, wd = _cast_experts(layer, w_gate, w_up, w_down)
    wd = wd.reshape(N_EXPERT_GROUPS, EXPERTS_PER_GROUP * D_FF_EXPERT, D_MODEL)
    dest, g_lo, g_hi = _group_sort_plan(grp)
    return _moe_sorted(xa, dest, g_lo, g_hi, wg, wu, wd, ln_g.reshape(1, D_MODEL), ln_b.reshape(1, D_MODEL))


def kernel(x, gla_w_in, gla_w_gate_up, gla_b_gate, gla_norm_g, gla_w_out, nsa_w_in, nsa_w_cmp_k1, nsa_w_cmp_k2,
           nsa_w_cmp_v1, nsa_w_cmp_v2, nsa_cmp_pe, nsa_w_out, moe_w_router, moe_b_router, moe_w_gate, moe_w_up,
           moe_w_down, ln_g, ln_b):
    bsz, seq, _ = x.shape
    x2 = x.reshape(bsz * seq, D_MODEL)
    w_router_t = moe_w_router.T
    wr_hi = w_router_t.astype(BF16)
    wr_lo = (w_router_t - wr_hi.astype(F32)).astype(BF16)
    router = (wr_hi, wr_lo, moe_b_router.reshape(N_EXPERTS, 1))
    for i in range(DEPTH):
        j = i // 2
        if i % 2 == 0:
            xa, grp = _gla_layer(x2, bsz, seq, gla_w_in[j], gla_w_gate_up[j], gla_b_gate[j], gla_norm_g[j],
                                 gla_w_out[j], ln_g[i, 0], ln_b[i, 0], router)
        else:
            xa, grp = _nsa_layer(x2, bsz, seq, nsa_w_in[j], nsa_w_cmp_k1[j], nsa_w_cmp_k2[j], nsa_w_cmp_v1[j],
                                 nsa_w_cmp_v2[j], nsa_cmp_pe[j], nsa_w_out[j], ln_g[i, 0], ln_b[i, 0], router)
        x2 = _moe_layer(xa, grp, i, moe_w_gate, moe_w_up, moe_w_down, ln_g[i, 1], ln_b[i, 1])
    return x2.reshape(bsz, seq, D_MODEL)
```

```python
import functools

import numpy as np
import jax
import jax.numpy as jnp
from jax import lax
from jax.experimental import pallas as pl
from jax.experimental.pallas import tpu as pltpu

F32 = jnp.float32
BF16 = jnp.bfloat16

D_MODEL = 1024
DEPTH = 2

GLA_HEADS = 4
GLA_DK = D_MODEL // 2 // GLA_HEADS
GLA_DV = D_MODEL // GLA_HEADS
GLA_GATE_RANK = 16
GLA_TAU = 16.0
GLA_CHUNK = 64

NSA_HEADS = 16
NSA_GROUPS = 4
NSA_HPG = NSA_HEADS // NSA_GROUPS
NSA_HEAD_DIM = D_MODEL // NSA_HEADS
CMP_BLOCK = 32
CMP_STRIDE = 16
SLC_BLOCK = 64
SLC_SHIFT = SLC_BLOCK.bit_length() - 1
SLC_TOP_N = 16
WINDOW = 512

ROPE_THETA = 500000.0
ROT_DIM = NSA_HEAD_DIM // 4
ROT_HALF = ROT_DIM // 2

N_EXPERTS = 16
N_EXPERT_GROUPS = 4
EXPERTS_PER_GROUP = N_EXPERTS // N_EXPERT_GROUPS
D_FF_EXPERT = D_MODEL // 4

DN_ALPHA = (2 * DEPTH) ** 0.25
LN_EPS = 1e-5
NEG = -1e30
BIG = 1e30
LOG2E = 1.4426950408889634

LANES = 128
MXU_COLS = 256
VMEM_LIMIT = 48 * 1024 * 1024

PROJ_TM = 256
OUT_TM = 512
GLA_T = 512
GLA_HEADS_PER_STEP = 4
MOE_TS = 512
MOE_VMEM_LIMIT = 56 * 1024 * 1024
XA_WIDTH = D_MODEL + LANES
CMP_TQ = 512
CMP_CW = 128
CMP_GROUPS = 2
ATT_TQ = 128
ATT_TK = 512
ATT_GROUPS = 2


def _cparams(sem):
    return pltpu.CompilerParams(dimension_semantics=sem, vmem_limit_bytes=VMEM_LIMIT)


def _bdot(a, b):
    return jnp.dot(a, b, preferred_element_type=F32)


def _dot_nt(a, b):
    return lax.dot_general(a, b, (((1,), (1,)), ((), ())), preferred_element_type=F32)


def _dot_tn(a, b):
    return lax.dot_general(a, b, (((0,), (0,)), ((), ())), preferred_element_type=F32)


def _layer_norm(z, g, b):
    mu = jnp.mean(z, axis=-1, keepdims=True)
    zc = z - mu
    var = jnp.mean(zc * zc, axis=-1, keepdims=True)
    return zc * lax.rsqrt(var + LN_EPS) * g + b


def _silu(t):
    return t * (0.5 * jnp.tanh(0.5 * t) + 0.5)


def _rope128(t, c, s1, s2):
    up = pltpu.roll(t, LANES - ROT_HALF, axis=1)
    dn = pltpu.roll(t, ROT_HALF, axis=1)
    return t * c + up * s1 + dn * s2


def _gla_proj_kernel(x_ref, wm_ref, wg_ref, wgu_ref, bg_ref, q_ref, k_ref, v_ref, r_ref, la_ref):
    xb = x_ref[...].astype(BF16)
    hk = GLA_HEADS * GLA_DK
    hv = GLA_HEADS * GLA_DV
    q_ref[...] = _bdot(xb, wm_ref[:, 0:hk])
    k_ref[...] = _bdot(xb, wm_ref[:, hk:2 * hk])
    v_ref[...] = _bdot(xb, wm_ref[:, 2 * hk:2 * hk + hv])
    r_ref[...] = _bdot(xb, wm_ref[:, 2 * hk + hv:2 * hk + 2 * hv])
    g_low = _bdot(xb, wg_ref[...])
    z = _bdot(g_low.astype(BF16), wgu_ref[...]) + bg_ref[...]
    log_sig = jnp.minimum(z, 0.0) - jnp.log1p(jnp.exp(-jnp.abs(z)))
    la_ref[...] = log_sig * (1.0 / GLA_TAU)


def _gla_proj(x2, w_main, w_glow, w_gu, b_gate):
    n = x2.shape[0]
    hk = GLA_HEADS * GLA_DK
    hv = GLA_HEADS * GLA_DV
    tm = PROJ_TM
    row = lambda i: (i, 0)
    full = lambda i: (0, 0)
    return pl.pallas_call(
        _gla_proj_kernel,
        grid=(n // tm,),
        in_specs=[pl.BlockSpec((tm, D_MODEL), row),
                  pl.BlockSpec(w_main.shape, full),
                  pl.BlockSpec(w_glow.shape, full),
                  pl.BlockSpec(w_gu.shape, full),
                  pl.BlockSpec(b_gate.shape, full)],
        out_specs=[pl.BlockSpec((tm, hk), row), pl.BlockSpec((tm, hk), row),
                   pl.BlockSpec((tm, hv), row), pl.BlockSpec((tm, hv), row),
                   pl.BlockSpec((tm, hk), row)],
        out_shape=[jax.ShapeDtypeStruct((n, hk), F32), jax.ShapeDtypeStruct((n, hk), F32),
                   jax.ShapeDtypeStruct((n, hv), F32), jax.ShapeDtypeStruct((n, hv), F32),
                   jax.ShapeDtypeStruct((n, hk), F32)],
        compiler_params=_cparams(("arbitrary",)),
        name="gla_proj",
    )(x2, w_main, w_glow, w_gu, b_gate)


def _gla_core_kernel(q_ref, k_ref, v_ref, la_ref, r_ref, g_ref, o_ref, st_ref):
    @pl.when(pl.program_id(2) == 0)
    def _():
        st_ref[...] = jnp.zeros_like(st_ref)

    c, t = GLA_CHUNK, GLA_T
    shift = c.bit_length() - 1
    row = lax.broadcasted_iota(jnp.int32, (t, t), 0)
    col = lax.broadcasted_iota(jnp.int32, (t, t), 1)
    causal = ((row >> shift) == (col >> shift)) & (row >= col)
    in_chunk = lax.broadcasted_iota(jnp.int32, (t, GLA_DK), 0) & (c - 1)
    chunks = [slice(ci * c, (ci + 1) * c) for ci in range(t // c)]
    for hh in range(st_ref.shape[0]):
        kcols = slice(hh * GLA_DK, (hh + 1) * GLA_DK)
        vcols = slice(hh * GLA_DV, (hh + 1) * GLA_DV)
        b = la_ref[0, :, kcols]
        step = 1
        while step < c:
            b = b + jnp.where(in_chunk >= step, pltpu.roll(b, step, axis=0), 0.0)
            step *= 2
        b_last = jnp.broadcast_to(b.reshape(t // c, c, GLA_DK)[:, c - 1:c, :],
                                  (t // c, c, GLA_DK)).reshape(t, GLA_DK)
        q = q_ref[0, :, kcols] * (GLA_DK ** -0.5)
        k = k_ref[0, :, kcols]
        vb = v_ref[0, :, vcols].astype(BF16)
        q_dec = (q * jnp.exp(b)).astype(BF16)
        k_neg = (k * jnp.exp(-b)).astype(BF16)
        k_dec = (k * jnp.exp(b_last - b)).astype(BF16)
        decay = jnp.exp(b_last)
        s = jnp.where(causal, _dot_nt(q_dec, k_neg), 0.0)
        o = _bdot(s.astype(BF16), vb)

        kv = [_dot_tn(vb[rows], k_dec[rows]) for rows in chunks]
        st = st_ref[hh]
        entering = []
        for ci, rows in enumerate(chunks):
            entering.append(st.astype(BF16))
            st = decay[ci * c:ci * c + 1] * st + kv[ci]
        st_ref[hh] = st
        o = o + jnp.concatenate([_dot_nt(q_dec[rows], s_in) for rows, s_in in zip(chunks, entering)], axis=0)
        ms = jnp.mean(o * o, axis=-1, keepdims=True)
        o = o * lax.rsqrt(ms + LN_EPS) * g_ref[...]
        o_ref[0, :, vcols] = o * _silu(r_ref[0, :, vcols])


def _gla_core(q, k, v, la, r, norm_g):
    bsz, s, _ = q.shape
    t = GLA_T
    hs = GLA_HEADS_PER_STEP
    kmap = lambda b, h, n: (b, n, h)
    kspec = pl.BlockSpec((1, t, hs * GLA_DK), kmap)
    vspec = pl.BlockSpec((1, t, hs * GLA_DV), kmap)
    return pl.pallas_call(
        _gla_core_kernel,
        grid=(bsz, GLA_HEADS // hs, s // t),
        in_specs=[kspec, kspec, vspec, kspec, vspec,
                  pl.BlockSpec((1, GLA_DV), lambda b, h, n: (0, 0))],
        out_specs=vspec,
        out_shape=jax.ShapeDtypeStruct((bsz, s, GLA_HEADS * GLA_DV), F32),
        scratch_shapes=[pltpu.VMEM((hs, GLA_DV, GLA_DK), F32)],
        compiler_params=_cparams(("arbitrary", "arbitrary", "arbitrary")),
        name="gla_core",
    )(q, k, v, la, r, norm_g)


def _row_xor(v, k, row):
    n = v.shape[0]
    up = pltpu.roll(v, n - k, axis=0)
    dn = pltpu.roll(v, k, axis=0)
    return jnp.where((row & k) == 0, up, dn)


def _route(x, wt_hi, wt_lo, bias):
    x_hi = x.astype(BF16)
    x_lo = (x - x_hi.astype(F32)).astype(BF16)
    logits = _dot_nt(wt_hi, x_hi) + (_dot_nt(wt_lo, x_hi) + _dot_nt(wt_hi, x_lo))
    s = jax.nn.sigmoid(logits)
    a = s + bias
    row = lax.broadcasted_iota(jnp.int32, a.shape, 0)
    a1 = _row_xor(a, 1, row)
    a2 = _row_xor(a, 2, row)
    a3 = _row_xor(a1, 2, row)
    p, q = jnp.maximum(a, a1), jnp.minimum(a, a1)
    r, t = jnp.maximum(a2, a3), jnp.minimum(a2, a3)
    gs = jnp.maximum(p, r) + jnp.maximum(jnp.minimum(p, r), jnp.maximum(q, t))
    g1 = _row_xor(gs, 4, row)
    g2 = _row_xor(gs, 8, row)
    g3 = _row_xor(g1, 8, row)

    def beats(other, other_first):
        return (other > gs) | ((other == gs) & other_first)

    lose = (beats(g1, (row & 4) != 0) | beats(g2, (row & 8) != 0) | beats(g3, (row & 8) != 0))

    def ahead(other, other_first):
        return jnp.where((other > a) | ((other == a) & other_first), 1.0, 0.0)

    rank = ahead(a1, (row & 1) != 0) + ahead(a2, (row & 2) != 0) + ahead(a3, (row & 2) != 0)
    sel = jnp.logical_not(lose) & (rank < 2.0)
    ssel = jnp.where(sel, s, 0.0)
    s1 = _row_xor(ssel, 1, row)
    tot = (ssel + s1) + (_row_xor(ssel, 2, row) + _row_xor(s1, 2, row))
    gate = jnp.where(sel, s / tot, 0.0)
    group = jnp.max(jnp.where(sel, (row >> 2).astype(F32), 0.0), axis=0, keepdims=True)
    tokens = x.shape[0]
    record = jnp.concatenate([gate, jnp.zeros((LANES - N_EXPERTS, tokens), F32)], axis=0)
    return record.T, group.astype(jnp.int32)


def _norm_route_store(z, g_ref, b_ref, wrh_ref, wrl_ref, br_ref, o_ref, grp_ref):
    x = _layer_norm(z, g_ref[...], b_ref[...])
    o_ref[:, :D_MODEL] = x
    o_ref[:, D_MODEL:], grp_ref[...] = _route(x, wrh_ref[...], wrl_ref[...], br_ref[...])


def _outproj_ln_kernel(h_ref, x_ref, w_ref, g_ref, b_ref, wrh_ref, wrl_ref, br_ref, o_ref, grp_ref):
    y = _bdot(h_ref[...].astype(BF16), w_ref[...])
    _norm_route_store(DN_ALPHA * x_ref[...] + y, g_ref, b_ref, wrh_ref, wrl_ref, br_ref, o_ref, grp_ref)


def _outproj_heads_ln_kernel(h_ref, x_ref, w_ref, g_ref, b_ref, wrh_ref, wrl_ref, br_ref, o_ref, grp_ref):
    h = jnp.concatenate([h_ref[0, i].astype(BF16) for i in range(h_ref.shape[1])], axis=1)
    _norm_route_store(DN_ALPHA * x_ref[...] + _bdot(h, w_ref[...]), g_ref, b_ref, wrh_ref, wrl_ref, br_ref,
                      o_ref, grp_ref)


def _outproj_call(kernel, name, h, h_spec, x2, w, g, b, router):
    n = x2.shape[0]
    tm = OUT_TM
    row = lambda i: (i, 0)
    full = lambda i: (0, 0)
    return pl.pallas_call(
        kernel,
        grid=(n // tm,),
        in_specs=[h_spec, pl.BlockSpec((tm, D_MODEL), row),
                  pl.BlockSpec(w.shape, full), pl.BlockSpec((1, D_MODEL), full), pl.BlockSpec((1, D_MODEL), full)]
                 + [pl.BlockSpec(r.shape, full) for r in router],
        out_specs=[pl.BlockSpec((tm, XA_WIDTH), row), pl.BlockSpec((1, tm), lambda i: (0, i))],
        out_shape=[jax.ShapeDtypeStruct((n, XA_WIDTH), F32), jax.ShapeDtypeStruct((1, n), jnp.int32)],
        compiler_params=_cparams(("arbitrary",)),
        name=name,
    )(h, x2, w, g, b, *router)


def _outproj_heads_ln(h4, x2, w, g, b, router):
    _, heads, seq, d = h4.shape
    per = seq // OUT_TM
    spec = pl.BlockSpec((1, heads, OUT_TM, d), lambda i: (i // per, 0, i % per, 0))
    return _outproj_call(_outproj_heads_ln_kernel, "outproj_heads_ln", h4, spec, x2, w, g, b, router)


def _outproj_ln(h2, x2, w, g, b, router):
    spec = pl.BlockSpec((OUT_TM, h2.shape[1]), lambda i: (i, 0))
    return _outproj_call(_outproj_ln_kernel, "outproj_ln", h2, spec, x2, w, g, b, router)


def _moe_sorted_kernel(dest_ref, glo_ref, ghi_ref, xa_hbm, wg_ref, wu_ref, wd_ref, lg_ref, lb_ref, out_hbm,
                       src_ref, xbuf, obuf, in_sem, out_sem):
    i = pl.program_id(0)
    last = pl.num_programs(0) - 1
    nt = 2 * pl.num_programs(0)
    ts = MOE_TS
    epg = EXPERTS_PER_GROUP

    def row_in(tile, sl, r):
        return pltpu.make_async_copy(xa_hbm.at[src_ref[tile * ts + r]], xbuf.at[sl, r], in_sem.at[sl])

    def row_out(tile, sl, r):
        return pltpu.make_async_copy(obuf.at[sl, r], out_hbm.at[src_ref[tile * ts + r]], out_sem.at[sl])

    def wait_in(sl):
        pltpu.make_async_copy(xa_hbm.at[pl.ds(0, ts)], xbuf.at[sl], in_sem.at[sl]).wait()

    def wait_out(sl):
        pltpu.make_async_copy(obuf.at[sl], out_hbm.at[pl.ds(0, ts)], out_sem.at[sl]).wait()

    @pl.when(i == 0)
    def _():
        def invert(n, carry):
            src_ref[dest_ref[n]] = n
            return carry

        lax.fori_loop(0, dest_ref.shape[0], invert, 0, unroll=8)
        for r in range(ts):
            row_in(0, 0, r).start()

    @pl.when(i >= 1)
    def _():
        wait_out(0)
        wait_out(1)

    for sl in range(2):
        tile = 2 * i + sl
        wait_in(sl)
        nxt = jnp.minimum(tile + 1, nt - 1)
        for r in range(ts):
            row_in(nxt, 1 - sl, r).start()

        xa = xbuf[sl]
        x = xa[:, :D_MODEL]
        gate = xa[:, D_MODEL:]
        xb = x.astype(BF16)

        def group_ffn(g, xb=xb, gate=gate):
            hs = []
            for e in range(epg):
                col = gate[:, e:e + 1]
                for other in range(1, N_EXPERT_GROUPS):
                    col = jnp.where(g == other, gate[:, other * epg + e:other * epg + e + 1], col)
                expert = g * epg + e
                hs.append((_silu(_bdot(xb, wg_ref[expert])) * _bdot(xb, wu_ref[expert]) * col).astype(BF16))
            return _bdot(jnp.concatenate(hs, axis=1), wd_ref[g])

        y = group_ffn(glo_ref[tile])
        y = lax.fori_loop(glo_ref[tile] + 1, ghi_ref[tile] + 1, lambda g, acc: acc + group_ffn(g), y)
        obuf[sl] = _layer_norm(DN_ALPHA * x + y, lg_ref[...], lb_ref[...])
        for r in range(ts):
            row_out(tile, sl, r).start()

    @pl.when(i == last)
    def _():
        wait_out(0)
        wait_out(1)
        wait_in(0)


def _moe_sorted(xa, dest, g_lo, g_hi, wg, wu, wd, lg, lb):
    n = xa.shape[0]
    ts = MOE_TS
    resident = pl.Buffered(1)
    whole = lambda i, *_: (0, 0, 0)
    grid_spec = pltpu.PrefetchScalarGridSpec(
        num_scalar_prefetch=3,
        grid=(n // (2 * ts),),
        in_specs=[pl.BlockSpec(memory_space=pl.ANY),
                  pl.BlockSpec(wg.shape, whole, pipeline_mode=resident),
                  pl.BlockSpec(wu.shape, whole, pipeline_mode=resident),
                  pl.BlockSpec(wd.shape, whole, pipeline_mode=resident),
                  pl.BlockSpec((1, D_MODEL), lambda i, *_: (0, 0)),
                  pl.BlockSpec((1, D_MODEL), lambda i, *_: (0, 0))],
        out_specs=pl.BlockSpec(memory_space=pl.ANY),
        scratch_shapes=[pltpu.SMEM((n,), jnp.int32),
                        pltpu.VMEM((2, ts, XA_WIDTH), F32), pltpu.VMEM((2, ts, D_MODEL), F32),
                        pltpu.SemaphoreType.DMA((2,)), pltpu.SemaphoreType.DMA((2,))])
    return pl.pallas_call(
        _moe_sorted_kernel,
        grid_spec=grid_spec,
        out_shape=jax.ShapeDtypeStruct((n, D_MODEL), F32),
        compiler_params=pltpu.CompilerParams(dimension_semantics=("arbitrary",), vmem_limit_bytes=MOE_VMEM_LIMIT),
        name="moe_sorted",
    )(dest, g_lo, g_hi, xa, wg, wu, wd, lg, lb)


def _cast_experts_kernel(wg_ref, wu_ref, wd_ref, og_ref, ou_ref, od_ref):
    og_ref[...] = wg_ref[0].astype(BF16)
    ou_ref[...] = wu_ref[0].astype(BF16)
    od_ref[...] = wd_ref[0].astype(BF16)


def _cast_experts(layer, w_gate, w_up, w_down):
    ws = (w_gate, w_up, w_down)
    return pl.pallas_call(
        _cast_experts_kernel,
        grid=(w_gate.shape[1],),
        in_specs=[pl.BlockSpec((1, 1) + w.shape[2:], lambda e: (layer, e, 0, 0)) for w in ws],
        out_specs=[pl.BlockSpec((1,) + w.shape[2:], lambda e: (e, 0, 0)) for w in ws],
        out_shape=[jax.ShapeDtypeStruct(w.shape[1:], BF16) for w in ws],
        compiler_params=_cparams(("arbitrary",)),
        name="cast_experts",
    )(*ws)


def _group_sort_plan(grp):
    n = grp.shape[1]
    onehot = (grp == jnp.arange(N_EXPERT_GROUPS, dtype=jnp.int32)[:, None]).astype(jnp.int32)
    csum = jnp.cumsum(onehot, axis=1)
    ends = jnp.cumsum(csum[:, -1])
    dest = jnp.sum(onehot * (csum - 1 + (ends - csum[:, -1])[:, None]), axis=0)
    first_row = jnp.arange(n // MOE_TS, dtype=jnp.int32) * MOE_TS
    group_of_row = lambda rows: jnp.sum(ends[None, :] <= rows[:, None], axis=1).astype(jnp.int32)
    return dest, group_of_row(first_row), group_of_row(first_row + (MOE_TS - 1))


def _nsa_proj_kernel(x_ref, wr_ref, wp_ref, wgt_ref, c_ref, s1_ref, s2_ref,
                     q_ref, kst_ref, kw_ref, kc_ref, vc_ref, vs_ref, vw_ref, gt_ref):
    xb = x_ref[...].astype(BF16)
    c, s1, s2 = c_ref[...], s1_ref[...], s2_ref[...]
    d = NSA_HEAD_DIM
    nq = NSA_HEADS * d // LANES
    ng = NSA_GROUPS * d // LANES
    first_head = lax.broadcasted_iota(jnp.int32, (x_ref.shape[0], LANES), 1) < d

    def put_rows(ref, j, t):
        ref[0, 2 * j] = t[:, :d].astype(ref.dtype)
        ref[0, 2 * j + 1] = t[:, d:].astype(ref.dtype)

    def put_with_ones(ref, j, t):
        ref[0, 2 * j] = jnp.where(first_head, t, 1.0).astype(ref.dtype)
        ref[0, 2 * j + 1] = jnp.where(first_head, pltpu.roll(t, d, axis=1), 1.0).astype(ref.dtype)

    def lane_blocks(w_ref, nblocks):
        per = MXU_COLS // LANES
        for j0 in range(0, nblocks, per):
            wide = _bdot(xb, w_ref[:, j0 * LANES:(j0 + per) * LANES])
            for k in range(per):
                yield j0 + k, wide[:, k * LANES:(k + 1) * LANES]

    for j, t in lane_blocks(wr_ref, nq + 2 * ng):
        t = _rope128(t, c, s1, s2)
        if j < nq:
            put_rows(q_ref, j, t * (d ** -0.5 * LOG2E))
        elif j < nq + ng:
            tt = t.T.astype(BF16)
            kst_ref[0, 2 * (j - nq), 0] = tt[:d]
            kst_ref[0, 2 * (j - nq) + 1, 0] = tt[d:]
        else:
            put_rows(kw_ref, j - nq - ng, t)
    plain = ((kc_ref, put_rows), (vc_ref, put_rows), (vs_ref, put_with_ones), (vw_ref, put_with_ones))
    for j, t in lane_blocks(wp_ref, len(plain) * ng):
        ref, put = plain[j // ng]
        put(ref, j % ng, t)
    gt_ref[...] = jax.nn.sigmoid(_bdot(xb, wgt_ref[...]))


def _nsa_proj(x2, w_rope, w_plain, w_gates, rc, rs1, rs2, seq):
    n = x2.shape[0]
    tm = PROJ_TM
    row = lambda i: (i, 0)
    full = lambda i: (0, 0)
    per = seq // tm
    pos = lambda i: (i % per, 0)
    bsz, g, d = n // seq, NSA_GROUPS, NSA_HEAD_DIM
    sub = ATT_TK // tm
    hmap = lambda i: (i // per, 0, i % per, 0)

    def rows_out(heads, width, dtype):
        return pl.BlockSpec((1, heads, tm, width), hmap), jax.ShapeDtypeStruct((bsz, heads, seq, width), dtype)

    outs = [rows_out(NSA_HEADS, d, BF16),
            (pl.BlockSpec((1, g, 1, d, tm), lambda i: (i // per, 0, (i % per) // sub, 0, (i % per) % sub)),
             jax.ShapeDtypeStruct((bsz, g, seq // ATT_TK, d, ATT_TK), BF16)),
            rows_out(g, d, BF16),
            rows_out(g, d, F32), rows_out(g, d, F32),
            rows_out(g, 2 * d, BF16), rows_out(g, 2 * d, BF16),
            (pl.BlockSpec((tm, g * LANES), row), jax.ShapeDtypeStruct((n, g * LANES), F32))]
    return pl.pallas_call(
        _nsa_proj_kernel,
        grid=(n // tm,),
        in_specs=[pl.BlockSpec((tm, D_MODEL), row),
                  pl.BlockSpec(w_rope.shape, full), pl.BlockSpec(w_plain.shape, full),
                  pl.BlockSpec(w_gates.shape, full),
                  pl.BlockSpec((tm, LANES), pos), pl.BlockSpec((tm, LANES), pos), pl.BlockSpec((tm, LANES), pos)],
        out_specs=[o[0] for o in outs],
        out_shape=[o[1] for o in outs],
        compiler_params=_cparams(("arbitrary",)),
        name="nsa_proj",
    )(x2, w_rope, w_plain, w_gates, rc, rs1, rs2)


def _compress_kernel(ck_ref, cv_ref, pet_ref, peb_ref, wk1_ref, wk2_ref, wv1_ref, wv2_ref,
                     c_ref, s1_ref, s2_ref, ko_ref, vo_ref):
    nch = ck_ref.shape[2] // CMP_STRIDE
    half = CMP_STRIDE * NSA_HEAD_DIM
    pet, peb = pet_ref[...], peb_ref[...]

    def chunk_rows(ref):
        return jnp.concatenate([ref[0, 0, pl.ds(t, nch, stride=CMP_STRIDE), :] for t in range(CMP_STRIDE)], axis=1)

    def mlp(ch, w1_ref, w2_ref):
        a = _bdot((ch + pet).astype(BF16), w1_ref[0:half, :])
        bm = _bdot((ch + peb).astype(BF16), w1_ref[half:2 * half, :])
        h = _silu(a + pltpu.roll(bm, nch - 1, axis=0))
        return _bdot(h.astype(BF16), w2_ref[...])

    kc = _rope128(mlp(chunk_rows(ck_ref), wk1_ref, wk2_ref), c_ref[...], s1_ref[...], s2_ref[...])
    ko_ref[0, 0] = kc[:, :NSA_HEAD_DIM].astype(BF16)
    vo_ref[0, 0] = mlp(chunk_rows(cv_ref), wv1_ref, wv2_ref)[:, :NSA_HEAD_DIM].astype(BF16)


def _compress(ck, cv, pet, peb, wk1, wk2, wv1, wv2, cc, cs1, cs2):
    bsz, g, seq, width = ck.shape
    nch = seq // CMP_STRIDE
    blk = lambda b, gi: (b, gi, 0, 0)
    full = lambda b, gi: (0, 0)
    return pl.pallas_call(
        _compress_kernel,
        grid=(bsz, g),
        in_specs=[pl.BlockSpec((1, 1, seq, width), blk), pl.BlockSpec((1, 1, seq, width), blk),
                  pl.BlockSpec(pet.shape, full), pl.BlockSpec(peb.shape, full),
                  pl.BlockSpec(wk1.shape, full), pl.BlockSpec(wk2.shape, full),
                  pl.BlockSpec(wv1.shape, full), pl.BlockSpec(wv2.shape, full),
                  pl.BlockSpec(cc.shape, full), pl.BlockSpec(cs1.shape, full), pl.BlockSpec(cs2.shape, full)],
        out_specs=[pl.BlockSpec((1, 1, nch, NSA_HEAD_DIM), blk), pl.BlockSpec((1, 1, nch, NSA_HEAD_DIM), blk)],
        out_shape=[jax.ShapeDtypeStruct((bsz, g, nch, NSA_HEAD_DIM), BF16)] * 2,
        compiler_params=_cparams(("arbitrary", "arbitrary")),
        name="nsa_compress",
    )(ck, cv, pet, peb, wk1, wk2, wv1, wv2, cc, cs1, cs2)


def _cmp_select_kernel(q_ref, kc_ref, vc_ref, wimp_ref, o_ref, sel_ref):
    tq = q_ref.shape[2]
    nch = kc_ref.shape[2]
    ns = wimp_ref.shape[0]
    s0 = pl.program_id(2) * tq
    ratio = SLC_BLOCK // CMP_STRIDE
    n_vis = (s0 + tq - CMP_BLOCK) // CMP_STRIDE + 1
    n_chunks = (n_vis + CMP_CW - 1) // CMP_CW

    def body(w):
        nb = w // ratio
        qpos_c = s0 + lax.broadcasted_iota(jnp.int32, (tq, 1), 0)
        cend = lax.broadcasted_iota(jnp.int32, (1, w), 1) * CMP_STRIDE + (CMP_BLOCK - 1)
        cmask = cend <= qpos_c
        anyvis = (qpos_c >= CMP_BLOCK - 1).astype(F32)
        blk = lax.broadcasted_iota(jnp.int32, (nb, tq), 0)
        cur = (s0 + lax.broadcasted_iota(jnp.int32, (nb, tq), 1)) >> SLC_SHIFT
        forced = (blk == 0) | (blk == cur) | (blk == cur - 1)
        visible = blk <= cur
        for gi in range(kc_ref.shape[1]):
            group_body(gi, w, nb, cmask, anyvis, forced, visible)

    def group_body(gi, w, nb, cmask, anyvis, forced, visible):
        d = q_ref.shape[3]
        hpg = q_ref.shape[1] // kc_ref.shape[1]
        heads = slice(gi * hpg, (gi + 1) * hpg)
        s = _dot_nt(q_ref[0, heads].reshape(hpg * tq, d), kc_ref[0, gi, :w, :]).reshape(hpg, tq, w)
        s = jnp.where(cmask[None], s, NEG)
        e = jnp.exp2(s - jnp.max(s, axis=-1, keepdims=True))
        p = e * (anyvis[None] / jnp.sum(e, axis=-1, keepdims=True))
        o_ref[0, heads] = _bdot(p.reshape(hpg * tq, w).astype(BF16), vc_ref[0, gi, :w, :]).reshape(hpg, tq, d)
        pg = jnp.sum(p, axis=0)

        pg_hi = pg.astype(BF16)
        pg_lo = (pg - pg_hi.astype(F32)).astype(BF16)
        wimp = wimp_ref[:nb, :w]
        imp = _dot_nt(wimp, pg_hi) + _dot_nt(wimp, pg_lo)
        imp = jnp.where(visible, jnp.where(forced, BIG, imp), NEG)
        parts = [slice(i, i + LANES) for i in range(0, tq, LANES)]
        blk_f = lax.broadcasted_iota(jnp.int32, (nb, LANES), 0).astype(F32)
        imps = [imp[:, pt] for pt in parts]
        sels = [jnp.zeros((nb, LANES), F32) for _ in parts]
        for _ in range(min(SLC_TOP_N, nb)):
            for i in range(len(parts)):
                m = jnp.max(imps[i], axis=0, keepdims=True)
                first = jnp.min(jnp.where(imps[i] == m, blk_f, float(nb)), axis=0, keepdims=True)
                hit = blk_f == first
                sels[i] = jnp.where(hit, 1.0, sels[i])
                imps[i] = jnp.where(hit, -jnp.inf, imps[i])
        sel = jnp.concatenate(sels, axis=1)
        if nb < ns:
            sel = jnp.concatenate([sel, jnp.zeros((ns - nb, tq), F32)], axis=0)
        sel_ref[0, gi] = jnp.where(sel.T > 0.0, 0.0, NEG).astype(BF16)

    for nv in range(1, nch // CMP_CW + 1):
        pl.when(n_chunks == nv)(functools.partial(body, nv * CMP_CW))


def _cmp_select(q, k_cmp, v_cmp, wimp_t):
    bsz, _, s, d = q.shape
    g = NSA_GROUPS
    ng = CMP_GROUPS
    nch = k_cmp.shape[2]
    ns = wimp_t.shape[0]
    tq = min(CMP_TQ, s)
    hmap = lambda b, gp, qi: (b, gp, qi, 0)
    return pl.pallas_call(
        _cmp_select_kernel,
        grid=(bsz, g // ng, s // tq),
        in_specs=[pl.BlockSpec((1, ng * NSA_HPG, tq, d), hmap),
                  pl.BlockSpec((1, ng, nch, d), lambda b, gp, qi: (b, gp, 0, 0)),
                  pl.BlockSpec((1, ng, nch, d), lambda b, gp, qi: (b, gp, 0, 0)),
                  pl.BlockSpec(wimp_t.shape, lambda b, gp, qi: (0, 0))],
        out_specs=[pl.BlockSpec((1, ng * NSA_HPG, tq, d), hmap),
                   pl.BlockSpec((1, ng, tq, ns), hmap)],
        out_shape=[jax.ShapeDtypeStruct((bsz, NSA_HEADS, s, d), F32),
                   jax.ShapeDtypeStruct((bsz, g, s, ns), BF16)],
        compiler_params=_cparams(("arbitrary", "arbitrary", "arbitrary")),
        name="nsa_cmp_select",
    )(q, k_cmp, v_cmp, wimp_t)


def _nsa_attn_kernel(q_ref, kst_ref, vs_ref, kw_ref, vw_ref, selb_ref, eexp_ref, ocmp_ref, gt_ref,
                     o_ref, m_ref, acc_ref, s_ref, p_ref, a_ref):
    ng = kst_ref.shape[1]
    hpg = q_ref.shape[1] // ng
    tq, d = q_ref.shape[2:]
    rows = hpg * tq
    s0 = pl.program_id(2) * tq
    qpos = s0 + lax.broadcasted_iota(jnp.int32, (tq, 1), 0)
    q4 = [q_ref[0, gi * hpg:(gi + 1) * hpg].reshape(rows, d) for gi in range(ng)]
    q4a = [jnp.concatenate([jnp.concatenate([selb_ref[0, gi]] * hpg, axis=0), q4[gi]], axis=1)
           for gi in range(ng)]

    def scores(gi, t):
        return _bdot(q4a[gi], jnp.concatenate([eexp_ref[t], kst_ref[0, gi, t]], axis=0))

    def flush(gi, t):
        acc_ref[gi] = a_ref[gi] * acc_ref[gi] + _bdot(p_ref[gi], vs_ref[0, gi, t])

    def softmax_step(gi, s):
        m_prev = m_ref[gi]
        m_new = jnp.maximum(m_prev, jnp.max(s, axis=-1, keepdims=True))
        a_ref[gi] = jnp.exp2(m_prev - m_new)
        p_ref[gi] = jnp.exp2(s - m_new).astype(BF16)
        m_ref[gi] = m_new

    m_ref[...] = jnp.full_like(m_ref, NEG)
    acc_ref[...] = jnp.zeros_like(acc_ref)
    p_ref[...] = jnp.zeros_like(p_ref)
    a_ref[...] = jnp.ones_like(a_ref)
    t_diag = s0 // ATT_TK
    for gi in range(ng):
        s_ref[gi] = scores(gi, 0)

    def step(t):
        for gi in range(ng):
            s = s_ref[gi]
            s_next = scores(gi, t + 1)
            flush(gi, jnp.maximum(t - 1, 0))
            softmax_step(gi, s)
            s_ref[gi] = s_next

    def pair_body(i, carry):
        step(2 * i)
        step(2 * i + 1)
        return carry

    lax.fori_loop(0, t_diag // 2, pair_body, 0)
    pl.when(t_diag % 2 == 1)(lambda: step(t_diag - 1))

    span = WINDOW + tq
    w0 = pl.multiple_of(jnp.maximum(s0 - WINDOW, 0), tq)
    rel = qpos - (w0 + lax.broadcasted_iota(jnp.int32, (1, span), 1))
    wbias = jnp.where((rel >= 0) & (rel < WINDOW), 0.0, NEG)
    kpos = t_diag * ATT_TK + lax.broadcasted_iota(jnp.int32, (1, ATT_TK), 1)
    causal = (kpos <= qpos)[None]
    for gi in range(ng):
        sw = _dot_nt(q4[gi], kw_ref[0, gi, pl.ds(w0, span), :])
        flush(gi, jnp.maximum(t_diag - 1, 0))
        s = s_ref[gi].reshape(hpg, tq, ATT_TK)
        softmax_step(gi, jnp.where(causal, s, NEG).reshape(rows, ATT_TK))
        sw = (sw.reshape(hpg, tq, span) + wbias[None]).reshape(rows, span)
        pw = jnp.exp2(sw - jnp.max(sw, axis=-1, keepdims=True)).astype(BF16)
        flush(gi, t_diag)
        accw = _bdot(pw, vw_ref[0, gi, pl.ds(w0, span), :])
        o_win = (accw[:, :d] / accw[:, d:d + 1]).reshape(hpg, tq, d)
        acc = acc_ref[gi]
        o_slc = (acc[:, :d] / acc[:, d:d + 1]).reshape(hpg, tq, d)
        gts = gt_ref[0, :, gi * LANES:(gi + 1) * LANES]
        for h in range(hpg):
            g0, g1, g2 = (gts[:, 3 * h + j:3 * h + j + 1] for j in range(3))
            o_ref[0, gi * hpg + h] = g0 * ocmp_ref[0, gi * hpg + h] + g1 * o_slc[h] + g2 * o_win[h]


def _nsa_attn(q, kst, vst, kw, vw, selb, eexp, o_cmp, gates):
    bsz, _, s, d = q.shape
    ng = ATT_GROUPS
    tq = ATT_TQ
    heads = ng * NSA_HPG
    rows = NSA_HPG * tq
    ns = selb.shape[-1]
    hmap = lambda b, gp, qi: (b, gp, qi, 0)
    once = pl.Buffered(1)
    return pl.pallas_call(
        _nsa_attn_kernel,
        grid=(bsz, NSA_GROUPS // ng, s // tq),
        in_specs=[pl.BlockSpec((1, heads, tq, d), hmap),
                  pl.BlockSpec((1, ng) + kst.shape[2:], lambda b, gp, qi: (b, gp, 0, 0, 0), pipeline_mode=once),
                  pl.BlockSpec((1, ng) + vst.shape[2:], lambda b, gp, qi: (b, gp, 0, 0, 0), pipeline_mode=once),
                  pl.BlockSpec((1, ng) + kw.shape[2:], lambda b, gp, qi: (b, gp, 0, 0), pipeline_mode=once),
                  pl.BlockSpec((1, ng) + vw.shape[2:], lambda b, gp, qi: (b, gp, 0, 0), pipeline_mode=once),
                  pl.BlockSpec((1, ng, tq, ns), hmap),
                  pl.BlockSpec(eexp.shape, lambda b, gp, qi: (0, 0, 0), pipeline_mode=once),
                  pl.BlockSpec((1, heads, tq, d), hmap),
                  pl.BlockSpec((1, tq, ng * LANES), lambda b, gp, qi: (b, qi, gp))],
        out_specs=pl.BlockSpec((1, heads, tq, d), hmap),
        out_shape=jax.ShapeDtypeStruct((bsz, NSA_HEADS, s, d), F32),
        scratch_shapes=[pltpu.VMEM((ng, rows, 1), F32),
                        pltpu.VMEM((ng, rows, 2 * d), F32),
                        pltpu.VMEM((ng, rows, ATT_TK), F32),
                        pltpu.VMEM((ng, rows, ATT_TK), BF16),
                        pltpu.VMEM((ng, rows, 1), F32)],
        compiler_params=_cparams(("arbitrary", "arbitrary", "arbitrary")),
        name="nsa_attn",
    )(q, kst, vst, kw, vw, selb, eexp, o_cmp, gates)


def _rope_tables(pos):
    inv_freq = jnp.power(ROPE_THETA, -jnp.arange(ROT_HALF, dtype=F32) * (2.0 / ROT_DIM))
    ang = pos.astype(F32)[:, None] * inv_freq[None, :]
    cos, sin = jnp.cos(ang), jnp.sin(ang)
    n = pos.shape[0]
    one = jnp.ones((n, NSA_HEAD_DIM - ROT_DIM), F32)
    zero8 = jnp.zeros((n, ROT_HALF), F32)
    zero = jnp.zeros((n, NSA_HEAD_DIM - ROT_DIM), F32)
    c = jnp.concatenate([cos, cos, one], axis=1)
    s1 = jnp.concatenate([-sin, zero8, zero], axis=1)
    s2 = jnp.concatenate([zero8, sin, zero], axis=1)
    rep = LANES // NSA_HEAD_DIM
    return tuple(jnp.tile(t, (1, rep)) for t in (c, s1, s2))


def _importance_weights(ns, nch):
    ratio = SLC_BLOCK // CMP_STRIDE
    span = CMP_BLOCK // CMP_STRIDE
    w = np.zeros((ns, nch), np.float32)
    for j in range(ns):
        for m in range(ratio):
            for n in range(span):
                c = ratio * j + m + n
                if c < nch - 1:
                    w[j, c] += 1.0
    return jnp.asarray(w, dtype=BF16)


def _block_expansion(nt, ns):
    key_blk = (np.arange(nt)[:, None] * ATT_TK + np.arange(ATT_TK)[None, :]) // SLC_BLOCK
    e = (np.arange(ns)[None, :, None] == key_blk[:, None, :]).astype(np.float32)
    return jnp.asarray(e, dtype=BF16)


def _gla_layer(x2, bsz, seq, w_in, w_gate_up, b_gate, norm_g, w_out, ln_g, ln_b, router):
    hk = GLA_HEADS * GLA_DK
    hv = GLA_HEADS * GLA_DV
    cuts = np.cumsum([hk, hk, hv, GLA_GATE_RANK]).tolist()
    wq, wk, wv, wg, wr = jnp.split(w_in, cuts, axis=1)
    w_main = jnp.concatenate([wq, wk, wv, wr], axis=1).astype(BF16)
    w_glow = jnp.pad(wg, ((0, 0), (0, LANES - GLA_GATE_RANK))).astype(BF16)
    w_gu = jnp.pad(w_gate_up, ((0, LANES - GLA_GATE_RANK), (0, 0))).astype(BF16)
    q, k, v, r, la = _gla_proj(x2, w_main, w_glow, w_gu, b_gate.reshape(1, hk))
    sh = lambda t: t.reshape(bsz, seq, t.shape[-1])
    o = _gla_core(sh(q), sh(k), sh(v), sh(la), sh(r), norm_g.reshape(1, GLA_DV))
    return _outproj_ln(o.reshape(bsz * seq, hv), x2, w_out.astype(BF16),
                       ln_g.reshape(1, D_MODEL), ln_b.reshape(1, D_MODEL), router)


def _nsa_layer(x2, bsz, seq, w_in, w_ck1, w_ck2, w_cv1, w_cv2, cmp_pe, w_out, ln_g, ln_b, router):
    h, g, hpg, d = NSA_HEADS, NSA_GROUPS, NSA_HPG, NSA_HEAD_DIM
    gd = g * d
    cuts = np.cumsum([h * d] + [gd] * 6).tolist()
    wq, wkc, wvc, wks, wvs, wkw, wvw, wgt = jnp.split(w_in, cuts, axis=1)
    w_rope = jnp.concatenate([wq, wks, wkw], axis=1).astype(BF16)
    w_plain = jnp.concatenate([wkc, wvc, wvs, wvw], axis=1).astype(BF16)
    w_gates = jnp.pad(wgt.reshape(D_MODEL, g, hpg * 3), ((0, 0), (0, 0), (0, LANES - hpg * 3)))
    w_gates = w_gates.reshape(D_MODEL, g * LANES).astype(BF16)
    rc, rs1, rs2 = _rope_tables(jnp.arange(seq, dtype=jnp.int32))
    q, kst, kw, kc, vc, vs1, vw1, gates = _nsa_proj(x2, w_rope, w_plain, w_gates, rc, rs1, rs2, seq)

    nch = seq // CMP_STRIDE
    pe = cmp_pe.reshape(1, CMP_BLOCK * d)
    half = CMP_STRIDE * d
    pad2 = lambda w: jnp.pad(w, ((0, 0), (0, LANES - d))).astype(BF16)
    cc, cs1, cs2 = _rope_tables(jnp.arange(nch, dtype=jnp.int32) * CMP_STRIDE + (CMP_BLOCK - 1))
    k_cmp, v_cmp = _compress(kc, vc, pe[:, :half], pe[:, half:],
                             w_ck1.astype(BF16), pad2(w_ck2), w_cv1.astype(BF16), pad2(w_cv2), cc, cs1, cs2)

    ns = seq // SLC_BLOCK
    sh = lambda t: t.reshape(bsz, seq, t.shape[-1])
    o_cmp, selb = _cmp_select(q, k_cmp, v_cmp, _importance_weights(ns, nch))

    nt = seq // ATT_TK
    o = _nsa_attn(q, kst, vs1.reshape(bsz, g, nt, ATT_TK, 2 * d), kw, vw1,
                  selb, _block_expansion(nt, ns), o_cmp, sh(gates))
    return _outproj_heads_ln(o, x2, w_out.astype(BF16), ln_g.reshape(1, D_MODEL), ln_b.reshape(1, D_MODEL), router)


def _moe_layer(xa, grp, layer, w_gate, w_up, w_down, ln_g, ln_b):
    wg, wu, wd = _cast_experts(layer, w_gate, w_up, w_down)
    wd = wd.reshape(N_EXPERT_GROUPS, EXPERTS_PER_GROUP * D_FF_EXPERT, D_MODEL)
    dest, g_lo, g_hi = _group_sort_plan(grp)
    return _moe_sorted(xa, dest, g_lo, g_hi, wg, wu, wd, ln_g.reshape(1, D_MODEL), ln_b.reshape(1, D_MODEL))


def kernel(x, gla_w_in, gla_w_gate_up, gla_b_gate, gla_norm_g, gla_w_out, nsa_w_in, nsa_w_cmp_k1, nsa_w_cmp_k2,
           nsa_w_cmp_v1, nsa_w_cmp_v2, nsa_cmp_pe, nsa_w_out, moe_w_router, moe_b_router, moe_w_gate, moe_w_up,
           moe_w_down, ln_g, ln_b):
    bsz, seq, _ = x.shape
    x2 = x.reshape(bsz * seq, D_MODEL)
    w_router_t = moe_w_router.T
    wr_hi = w_router_t.astype(BF16)
    wr_lo = (w_router_t - wr_hi.astype(F32)).astype(BF16)
    router = (wr_hi, wr_lo, moe_b_router.reshape(N_EXPERTS, 1))
    for i in range(DEPTH):
        j = i // 2
        if i % 2 == 0:
            xa, grp = _gla_layer(x2, bsz, seq, gla_w_in[j], gla_w_gate_up[j], gla_b_gate[j], gla_norm_g[j],
                                 gla_w_out[j], ln_g[i, 0], ln_b[i, 0], router)
        else:
            xa, grp = _nsa_layer(x2, bsz, seq, nsa_w_in[j], nsa_w_cmp_k1[j], nsa_w_cmp_k2[j], nsa_w_cmp_v1[j],
                                 nsa_w_cmp_v2[j], nsa_cmp_pe[j], nsa_w_out[j], ln_g[i, 0], ln_b[i, 0], router)
        x2 = _moe_layer(xa, grp, i, moe_w_gate, moe_w_up, moe_w_down, ln_g[i, 1], ln_b[i, 1])
    return x2.reshape(bsz, seq, D_MODEL)
```

```python
import functools

import numpy as np
import jax
import jax.numpy as jnp
from jax import lax
from jax.experimental import pallas as pl
from jax.experimental.pallas import tpu as pltpu

F32 = jnp.float32
BF16 = jnp.bfloat16

D_MODEL = 1024
DEPTH = 2

GLA_HEADS = 4
GLA_DK = D_MODEL // 2 // GLA_HEADS
GLA_DV = D_MODEL // GLA_HEADS
GLA_GATE_RANK = 16
GLA_TAU = 16.0
GLA_CHUNK = 64

NSA_HEADS = 16
NSA_GROUPS = 4
NSA_HPG = NSA_HEADS // NSA_GROUPS
NSA_HEAD_DIM = D_MODEL // NSA_HEADS
CMP_BLOCK = 32
CMP_STRIDE = 16
SLC_BLOCK = 64
SLC_SHIFT = SLC_BLOCK.bit_length() - 1
SLC_TOP_N = 16
WINDOW = 512

ROPE_THETA = 500000.0
ROT_DIM = NSA_HEAD_DIM // 4
ROT_HALF = ROT_DIM // 2

N_EXPERTS = 16
N_EXPERT_GROUPS = 4
EXPERTS_PER_GROUP = N_EXPERTS // N_EXPERT_GROUPS
D_FF_EXPERT = D_MODEL // 4

DN_ALPHA = (2 * DEPTH) ** 0.25
LN_EPS = 1e-5
NEG = -1e30
BIG = 1e30
LOG2E = 1.4426950408889634

LANES = 128
MXU_COLS = 256
VMEM_LIMIT = 48 * 1024 * 1024

PROJ_TM = 256
OUT_TM = 512
GLA_T = 512
GLA_HEADS_PER_STEP = 4
MOE_TS = 512
MOE_VMEM_LIMIT = 56 * 1024 * 1024
XA_WIDTH = D_MODEL + LANES
CMP_TQ = 512
CMP_CW = 128
CMP_GROUPS = 2
ATT_TQ = 128
ATT_TK = 512
ATT_GROUPS = 2


def _cparams(sem):
    return pltpu.CompilerParams(dimension_semantics=sem, vmem_limit_bytes=VMEM_LIMIT)


def _bdot(a, b):
    return jnp.dot(a, b, preferred_element_type=F32)


def _dot_nt(a, b):
    return lax.dot_general(a, b, (((1,), (1,)), ((), ())), preferred_element_type=F32)


def _dot_tn(a, b):
    return lax.dot_general(a, b, (((0,), (0,)), ((), ())), preferred_element_type=F32)


def _layer_norm(z, g, b):
    mu = jnp.mean(z, axis=-1, keepdims=True)
    zc = z - mu
    var = jnp.mean(zc * zc, axis=-1, keepdims=True)
    return zc * lax.rsqrt(var + LN_EPS) * g + b


def _silu(t):
    return t * (0.5 * jnp.tanh(0.5 * t) + 0.5)


def _rope128(t, c, s1, s2):
    up = pltpu.roll(t, LANES - ROT_HALF, axis=1)
    dn = pltpu.roll(t, ROT_HALF, axis=1)
    return t * c + up * s1 + dn * s2


def _gla_proj_kernel(x_ref, wm_ref, wg_ref, wgu_ref, bg_ref, q_ref, k_ref, v_ref, r_ref, la_ref):
    xb = x_ref[...].astype(BF16)
    hk = GLA_HEADS * GLA_DK
    hv = GLA_HEADS * GLA_DV
    q_ref[...] = _bdot(xb, wm_ref[:, 0:hk])
    k_ref[...] = _bdot(xb, wm_ref[:, hk:2 * hk])
    v_ref[...] = _bdot(xb, wm_ref[:, 2 * hk:2 * hk + hv])
    r_ref[...] = _bdot(xb, wm_ref[:, 2 * hk + hv:2 * hk + 2 * hv])
    g_low = _bdot(xb, wg_ref[...])
    z = _bdot(g_low.astype(BF16), wgu_ref[...]) + bg_ref[...]
    log_sig = jnp.minimum(z, 0.0) - jnp.log1p(jnp.exp(-jnp.abs(z)))
    la_ref[...] = log_sig * (1.0 / GLA_TAU)


def _gla_proj(x2, w_main, w_glow, w_gu, b_gate):
    n = x2.shape[0]
    hk = GLA_HEADS * GLA_DK
    hv = GLA_HEADS * GLA_DV
    tm = PROJ_TM
    row = lambda i: (i, 0)
    full = lambda i: (0, 0)
    return pl.pallas_call(
        _gla_proj_kernel,
        grid=(n // tm,),
        in_specs=[pl.BlockSpec((tm, D_MODEL), row),
                  pl.BlockSpec(w_main.shape, full),
                  pl.BlockSpec(w_glow.shape, full),
                  pl.BlockSpec(w_gu.shape, full),
                  pl.BlockSpec(b_gate.shape, full)],
        out_specs=[pl.BlockSpec((tm, hk), row), pl.BlockSpec((tm, hk), row),
                   pl.BlockSpec((tm, hv), row), pl.BlockSpec((tm, hv), row),
                   pl.BlockSpec((tm, hk), row)],
        out_shape=[jax.ShapeDtypeStruct((n, hk), F32), jax.ShapeDtypeStruct((n, hk), F32),
                   jax.ShapeDtypeStruct((n, hv), F32), jax.ShapeDtypeStruct((n, hv), F32),
                   jax.ShapeDtypeStruct((n, hk), F32)],
        compiler_params=_cparams(("arbitrary",)),
        name="gla_proj",
    )(x2, w_main, w_glow, w_gu, b_gate)


def _gla_core_kernel(q_ref, k_ref, v_ref, la_ref, r_ref, g_ref, o_ref, st_ref):
    @pl.when(pl.program_id(2) == 0)
    def _():
        st_ref[...] = jnp.zeros_like(st_ref)

    c, t = GLA_CHUNK, GLA_T
    shift = c.bit_length() - 1
    row = lax.broadcasted_iota(jnp.int32, (t, t), 0)
    col = lax.broadcasted_iota(jnp.int32, (t, t), 1)
    causal = ((row >> shift) == (col >> shift)) & (row >= col)
    in_chunk = lax.broadcasted_iota(jnp.int32, (t, GLA_DK), 0) & (c - 1)
    chunks = [slice(ci * c, (ci + 1) * c) for ci in range(t // c)]
    for hh in range(st_ref.shape[0]):
        kcols = slice(hh * GLA_DK, (hh + 1) * GLA_DK)
        vcols = slice(hh * GLA_DV, (hh + 1) * GLA_DV)
        b = la_ref[0, :, kcols]
        step = 1
        while step < c:
            b = b + jnp.where(in_chunk >= step, pltpu.roll(b, step, axis=0), 0.0)
            step *= 2
        b_last = jnp.broadcast_to(b.reshape(t // c, c, GLA_DK)[:, c - 1:c, :],
                                  (t // c, c, GLA_DK)).reshape(t, GLA_DK)
        q = q_ref[0, :, kcols] * (GLA_DK ** -0.5)
        k = k_ref[0, :, kcols]
        vb = v_ref[0, :, vcols].astype(BF16)
        q_dec = (q * jnp.exp(b)).astype(BF16)
        k_neg = (k * jnp.exp(-b)).astype(BF16)
        k_dec = (k * jnp.exp(b_last - b)).astype(BF16)
        decay = jnp.exp(b_last)
        s = jnp.where(causal, _dot_nt(q_dec, k_neg), 0.0)
        o = _bdot(s.astype(BF16), vb)

        kv = [_dot_tn(vb[rows], k_dec[rows]) for rows in chunks]
        st = st_ref[hh]
        entering = []
        for ci, rows in enumerate(chunks):
            entering.append(st.astype(BF16))
            st = decay[ci * c:ci * c + 1] * st + kv[ci]
        st_ref[hh] = st
        o = o + jnp.concatenate([_dot_nt(q_dec[rows], s_in) for rows, s_in in zip(chunks, entering)], axis=0)
        ms = jnp.mean(o * o, axis=-1, keepdims=True)
        o = o * lax.rsqrt(ms + LN_EPS) * g_ref[...]
        o_ref[0, :, vcols] = o * _silu(r_ref[0, :, vcols])


def _gla_core(q, k, v, la, r, norm_g):
    bsz, s, _ = q.shape
    t = GLA_T
    hs = GLA_HEADS_PER_STEP
    kmap = lambda b, h, n: (b, n, h)
    kspec = pl.BlockSpec((1, t, hs * GLA_DK), kmap)
    vspec = pl.BlockSpec((1, t, hs * GLA_DV), kmap)
    return pl.pallas_call(
        _gla_core_kernel,
        grid=(bsz, GLA_HEADS // hs, s // t),
        in_specs=[kspec, kspec, vspec, kspec, vspec,
                  pl.BlockSpec((1, GLA_DV), lambda b, h, n: (0, 0))],
        out_specs=vspec,
        out_shape=jax.ShapeDtypeStruct((bsz, s, GLA_HEADS * GLA_DV), F32),
        scratch_shapes=[pltpu.VMEM((hs, GLA_DV, GLA_DK), F32)],
        compiler_params=_cparams(("arbitrary", "arbitrary", "arbitrary")),
        name="gla_core",
    )(q, k, v, la, r, norm_g)


def _row_xor(v, k, row):
    n = v.shape[0]
    up = pltpu.roll(v, n - k, axis=0)
    dn = pltpu.roll(v, k, axis=0)
    return jnp.where((row & k) == 0, up, dn)


def _route(x, wt_hi, wt_lo, bias):
    x_hi = x.astype(BF16)
    x_lo = (x - x_hi.astype(F32)).astype(BF16)
    logits = _dot_nt(wt_hi, x_hi) + (_dot_nt(wt_lo, x_hi) + _dot_nt(wt_hi, x_lo))
    s = jax.nn.sigmoid(logits)
    a = s + bias
    row = lax.broadcasted_iota(jnp.int32, a.shape, 0)
    a1 = _row_xor(a, 1, row)
    a2 = _row_xor(a, 2, row)
    a3 = _row_xor(a1, 2, row)
    p, q = jnp.maximum(a, a1), jnp.minimum(a, a1)
    r, t = jnp.maximum(a2, a3), jnp.minimum(a2, a3)
    gs = jnp.maximum(p, r) + jnp.maximum(jnp.minimum(p, r), jnp.maximum(q, t))
    g1 = _row_xor(gs, 4, row)
    g2 = _row_xor(gs, 8, row)
    g3 = _row_xor(g1, 8, row)

    def beats(other, other_first):
        return (other > gs) | ((other == gs) & other_first)

    lose = (beats(g1, (row & 4) != 0) | beats(g2, (row & 8) != 0) | beats(g3, (row & 8) != 0))

    def ahead(other, other_first):
        return jnp.where((other > a) | ((other == a) & other_first), 1.0, 0.0)

    rank = ahead(a1, (row & 1) != 0) + ahead(a2, (row & 2) != 0) + ahead(a3, (row & 2) != 0)
    sel = jnp.logical_not(lose) & (rank < 2.0)
    ssel = jnp.where(sel, s, 0.0)
    s1 = _row_xor(ssel, 1, row)
    tot = (ssel + s1) + (_row_xor(ssel, 2, row) + _row_xor(s1, 2, row))
    gate = jnp.where(sel, s / tot, 0.0)
    group = jnp.max(jnp.where(sel, (row >> 2).astype(F32), 0.0), axis=0, keepdims=True)
    tokens = x.shape[0]
    record = jnp.concatenate([gate, jnp.zeros((LANES - N_EXPERTS, tokens), F32)], axis=0)
    return record.T, group.astype(jnp.int32)


def _norm_route_store(z, g_ref, b_ref, wrh_ref, wrl_ref, br_ref, o_ref, grp_ref):
    x = _layer_norm(z, g_ref[...], b_ref[...])
    o_ref[:, :D_MODEL] = x
    o_ref[:, D_MODEL:], grp_ref[...] = _route(x, wrh_ref[...], wrl_ref[...], br_ref[...])


def _outproj_ln_kernel(h_ref, x_ref, w_ref, g_ref, b_ref, wrh_ref, wrl_ref, br_ref, o_ref, grp_ref):
    y = _bdot(h_ref[...].astype(BF16), w_ref[...])
    _norm_route_store(DN_ALPHA * x_ref[...] + y, g_ref, b_ref, wrh_ref, wrl_ref, br_ref, o_ref, grp_ref)


def _outproj_heads_ln_kernel(h_ref, x_ref, w_ref, g_ref, b_ref, wrh_ref, wrl_ref, br_ref, o_ref, grp_ref):
    h = jnp.concatenate([h_ref[0, i].astype(BF16) for i in range(h_ref.shape[1])], axis=1)
    _norm_route_store(DN_ALPHA * x_ref[...] + _bdot(h, w_ref[...]), g_ref, b_ref, wrh_ref, wrl_ref, br_ref,
                      o_ref, grp_ref)


def _outproj_call(kernel, name, h, h_spec, x2, w, g, b, router):
    n = x2.shape[0]
    tm = OUT_TM
    row = lambda i: (i, 0)
    full = lambda i: (0, 0)
    return pl.pallas_call(
        kernel,
        grid=(n // tm,),
        in_specs=[h_spec, pl.BlockSpec((tm, D_MODEL), row),
                  pl.BlockSpec(w.shape, full), pl.BlockSpec((1, D_MODEL), full), pl.BlockSpec((1, D_MODEL), full)]
                 + [pl.BlockSpec(r.shape, full) for r in router],
        out_specs=[pl.BlockSpec((tm, XA_WIDTH), row), pl.BlockSpec((1, tm), lambda i: (0, i))],
        out_shape=[jax.ShapeDtypeStruct((n, XA_WIDTH), F32), jax.ShapeDtypeStruct((1, n), jnp.int32)],
        compiler_params=_cparams(("arbitrary",)),
        name=name,
    )(h, x2, w, g, b, *router)


def _outproj_heads_ln(h4, x2, w, g, b, router):
    _, heads, seq, d = h4.shape
    per = seq // OUT_TM
    spec = pl.BlockSpec((1, heads, OUT_TM, d), lambda i: (i // per, 0, i % per, 0))
    return _outproj_call(_outproj_heads_ln_kernel, "outproj_heads_ln", h4, spec, x2, w, g, b, router)


def _outproj_ln(h2, x2, w, g, b, router):
    spec = pl.BlockSpec((OUT_TM, h2.shape[1]), lambda i: (i, 0))
    return _outproj_call(_outproj_ln_kernel, "outproj_ln", h2, spec, x2, w, g, b, router)


def _moe_sorted_kernel(dest_ref, glo_ref, ghi_ref, xa_hbm, wg_ref, wu_ref, wd_ref, lg_ref, lb_ref, out_hbm,
                       src_ref, xbuf, obuf, in_sem, out_sem):
    i = pl.program_id(0)
    last = pl.num_programs(0) - 1
    nt = 2 * pl.num_programs(0)
    ts = MOE_TS
    epg = EXPERTS_PER_GROUP

    def row_in(tile, sl, r):
        return pltpu.make_async_copy(xa_hbm.at[src_ref[tile * ts + r]], xbuf.at[sl, r], in_sem.at[sl])

    def row_out(tile, sl, r):
        return pltpu.make_async_copy(obuf.at[sl, r], out_hbm.at[src_ref[tile * ts + r]], out_sem.at[sl])

    def wait_in(sl):
        pltpu.make_async_copy(xa_hbm.at[pl.ds(0, ts)], xbuf.at[sl], in_sem.at[sl]).wait()

    def wait_out(sl):
        pltpu.make_async_copy(obuf.at[sl], out_hbm.at[pl.ds(0, ts)], out_sem.at[sl]).wait()

    @pl.when(i == 0)
    def _():
        def invert(n, carry):
            src_ref[dest_ref[n]] = n
            return carry

        lax.fori_loop(0, dest_ref.shape[0], invert, 0, unroll=8)
        for r in range(ts):
            row_in(0, 0, r).start()

    @pl.when(i >= 1)
    def _():
        wait_out(0)
        wait_out(1)

    for sl in range(2):
        tile = 2 * i + sl
        wait_in(sl)
        nxt = jnp.minimum(tile + 1, nt - 1)
        for r in range(ts):
            row_in(nxt, 1 - sl, r).start(priority=r % 2)

        xa = xbuf[sl]
        x = xa[:, :D_MODEL]
        gate = xa[:, D_MODEL:]
        xb = x.astype(BF16)

        def group_ffn(g, xb=xb, gate=gate):
            hs = []
            for e in range(epg):
                col = gate[:, e:e + 1]
                for other in range(1, N_EXPERT_GROUPS):
                    col = jnp.where(g == other, gate[:, other * epg + e:other * epg + e + 1], col)
                expert = g * epg + e
                hs.append((_silu(_bdot(xb, wg_ref[expert])) * _bdot(xb, wu_ref[expert]) * col).astype(BF16))
            return _bdot(jnp.concatenate(hs, axis=1), wd_ref[g])

        y = group_ffn(glo_ref[tile])
        y = lax.fori_loop(glo_ref[tile] + 1, ghi_ref[tile] + 1, lambda g, acc: acc + group_ffn(g), y)
        obuf[sl] = _layer_norm(DN_ALPHA * x + y, lg_ref[...], lb_ref[...])
        for r in range(ts):
            row_out(tile, sl, r).start(priority=r % 2)

    @pl.when(i == last)
    def _():
        wait_out(0)
        wait_out(1)
        wait_in(0)


def _moe_sorted(xa, dest, g_lo, g_hi, wg, wu, wd, lg, lb):
    n = xa.shape[0]
    ts = MOE_TS
    resident = pl.Buffered(1)
    whole = lambda i, *_: (0, 0, 0)
    grid_spec = pltpu.PrefetchScalarGridSpec(
        num_scalar_prefetch=3,
        grid=(n // (2 * ts),),
        in_specs=[pl.BlockSpec(memory_space=pl.ANY),
                  pl.BlockSpec(wg.shape, whole, pipeline_mode=resident),
                  pl.BlockSpec(wu.shape, whole, pipeline_mode=resident),
                  pl.BlockSpec(wd.shape, whole, pipeline_mode=resident),
                  pl.BlockSpec((1, D_MODEL), lambda i, *_: (0, 0)),
                  pl.BlockSpec((1, D_MODEL), lambda i, *_: (0, 0))],
        out_specs=pl.BlockSpec(memory_space=pl.ANY),
        scratch_shapes=[pltpu.SMEM((n,), jnp.int32),
                        pltpu.VMEM((2, ts, XA_WIDTH), F32), pltpu.VMEM((2, ts, D_MODEL), F32),
                        pltpu.SemaphoreType.DMA((2,)), pltpu.SemaphoreType.DMA((2,))])
    return pl.pallas_call(
        _moe_sorted_kernel,
        grid_spec=grid_spec,
        out_shape=jax.ShapeDtypeStruct((n, D_MODEL), F32),
        compiler_params=pltpu.CompilerParams(dimension_semantics=("arbitrary",), vmem_limit_bytes=MOE_VMEM_LIMIT),
        name="moe_sorted",
    )(dest, g_lo, g_hi, xa, wg, wu, wd, lg, lb)


def _cast_experts_kernel(wg_ref, wu_ref, wd_ref, og_ref, ou_ref, od_ref):
    og_ref[...] = wg_ref[0].astype(BF16)
    ou_ref[...] = wu_ref[0].astype(BF16)
    od_ref[...] = wd_ref[0].astype(BF16)


def _cast_experts(layer, w_gate, w_up, w_down):
    ws = (w_gate, w_up, w_down)
    return pl.pallas_call(
        _cast_experts_kernel,
        grid=(w_gate.shape[1],),
        in_specs=[pl.BlockSpec((1, 1) + w.shape[2:], lambda e: (layer, e, 0, 0)) for w in ws],
        out_specs=[pl.BlockSpec((1,) + w.shape[2:], lambda e: (e, 0, 0)) for w in ws],
        out_shape=[jax.ShapeDtypeStruct(w.shape[1:], BF16) for w in ws],
        compiler_params=_cparams(("arbitrary",)),
        name="cast_experts",
    )(*ws)


def _group_sort_plan(grp):
    n = grp.shape[1]
    onehot = (grp == jnp.arange(N_EXPERT_GROUPS, dtype=jnp.int32)[:, None]).astype(jnp.int32)
    csum = jnp.cumsum(onehot, axis=1)
    ends = jnp.cumsum(csum[:, -1])
    dest = jnp.sum(onehot * (csum - 1 + (ends - csum[:, -1])[:, None]), axis=0)
    first_row = jnp.arange(n // MOE_TS, dtype=jnp.int32) * MOE_TS
    group_of_row = lambda rows: jnp.sum(ends[None, :] <= rows[:, None], axis=1).astype(jnp.int32)
    return dest, group_of_row(first_row), group_of_row(first_row + (MOE_TS - 1))


def _nsa_proj_kernel(x_ref, wr_ref, wp_ref, wgt_ref, c_ref, s1_ref, s2_ref,
                     q_ref, kst_ref, kw_ref, kc_ref, vc_ref, vs_ref, vw_ref, gt_ref):
    xb = x_ref[...].astype(BF16)
    c, s1, s2 = c_ref[...], s1_ref[...], s2_ref[...]
    d = NSA_HEAD_DIM
    nq = NSA_HEADS * d // LANES
    ng = NSA_GROUPS * d // LANES
    first_head = lax.broadcasted_iota(jnp.int32, (x_ref.shape[0], LANES), 1) < d

    def put_rows(ref, j, t):
        ref[0, 2 * j] = t[:, :d].astype(ref.dtype)
        ref[0, 2 * j + 1] = t[:, d:].astype(ref.dtype)

    def put_with_ones(ref, j, t):
        ref[0, 2 * j] = jnp.where(first_head, t, 1.0).astype(ref.dtype)
        ref[0, 2 * j + 1] = jnp.where(first_head, pltpu.roll(t, d, axis=1), 1.0).astype(ref.dtype)

    def lane_blocks(w_ref, nblocks):
        per = MXU_COLS // LANES
        for j0 in range(0, nblocks, per):
            wide = _bdot(xb, w_ref[:, j0 * LANES:(j0 + per) * LANES])
            for k in range(per):
                yield j0 + k, wide[:, k * LANES:(k + 1) * LANES]

    for j, t in lane_blocks(wr_ref, nq + 2 * ng):
        t = _rope128(t, c, s1, s2)
        if j < nq:
            put_rows(q_ref, j, t * (d ** -0.5 * LOG2E))
        elif j < nq + ng:
            tt = t.T.astype(BF16)
            kst_ref[0, 2 * (j - nq), 0] = tt[:d]
            kst_ref[0, 2 * (j - nq) + 1, 0] = tt[d:]
        else:
            put_rows(kw_ref, j - nq - ng, t)
    plain = ((kc_ref, put_rows), (vc_ref, put_rows), (vs_ref, put_with_ones), (vw_ref, put_with_ones))
    for j, t in lane_blocks(wp_ref, len(plain) * ng):
        ref, put = plain[j // ng]
        put(ref, j % ng, t)
    gt_ref[...] = jax.nn.sigmoid(_bdot(xb, wgt_ref[...]))


def _nsa_proj(x2, w_rope, w_plain, w_gates, rc, rs1, rs2, seq):
    n = x2.shape[0]
    tm = PROJ_TM
    row = lambda i: (i, 0)
    full = lambda i: (0, 0)
    per = seq // tm
    pos = lambda i: (i % per, 0)
    bsz, g, d = n // seq, NSA_GROUPS, NSA_HEAD_DIM
    sub = ATT_TK // tm
    hmap = lambda i: (i // per, 0, i % per, 0)

    def rows_out(heads, width, dtype):
        return pl.BlockSpec((1, heads, tm, width), hmap), jax.ShapeDtypeStruct((bsz, heads, seq, width), dtype)

    outs = [rows_out(NSA_HEADS, d, BF16),
            (pl.BlockSpec((1, g, 1, d, tm), lambda i: (i // per, 0, (i % per) // sub, 0, (i % per) % sub)),
             jax.ShapeDtypeStruct((bsz, g, seq // ATT_TK, d, ATT_TK), BF16)),
            rows_out(g, d, BF16),
            rows_out(g, d, F32), rows_out(g, d, F32),
            rows_out(g, 2 * d, BF16), rows_out(g, 2 * d, BF16),
            (pl.BlockSpec((tm, g * LANES), row), jax.ShapeDtypeStruct((n, g * LANES), F32))]
    return pl.pallas_call(
        _nsa_proj_kernel,
        grid=(n // tm,),
        in_specs=[pl.BlockSpec((tm, D_MODEL), row),
                  pl.BlockSpec(w_rope.shape, full), pl.BlockSpec(w_plain.shape, full),
                  pl.BlockSpec(w_gates.shape, full),
                  pl.BlockSpec((tm, LANES), pos), pl.BlockSpec((tm, LANES), pos), pl.BlockSpec((tm, LANES), pos)],
        out_specs=[o[0] for o in outs],
        out_shape=[o[1] for o in outs],
        compiler_params=_cparams(("arbitrary",)),
        name="nsa_proj",
    )(x2, w_rope, w_plain, w_gates, rc, rs1, rs2)


def _compress_kernel(ck_ref, cv_ref, pet_ref, peb_ref, wk1_ref, wk2_ref, wv1_ref, wv2_ref,
                     c_ref, s1_ref, s2_ref, ko_ref, vo_ref):
    nch = ck_ref.shape[2] // CMP_STRIDE
    half = CMP_STRIDE * NSA_HEAD_DIM
    pet, peb = pet_ref[...], peb_ref[...]

    def chunk_rows(ref):
        return jnp.concatenate([ref[0, 0, pl.ds(t, nch, stride=CMP_STRIDE), :] for t in range(CMP_STRIDE)], axis=1)

    def mlp(ch, w1_ref, w2_ref):
        a = _bdot((ch + pet).astype(BF16), w1_ref[0:half, :])
        bm = _bdot((ch + peb).astype(BF16), w1_ref[half:2 * half, :])
        h = _silu(a + pltpu.roll(bm, nch - 1, axis=0))
        return _bdot(h.astype(BF16), w2_ref[...])

    kc = _rope128(mlp(chunk_rows(ck_ref), wk1_ref, wk2_ref), c_ref[...], s1_ref[...], s2_ref[...])
    ko_ref[0, 0] = kc[:, :NSA_HEAD_DIM].astype(BF16)
    vo_ref[0, 0] = mlp(chunk_rows(cv_ref), wv1_ref, wv2_ref)[:, :NSA_HEAD_DIM].astype(BF16)


def _compress(ck, cv, pet, peb, wk1, wk2, wv1, wv2, cc, cs1, cs2):
    bsz, g, seq, width = ck.shape
    nch = seq // CMP_STRIDE
    blk = lambda b, gi: (b, gi, 0, 0)
    full = lambda b, gi: (0, 0)
    return pl.pallas_call(
        _compress_kernel,
        grid=(bsz, g),
        in_specs=[pl.BlockSpec((1, 1, seq, width), blk), pl.BlockSpec((1, 1, seq, width), blk),
                  pl.BlockSpec(pet.shape, full), pl.BlockSpec(peb.shape, full),
                  pl.BlockSpec(wk1.shape, full), pl.BlockSpec(wk2.shape, full),
                  pl.BlockSpec(wv1.shape, full), pl.BlockSpec(wv2.shape, full),
                  pl.BlockSpec(cc.shape, full), pl.BlockSpec(cs1.shape, full), pl.BlockSpec(cs2.shape, full)],
        out_specs=[pl.BlockSpec((1, 1, nch, NSA_HEAD_DIM), blk), pl.BlockSpec((1, 1, nch, NSA_HEAD_DIM), blk)],
        out_shape=[jax.ShapeDtypeStruct((bsz, g, nch, NSA_HEAD_DIM), BF16)] * 2,
        compiler_params=_cparams(("arbitrary", "arbitrary")),
        name="nsa_compress",
    )(ck, cv, pet, peb, wk1, wk2, wv1, wv2, cc, cs1, cs2)


def _cmp_select_kernel(q_ref, kc_ref, vc_ref, wimp_ref, o_ref, sel_ref):
    tq = q_ref.shape[2]
    nch = kc_ref.shape[2]
    ns = wimp_ref.shape[0]
    s0 = pl.program_id(2) * tq
    ratio = SLC_BLOCK // CMP_STRIDE
    n_vis = (s0 + tq - CMP_BLOCK) // CMP_STRIDE + 1
    n_chunks = (n_vis + CMP_CW - 1) // CMP_CW

    def body(w):
        nb = w // ratio
        qpos_c = s0 + lax.broadcasted_iota(jnp.int32, (tq, 1), 0)
        cend = lax.broadcasted_iota(jnp.int32, (1, w), 1) * CMP_STRIDE + (CMP_BLOCK - 1)
        cmask = cend <= qpos_c
        anyvis = (qpos_c >= CMP_BLOCK - 1).astype(F32)
        blk = lax.broadcasted_iota(jnp.int32, (nb, tq), 0)
        cur = (s0 + lax.broadcasted_iota(jnp.int32, (nb, tq), 1)) >> SLC_SHIFT
        forced = (blk == 0) | (blk == cur) | (blk == cur - 1)
        visible = blk <= cur
        for gi in range(kc_ref.shape[1]):
            group_body(gi, w, nb, cmask, anyvis, forced, visible)

    def group_body(gi, w, nb, cmask, anyvis, forced, visible):
        d = q_ref.shape[3]
        hpg = q_ref.shape[1] // kc_ref.shape[1]
        heads = slice(gi * hpg, (gi + 1) * hpg)
        s = _dot_nt(q_ref[0, heads].reshape(hpg * tq, d), kc_ref[0, gi, :w, :]).reshape(hpg, tq, w)
        s = jnp.where(cmask[None], s, NEG)
        e = jnp.exp2(s - jnp.max(s, axis=-1, keepdims=True))
        p = e * (anyvis[None] / jnp.sum(e, axis=-1, keepdims=True))
        o_ref[0, heads] = _bdot(p.reshape(hpg * tq, w).astype(BF16), vc_ref[0, gi, :w, :]).reshape(hpg, tq, d)
        pg = jnp.sum(p, axis=0)

        pg_hi = pg.astype(BF16)
        pg_lo = (pg - pg_hi.astype(F32)).astype(BF16)
        wimp = wimp_ref[:nb, :w]
        imp = _dot_nt(wimp, pg_hi) + _dot_nt(wimp, pg_lo)
        imp = jnp.where(visible, jnp.where(forced, BIG, imp), NEG)
        parts = [slice(i, i + LANES) for i in range(0, tq, LANES)]
        blk_f = lax.broadcasted_iota(jnp.int32, (nb, LANES), 0).astype(F32)
        imps = [imp[:, pt] for pt in parts]
        sels = [jnp.zeros((nb, LANES), F32) for _ in parts]
        for _ in range(min(SLC_TOP_N, nb)):
            for i in range(len(parts)):
                m = jnp.max(imps[i], axis=0, keepdims=True)
                first = jnp.min(jnp.where(imps[i] == m, blk_f, float(nb)), axis=0, keepdims=True)
                hit = blk_f == first
                sels[i] = jnp.where(hit, 1.0, sels[i])
                imps[i] = jnp.where(hit, -jnp.inf, imps[i])
        sel = jnp.concatenate(sels, axis=1)
        if nb < ns:
            sel = jnp.concatenate([sel, jnp.zeros((ns - nb, tq), F32)], axis=0)
        sel_ref[0, gi] = jnp.where(sel.T > 0.0, 0.0, NEG).astype(BF16)

    for nv in range(1, nch // CMP_CW + 1):
        pl.when(n_chunks == nv)(functools.partial(body, nv * CMP_CW))


def _cmp_select(q, k_cmp, v_cmp, wimp_t):
    bsz, _, s, d = q.shape
    g = NSA_GROUPS
    ng = CMP_GROUPS
    nch = k_cmp.shape[2]
    ns = wimp_t.shape[0]
    tq = min(CMP_TQ, s)
    hmap = lambda b, gp, qi: (b, gp, qi, 0)
    return pl.pallas_call(
        _cmp_select_kernel,
        grid=(bsz, g // ng, s // tq),
        in_specs=[pl.BlockSpec((1, ng * NSA_HPG, tq, d), hmap),
                  pl.BlockSpec((1, ng, nch, d), lambda b, gp, qi: (b, gp, 0, 0)),
                  pl.BlockSpec((1, ng, nch, d), lambda b, gp, qi: (b, gp, 0, 0)),
                  pl.BlockSpec(wimp_t.shape, lambda b, gp, qi: (0, 0))],
        out_specs=[pl.BlockSpec((1, ng * NSA_HPG, tq, d), hmap),
                   pl.BlockSpec((1, ng, tq, ns), hmap)],
        out_shape=[jax.ShapeDtypeStruct((bsz, NSA_HEADS, s, d), F32),
                   jax.ShapeDtypeStruct((bsz, g, s, ns), BF16)],
        compiler_params=_cparams(("arbitrary", "arbitrary", "arbitrary")),
        name="nsa_cmp_select",
    )(q, k_cmp, v_cmp, wimp_t)


def _nsa_attn_kernel(q_ref, kst_ref, vs_ref, kw_ref, vw_ref, selb_ref, eexp_ref, ocmp_ref, gt_ref,
                     o_ref, m_ref, acc_ref, s_ref, p_ref, a_ref):
    ng = kst_ref.shape[1]
    hpg = q_ref.shape[1] // ng
    tq, d = q_ref.shape[2:]
    rows = hpg * tq
    s0 = pl.program_id(2) * tq
    qpos = s0 + lax.broadcasted_iota(jnp.int32, (tq, 1), 0)
    q4 = [q_ref[0, gi * hpg:(gi + 1) * hpg].reshape(rows, d) for gi in range(ng)]
    q4a = [jnp.concatenate([jnp.concatenate([selb_ref[0, gi]] * hpg, axis=0), q4[gi]], axis=1)
           for gi in range(ng)]

    def scores(gi, t):
        return _bdot(q4a[gi], jnp.concatenate([eexp_ref[t], kst_ref[0, gi, t]], axis=0))

    def flush(gi, t):
        acc_ref[gi] = a_ref[gi] * acc_ref[gi] + _bdot(p_ref[gi], vs_ref[0, gi, t])

    def softmax_step(gi, s):
        m_prev = m_ref[gi]
        m_new = jnp.maximum(m_prev, jnp.max(s, axis=-1, keepdims=True))
        a_ref[gi] = jnp.exp2(m_prev - m_new)
        p_ref[gi] = jnp.exp2(s - m_new).astype(BF16)
        m_ref[gi] = m_new

    m_ref[...] = jnp.full_like(m_ref, NEG)
    acc_ref[...] = jnp.zeros_like(acc_ref)
    p_ref[...] = jnp.zeros_like(p_ref)
    a_ref[...] = jnp.ones_like(a_ref)
    t_diag = s0 // ATT_TK
    for gi in range(ng):
        s_ref[gi] = scores(gi, 0)

    def step(t):
        for gi in range(ng):
            s = s_ref[gi]
            s_next = scores(gi, t + 1)
            flush(gi, jnp.maximum(t - 1, 0))
            softmax_step(gi, s)
            s_ref[gi] = s_next

    def pair_body(i, carry):
        step(2 * i)
        step(2 * i + 1)
        return carry

    lax.fori_loop(0, t_diag // 2, pair_body, 0)
    pl.when(t_diag % 2 == 1)(lambda: step(t_diag - 1))

    span = WINDOW + tq
    w0 = pl.multiple_of(jnp.maximum(s0 - WINDOW, 0), tq)
    rel = qpos - (w0 + lax.broadcasted_iota(jnp.int32, (1, span), 1))
    wbias = jnp.where((rel >= 0) & (rel < WINDOW), 0.0, NEG)
    kpos = t_diag * ATT_TK + lax.broadcasted_iota(jnp.int32, (1, ATT_TK), 1)
    causal = (kpos <= qpos)[None]
    for gi in range(ng):
        sw = _dot_nt(q4[gi], kw_ref[0, gi, pl.ds(w0, span), :])
        flush(gi, jnp.maximum(t_diag - 1, 0))
        s = s_ref[gi].reshape(hpg, tq, ATT_TK)
        softmax_step(gi, jnp.where(causal, s, NEG).reshape(rows, ATT_TK))
        sw = (sw.reshape(hpg, tq, span) + wbias[None]).reshape(rows, span)
        pw = jnp.exp2(sw - jnp.max(sw, axis=-1, keepdims=True)).astype(BF16)
        flush(gi, t_diag)
        accw = _bdot(pw, vw_ref[0, gi, pl.ds(w0, span), :])
        o_win = (accw[:, :d] / accw[:, d:d + 1]).reshape(hpg, tq, d)
        acc = acc_ref[gi]
        o_slc = (acc[:, :d] / acc[:, d:d + 1]).reshape(hpg, tq, d)
        gts = gt_ref[0, :, gi * LANES:(gi + 1) * LANES]
        for h in range(hpg):
            g0, g1, g2 = (gts[:, 3 * h + j:3 * h + j + 1] for j in range(3))
            o_ref[0, gi * hpg + h] = g0 * ocmp_ref[0, gi * hpg + h] + g1 * o_slc[h] + g2 * o_win[h]


def _nsa_attn(q, kst, vst, kw, vw, selb, eexp, o_cmp, gates):
    bsz, _, s, d = q.shape
    ng = ATT_GROUPS
    tq = ATT_TQ
    heads = ng * NSA_HPG
    rows = NSA_HPG * tq
    ns = selb.shape[-1]
    hmap = lambda b, gp, qi: (b, gp, qi, 0)
    once = pl.Buffered(1)
    return pl.pallas_call(
        _nsa_attn_kernel,
        grid=(bsz, NSA_GROUPS // ng, s // tq),
        in_specs=[pl.BlockSpec((1, heads, tq, d), hmap),
                  pl.BlockSpec((1, ng) + kst.shape[2:], lambda b, gp, qi: (b, gp, 0, 0, 0), pipeline_mode=once),
                  pl.BlockSpec((1, ng) + vst.shape[2:], lambda b, gp, qi: (b, gp, 0, 0, 0), pipeline_mode=once),
                  pl.BlockSpec((1, ng) + kw.shape[2:], lambda b, gp, qi: (b, gp, 0, 0), pipeline_mode=once),
                  pl.BlockSpec((1, ng) + vw.shape[2:], lambda b, gp, qi: (b, gp, 0, 0), pipeline_mode=once),
                  pl.BlockSpec((1, ng, tq, ns), hmap),
                  pl.BlockSpec(eexp.shape, lambda b, gp, qi: (0, 0, 0), pipeline_mode=once),
                  pl.BlockSpec((1, heads, tq, d), hmap),
                  pl.BlockSpec((1, tq, ng * LANES), lambda b, gp, qi: (b, qi, gp))],
        out_specs=pl.BlockSpec((1, heads, tq, d), hmap),
        out_shape=jax.ShapeDtypeStruct((bsz, NSA_HEADS, s, d), F32),
        scratch_shapes=[pltpu.VMEM((ng, rows, 1), F32),
                        pltpu.VMEM((ng, rows, 2 * d), F32),
                        pltpu.VMEM((ng, rows, ATT_TK), F32),
                        pltpu.VMEM((ng, rows, ATT_TK), BF16),
                        pltpu.VMEM((ng, rows, 1), F32)],
        compiler_params=_cparams(("arbitrary", "arbitrary", "arbitrary")),
        name="nsa_attn",
    )(q, kst, vst, kw, vw, selb, eexp, o_cmp, gates)


def _rope_tables(pos):
    inv_freq = jnp.power(ROPE_THETA, -jnp.arange(ROT_HALF, dtype=F32) * (2.0 / ROT_DIM))
    ang = pos.astype(F32)[:, None] * inv_freq[None, :]
    cos, sin = jnp.cos(ang), jnp.sin(ang)
    n = pos.shape[0]
    one = jnp.ones((n, NSA_HEAD_DIM - ROT_DIM), F32)
    zero8 = jnp.zeros((n, ROT_HALF), F32)
    zero = jnp.zeros((n, NSA_HEAD_DIM - ROT_DIM), F32)
    c = jnp.concatenate([cos, cos, one], axis=1)
    s1 = jnp.concatenate([-sin, zero8, zero], axis=1)
    s2 = jnp.concatenate([zero8, sin, zero], axis=1)
    rep = LANES // NSA_HEAD_DIM
    return tuple(jnp.tile(t, (1, rep)) for t in (c, s1, s2))


def _importance_weights(ns, nch):
    ratio = SLC_BLOCK // CMP_STRIDE
    span = CMP_BLOCK // CMP_STRIDE
    w = np.zeros((ns, nch), np.float32)
    for j in range(ns):
        for m in range(ratio):
            for n in range(span):
                c = ratio * j + m + n
                if c < nch - 1:
                    w[j, c] += 1.0
    return jnp.asarray(w, dtype=BF16)


def _block_expansion(nt, ns):
    key_blk = (np.arange(nt)[:, None] * ATT_TK + np.arange(ATT_TK)[None, :]) // SLC_BLOCK
    e = (np.arange(ns)[None, :, None] == key_blk[:, None, :]).astype(np.float32)
    return jnp.asarray(e, dtype=BF16)


def _gla_layer(x2, bsz, seq, w_in, w_gate_up, b_gate, norm_g, w_out, ln_g, ln_b, router):
    hk = GLA_HEADS * GLA_DK
    hv = GLA_HEADS * GLA_DV
    cuts = np.cumsum([hk, hk, hv, GLA_GATE_RANK]).tolist()
    wq, wk, wv, wg, wr = jnp.split(w_in, cuts, axis=1)
    w_main = jnp.concatenate([wq, wk, wv, wr], axis=1).astype(BF16)
    w_glow = jnp.pad(wg, ((0, 0), (0, LANES - GLA_GATE_RANK))).astype(BF16)
    w_gu = jnp.pad(w_gate_up, ((0, LANES - GLA_GATE_RANK), (0, 0))).astype(BF16)
    q, k, v, r, la = _gla_proj(x2, w_main, w_glow, w_gu, b_gate.reshape(1, hk))
    sh = lambda t: t.reshape(bsz, seq, t.shape[-1])
    o = _gla_core(sh(q), sh(k), sh(v), sh(la), sh(r), norm_g.reshape(1, GLA_DV))
    return _outproj_ln(o.reshape(bsz * seq, hv), x2, w_out.astype(BF16),
                       ln_g.reshape(1, D_MODEL), ln_b.reshape(1, D_MODEL), router)


def _nsa_layer(x2, bsz, seq, w_in, w_ck1, w_ck2, w_cv1, w_cv2, cmp_pe, w_out, ln_g, ln_b, router):
    h, g, hpg, d = NSA_HEADS, NSA_GROUPS, NSA_HPG, NSA_HEAD_DIM
    gd = g * d
    cuts = np.cumsum([h * d] + [gd] * 6).tolist()
    wq, wkc, wvc, wks, wvs, wkw, wvw, wgt = jnp.split(w_in, cuts, axis=1)
    w_rope = jnp.concatenate([wq, wks, wkw], axis=1).astype(BF16)
    w_plain = jnp.concatenate([wkc, wvc, wvs, wvw], axis=1).astype(BF16)
    w_gates = jnp.pad(wgt.reshape(D_MODEL, g, hpg * 3), ((0, 0), (0, 0), (0, LANES - hpg * 3)))
    w_gates = w_gates.reshape(D_MODEL, g * LANES).astype(BF16)
    rc, rs1, rs2 = _rope_tables(jnp.arange(seq, dtype=jnp.int32))
    q, kst, kw, kc, vc, vs1, vw1, gates = _nsa_proj(x2, w_rope, w_plain, w_gates, rc, rs1, rs2, seq)

    nch = seq // CMP_STRIDE
    pe = cmp_pe.reshape(1, CMP_BLOCK * d)
    half = CMP_STRIDE * d
    pad2 = lambda w: jnp.pad(w, ((0, 0), (0, LANES - d))).astype(BF16)
    cc, cs1, cs2 = _rope_tables(jnp.arange(nch, dtype=jnp.int32) * CMP_STRIDE + (CMP_BLOCK - 1))
    k_cmp, v_cmp = _compress(kc, vc, pe[:, :half], pe[:, half:],
                             w_ck1.astype(BF16), pad2(w_ck2), w_cv1.astype(BF16), pad2(w_cv2), cc, cs1, cs2)

    ns = seq // SLC_BLOCK
    sh = lambda t: t.reshape(bsz, seq, t.shape[-1])
    o_cmp, selb = _cmp_select(q, k_cmp, v_cmp, _importance_weights(ns, nch))

    nt = seq // ATT_TK
    o = _nsa_attn(q, kst, vs1.reshape(bsz, g, nt, ATT_TK, 2 * d), kw, vw1,
                  selb, _block_expansion(nt, ns), o_cmp, sh(gates))
    return _outproj_heads_ln(o, x2, w_out.astype(BF16), ln_g.reshape(1, D_MODEL), ln_b.reshape(1, D_MODEL), router)


def _moe_layer(xa, grp, layer, w_gate, w_up, w_down, ln_g, ln_b):
    wg, wu, wd = _cast_experts(layer, w_gate, w_up, w_down)
    wd = wd.reshape(N_EXPERT_GROUPS, EXPERTS_PER_GROUP * D_FF_EXPERT, D_MODEL)
    dest, g_lo, g_hi = _group_sort_plan(grp)
    return _moe_sorted(xa, dest, g_lo, g_hi, wg, wu, wd, ln_g.reshape(1, D_MODEL), ln_b.reshape(1, D_MODEL))


def kernel(x, gla_w_in, gla_w_gate_up, gla_b_gate, gla_norm_g, gla_w_out, nsa_w_in, nsa_w_cmp_k1, nsa_w_cmp_k2,
           nsa_w_cmp_v1, nsa_w_cmp_v2, nsa_cmp_pe, nsa_w_out, moe_w_router, moe_b_router, moe_w_gate, moe_w_up,
           moe_w_down, ln_g, ln_b):
    bsz, seq, _ = x.shape
    x2 = x.reshape(bsz * seq, D_MODEL)
    w_router_t = moe_w_router.T
    wr_hi = w_router_t.astype(BF16)
    wr_lo = (w_router_t - wr_hi.astype(F32)).astype(BF16)
    router = (wr_hi, wr_lo, moe_b_router.reshape(N_EXPERTS, 1))
    for i in range(DEPTH):
        j = i // 2
        if i % 2 == 0:
            xa, grp = _gla_layer(x2, bsz, seq, gla_w_in[j], gla_w_gate_up[j], gla_b_gate[j], gla_norm_g[j],
                                 gla_w_out[j], ln_g[i, 0], ln_b[i, 0], router)
        else:
            xa, grp = _nsa_layer(x2, bsz, seq, nsa_w_in[j], nsa_w_cmp_k1[j], nsa_w_cmp_k2[j], nsa_w_cmp_v1[j],
                                 nsa_w_cmp_v2[j], nsa_cmp_pe[j], nsa_w_out[j], ln_g[i, 0], ln_b[i, 0], router)
        x2 = _moe_layer(xa, grp, i, moe_w_gate, moe_w_up, moe_w_down, ln_g[i, 1], ln_b[i, 1])
    return x2.reshape(bsz, seq, D_MODEL)
```
